```python
import math
import jax
import jax.numpy as jnp
from jax import lax
import numpy as np

D_MODEL = 2048
BATCH = 4
SEQ = 2048
DEPTH = 1

GRID_W = 64
CTX_LEN = 256
HEAD_DIM = 64
D_HY = 1024
D_NA = D_MODEL - D_HY
NA_HEADS = D_NA // HEAD_DIM
OFF_Q = 3 * D_HY
OFF_K = OFF_Q + D_NA
OFF_V = OFF_K + D_NA
D_PROJ = OFF_V + D_NA
SHORT_CONV = 3
HY_ORDER = 2
FILTER_EMB = 33
FILTER_ORDER = 64
DECAY_TARGET = 1e-2
FAST_DECAY_PCT = 0.3
SLOW_DECAY_PCT = 1.5
NA_KH = 8
NA_KW = 16
NA_QB = 16
NA_KB = 2 * NA_KW
PEER_HEADS = 8
PEER_NKEYS = 128
PEER_EXPERTS = PEER_NKEYS * PEER_NKEYS
PEER_DKEY = 256
PEER_TOPK = 16
PEER_CHUNK = 128
LN_EPS = 1e-5
ALPHA = (2.0 * DEPTH) ** 0.25
BETA = (8.0 * DEPTH) ** -0.25
NEG_INF = -1e30

kernel_name = 'hymba_hyena_natten_peer_dit_block'


def layer_norm(x, g, b):
    xf = x.astype(jnp.float32)
    xc = xf - jnp.mean(xf, -1, keepdims=True)
    var = jnp.mean(xc * xc, -1, keepdims=True)
    return (xc * lax.rsqrt(var + LN_EPS) * g + b).astype(x.dtype)


def rms_norm(x, g):
    xf = x.astype(jnp.float32)
    return (xf * lax.rsqrt(jnp.mean(xf * xf, -1, keepdims=True) + LN_EPS) * g).astype(x.dtype)


def ada_params(cond, w_ada, b_ada):
    m = jax.nn.silu(cond) @ w_ada + b_ada
    return m.reshape(cond.shape[0], 6, 1, D_MODEL)


def modulate(x, shift, scale):
    return x * (1.0 + scale) + shift


def short_conv(z, w, b):
    L = z.shape[1]
    zp = jnp.pad(z, ((0, 0), (1, 1), (0, 0)))
    return zp[:, :L] * w[0] + zp[:, 1:L + 1] * w[1] + zp[:, 2:L + 2] * w[2] + b


def hyena_filters(L, w1, b1, w2, b2, w3, b3, w4, freq):
    t = jnp.linspace(0.0, 1.0, L, dtype=jnp.float32)[:, None]
    bands = (FILTER_EMB - 1) // 2
    w = 2.0 * math.pi * jnp.arange(L, dtype=jnp.float32)[:, None] / L
    f = jnp.linspace(1e-4, bands - 1, bands, dtype=jnp.float32)[None, :]
    z = jnp.concatenate([t, jnp.cos(f * w), -jnp.sin(f * w)], axis=-1)
    h = jnp.sin(freq[0] * (z @ w1 + b1))
    h = jnp.sin(freq[1] * (h @ w2 + b2))
    h = jnp.sin(freq[2] * (h @ w3 + b3))
    h = (h @ w4).astype(jnp.float32).reshape(L, HY_ORDER, 2, D_HY)
    deltas = jnp.abs(jnp.linspace(math.log(DECAY_TARGET) / SLOW_DECAY_PCT,
                                  math.log(DECAY_TARGET) / FAST_DECAY_PCT, D_HY, dtype=jnp.float32))
    h = h * jnp.exp(-t * deltas)[:, None, None, :]
    h = h / jnp.sum(jnp.abs(h), axis=(0, 2), keepdims=True)
    fwd, bwd = h[:, :, 0], h[:, :, 1]
    k_circ = jnp.concatenate([fwd, jnp.zeros_like(fwd[:1]), bwd[:0:-1]], axis=0)
    return jnp.fft.rfft(k_circ, axis=0)


def hyena(z, kf, skip, conv_w, conv_b):
    L = z.shape[1]
    z = short_conv(z, conv_w, conv_b)
    v, x1, x2 = jnp.split(z, 3, axis=-1)

    def long_conv(u, o):
        uf = jnp.fft.rfft(u.astype(jnp.float32), n=2 * L, axis=1)
        y = jnp.fft.irfft(uf * kf[:, o], n=2 * L, axis=1)[:, :L]
        return y.astype(u.dtype) + u * skip[o]

    y = x1 * long_conv(v, 0)
    return x2 * long_conv(y, 1)


def neighbourhood_attention(q, k, v, k_ctx, v_ctx, rpb):
    B, L, H, DH = q.shape
    rows = L // GRID_W
    kh = min(NA_KH, rows)
    ncb = GRID_W // NA_QB
    nk = kh * NA_KB
    r = jnp.arange(rows)
    row_start = jnp.clip(r - kh // 2, 0, rows - kh)
    key_rows = row_start[:, None] + jnp.arange(kh)
    j = jnp.arange(ncb)
    col_start = jnp.clip(j * NA_QB - NA_KW // 2, 0, GRID_W - NA_KB)
    key_cols = col_start[:, None] + jnp.arange(NA_KB)
    tok = (key_rows[:, None, :, None] * GRID_W + key_cols[None, :, None, :]).reshape(rows, ncb, nk)
    k_blk = k[:, tok]
    v_blk = v[:, tok]
    q_blk = q.reshape(B, rows, ncb, NA_QB, H, DH)
    q_col = j[:, None] * NA_QB + jnp.arange(NA_QB)
    win_start = jnp.clip(q_col - NA_KW // 2, 0, GRID_W - NA_KW)
    rel = key_cols[:, None, :] - win_start[:, :, None]
    col_ok = (rel >= 0) & (rel < NA_KW)
    mask = jnp.broadcast_to(col_ok[:, :, None, :], (ncb, NA_QB, kh, NA_KB)).reshape(ncb, NA_QB, nk)
    d_row = key_rows - r[:, None]
    d_col = jnp.clip(key_cols[:, None, :] - q_col[:, :, None] + NA_KW - 1, 0, 2 * NA_KW - 2)
    bias = rpb[:, d_row + NA_KH - 1][..., d_col]
    bias = bias.transpose(0, 1, 3, 4, 2, 5).reshape(H, rows, ncb, NA_QB, nk)
    scale = HEAD_DIM ** -0.5
    s_lat = jnp.einsum('brjqhd,brjkhd->bhrjqk', q_blk, k_blk).astype(jnp.float32) * scale + bias
    s_lat = jnp.where(mask, s_lat, NEG_INF)
    s_ctx = jnp.einsum('brjqhd,bchd->bhrjqc', q_blk, k_ctx).astype(jnp.float32) * scale
    p = jax.nn.softmax(jnp.concatenate([s_lat, s_ctx], axis=-1), axis=-1).astype(v.dtype)
    o = (jnp.einsum('bhrjqk,brjkhd->brjqhd', p[..., :nk], v_blk)
         + jnp.einsum('bhrjqc,bchd->brjqhd', p[..., nk:], v_ctx))
    return o.reshape(B, L, H * DH)


def context_attention(q, k, v):
    B, C, H, DH = q.shape
    s = jnp.einsum('bqhd,bkhd->bhqk', q, k).astype(jnp.float32) * HEAD_DIM ** -0.5
    p = jax.nn.softmax(s, axis=-1).astype(v.dtype)
    return jnp.einsum('bhqk,bkhd->bqhd', p, v).reshape(B, C, H * DH)


def merge_groups(hy_out, na_out, g_hy, g_na, w_out):
    return jnp.concatenate([rms_norm(hy_out, g_hy), rms_norm(na_out, g_na)], axis=-1) @ w_out


def peer(u, w_q, sub_keys, expert_u, expert_v):
    B, L, D = u.shape
    q = (u @ w_q).reshape(B, L, PEER_HEADS, 2, PEER_DKEY // 2)
    s = jnp.einsum('blhpd,hpnd->blhpn', q, sub_keys).astype(jnp.float32)
    s_a, i_a = lax.top_k(s[..., 0, :], PEER_TOPK)
    s_b, i_b = lax.top_k(s[..., 1, :], PEER_TOPK)
    n_cand = PEER_TOPK * PEER_TOPK
    cand_s = (s_a[..., :, None] + s_b[..., None, :]).reshape(B, L, PEER_HEADS, n_cand)
    cand_i = (i_a[..., :, None] * PEER_NKEYS + i_b[..., None, :]).reshape(B, L, PEER_HEADS, n_cand)
    top_s, pos = lax.top_k(cand_s, PEER_TOPK)
    idx = jnp.take_along_axis(cand_i, pos, axis=-1)
    g = jax.nn.softmax(top_s, axis=-1).astype(u.dtype)
    n_chunks = (B * L) // PEER_CHUNK
    sel = PEER_HEADS * PEER_TOPK
    u_c = u.reshape(n_chunks, PEER_CHUNK, D)
    idx_c = idx.reshape(n_chunks, PEER_CHUNK, sel)
    g_c = g.reshape(n_chunks, PEER_CHUNK, sel)

    def chunk(args):
        ut, it, gt = args
        act = jax.nn.gelu(jnp.einsum('td,tkd->tk', ut, expert_u[it]), approximate=False)
        return jnp.einsum('tk,tkd->td', gt * act, expert_v[it])

    return lax.map(chunk, (u_c, idx_c, g_c)).reshape(B, L, D)


def setup_inputs(seed: int = 0) -> dict:
    key = jax.random.key(seed)
    ks = jax.random.split(key, 32)

    def nrm(k, shape, s):
        return jax.random.normal(k, shape, jnp.float32) * s

    col_scale = jnp.concatenate([jnp.full((D_HY,), BETA, jnp.float32),
                                 jnp.ones((OFF_V - D_HY,), jnp.float32),
                                 jnp.full((D_NA,), BETA, jnp.float32)])
    return {
        'x': nrm(ks[0], (BATCH, SEQ, D_MODEL), 1.0),
        'c': nrm(ks[1], (BATCH, D_MODEL), 1.0),
        'ctx': nrm(ks[2], (BATCH, CTX_LEN, D_MODEL), 1.0),
        'c_ctx': nrm(ks[3], (D_MODEL,), 1.0),
        'w_ada': nrm(ks[4], (DEPTH, D_MODEL, 6 * D_MODEL), D_MODEL ** -0.5),
        'b_ada': nrm(ks[5], (DEPTH, 6 * D_MODEL), 0.01),
        'w_in': nrm(ks[6], (DEPTH, D_MODEL, D_PROJ), D_MODEL ** -0.5) * col_scale,
        'conv_w': nrm(ks[7], (DEPTH, SHORT_CONV, 3 * D_HY), SHORT_CONV ** -0.5),
        'conv_b': nrm(ks[8], (DEPTH, 3 * D_HY), 0.01),
        'filt_w1': nrm(ks[9], (DEPTH, FILTER_EMB, FILTER_ORDER), FILTER_EMB ** -0.5),
        'filt_b1': nrm(ks[10], (DEPTH, FILTER_ORDER), 0.1),
        'filt_w2': nrm(ks[11], (DEPTH, FILTER_ORDER, FILTER_ORDER), FILTER_ORDER ** -0.5),
        'filt_b2': nrm(ks[12], (DEPTH, FILTER_ORDER), 0.1),
        'filt_w3': nrm(ks[13], (DEPTH, FILTER_ORDER, FILTER_ORDER), FILTER_ORDER ** -0.5),
        'filt_b3': nrm(ks[14], (DEPTH, FILTER_ORDER), 0.1),
        'filt_w4': nrm(ks[15], (DEPTH, FILTER_ORDER, HY_ORDER * 2 * D_HY), FILTER_ORDER ** -0.5),
        'filt_freq': 1.0 + nrm(ks[16], (DEPTH, 3, FILTER_ORDER), 0.01),
        'hy_skip': nrm(ks[17], (DEPTH, HY_ORDER, D_HY), 0.5),
        'na_rpb': nrm(ks[18], (DEPTH, NA_HEADS, 2 * NA_KH - 1, 2 * NA_KW - 1), 0.02),
        'hy_norm_g': 1.0 + nrm(ks[19], (DEPTH, D_HY), 0.01),
        'na_norm_g': 1.0 + nrm(ks[20], (DEPTH, D_NA), 0.01),
        'w_out': nrm(ks[21], (DEPTH, D_MODEL, D_MODEL), D_MODEL ** -0.5 * BETA),
        'ln1_g': 1.0 + nrm(ks[22], (DEPTH, D_MODEL), 0.01),
        'ln1_b': nrm(ks[23], (DEPTH, D_MODEL), 0.01),
        'peer_wq': nrm(ks[24], (DEPTH, D_MODEL, PEER_HEADS * PEER_DKEY), D_MODEL ** -0.5),
        'peer_keys': nrm(ks[25], (DEPTH, PEER_HEADS, 2, PEER_NKEYS, PEER_DKEY // 2), (PEER_DKEY // 2) ** -0.5),
        'peer_u': nrm(ks[26], (DEPTH, PEER_EXPERTS, D_MODEL), D_MODEL ** -0.5 * BETA),
        'peer_v': nrm(ks[27], (DEPTH, PEER_EXPERTS, D_MODEL), BETA),
        'ln2_g': 1.0 + nrm(ks[28], (DEPTH, D_MODEL), 0.01),
        'ln2_b': nrm(ks[29], (DEPTH, D_MODEL), 0.01),
    }


def reference(x, c, ctx, c_ctx, w_ada, b_ada, w_in, conv_w, conv_b, filt_w1, filt_b1, filt_w2, filt_b2,
              filt_w3, filt_b3, filt_w4, filt_freq, hy_skip, na_rpb, hy_norm_g, na_norm_g, w_out,
              ln1_g, ln1_b, peer_wq, peer_keys, peer_u, peer_v, ln2_g, ln2_b):
    B, L, _ = x.shape
    C = ctx.shape[1]
    for l in range(DEPTH):
        m = ada_params(c, w_ada[l], b_ada[l])
        mc = ada_params(c_ctx[None], w_ada[l], b_ada[l])
        filt = (filt_w1[l], filt_b1[l], filt_w2[l], filt_b2[l], filt_w3[l], filt_b3[l], filt_w4[l], filt_freq[l])

        u = modulate(x, m[:, 0], m[:, 1])
        uc = modulate(ctx, mc[:, 0], mc[:, 1])
        proj = u @ w_in[l]
        kv_c = uc @ w_in[l][:, OFF_K:]
        k_c = kv_c[..., :D_NA].reshape(B, C, NA_HEADS, HEAD_DIM)
        v_c = kv_c[..., D_NA:].reshape(B, C, NA_HEADS, HEAD_DIM)
        hy_out = hyena(proj[..., :OFF_Q], hyena_filters(L, *filt), hy_skip[l], conv_w[l], conv_b[l])
        q = proj[..., OFF_Q:OFF_K].reshape(B, L, NA_HEADS, HEAD_DIM)
        k = proj[..., OFF_K:OFF_V].reshape(B, L, NA_HEADS, HEAD_DIM)
        v = proj[..., OFF_V:].reshape(B, L, NA_HEADS, HEAD_DIM)
        na_out = neighbourhood_attention(q, k, v, k_c, v_c, na_rpb[l])
        y = merge_groups(hy_out, na_out, hy_norm_g[l], na_norm_g[l], w_out[l])
        x_new = layer_norm(ALPHA * x + m[:, 2] * y, ln1_g[l], ln1_b[l])

        f = peer(modulate(x_new, m[:, 3], m[:, 4]), peer_wq[l], peer_keys[l], peer_u[l], peer_v[l])
        x_new = layer_norm(ALPHA * x_new + m[:, 5] * f, ln2_g[l], ln2_b[l])

        if l < DEPTH - 1:
            proj_c = uc @ w_in[l][:, :OFF_K]
            hy_c = hyena(proj_c[..., :OFF_Q], hyena_filters(C, *filt), hy_skip[l], conv_w[l], conv_b[l])
            q_c = proj_c[..., OFF_Q:].reshape(B, C, NA_HEADS, HEAD_DIM)
            att_c = context_attention(q_c, k_c, v_c)
            y_c = merge_groups(hy_c, att_c, hy_norm_g[l], na_norm_g[l], w_out[l])
            ctx = layer_norm(ALPHA * ctx + mc[:, 2] * y_c, ln1_g[l], ln1_b[l])
            f_c = peer(modulate(ctx, mc[:, 3], mc[:, 4]), peer_wq[l], peer_keys[l], peer_u[l], peer_v[l])
            ctx = layer_norm(ALPHA * ctx + mc[:, 5] * f_c, ln2_g[l], ln2_b[l])
        x = x_new
    return x
```

```python
import functools
import math

import jax
import jax.numpy as jnp
from jax import lax
from jax.experimental import pallas as pl
from jax.experimental.pallas import tpu as pltpu

F32 = jnp.float32
BF16 = jnp.bfloat16
HIGHEST = lax.Precision.HIGHEST

GRID_W = 64
HEAD_DIM = 64
D_HY = 1024
HY_ORDER = 2
FILTER_EMB = 33
FILTER_PAD = 128
DECAY_TARGET = 1e-2
FAST_DECAY_PCT = 0.3
SLOW_DECAY_PCT = 1.5
NA_KH = 8
NA_KW = 16
PEER_HEADS = 8
PEER_NKEYS = 128
NKEYS_SHIFT = PEER_NKEYS.bit_length() - 1
assert 1 << NKEYS_SHIFT == PEER_NKEYS
PEER_TOPK = 16
LN_EPS = 1e-5
NEG_INF = -1e30

V7X_VMEM_BYTES = 64 * 1024 * 1024
V7X_LANES = 128


def _vmem_limit(estimate_bytes):
    return int(min(estimate_bytes + 12 * 1024 * 1024, V7X_VMEM_BYTES - 6 * 1024 * 1024))


def _params(semantics, estimate_bytes):
    return pltpu.CompilerParams(dimension_semantics=semantics, vmem_limit_bytes=_vmem_limit(estimate_bytes))


def _ada_kernel(cond_ref, w_ref, b_ref, o_ref):
    s = jax.nn.silu(cond_ref[...])
    o_ref[...] = jnp.dot(s, w_ref[...], precision=HIGHEST, preferred_element_type=F32) + b_ref[...]


def _ada(cond, w, b):
    n, d = cond.shape
    tn = 1024
    return pl.pallas_call(
        _ada_kernel,
        grid=(w.shape[1] // tn,),
        in_specs=[pl.BlockSpec((n, d), lambda j: (0, 0)),
                  pl.BlockSpec((d, tn), lambda j: (0, j)),
                  pl.BlockSpec((1, tn), lambda j: (0, j))],
        out_specs=pl.BlockSpec((n, tn), lambda j: (0, j)),
        out_shape=jax.ShapeDtypeStruct((n, w.shape[1]), F32),
        compiler_params=_params(("arbitrary",), 2 * d * tn * 4),
        name="ada",
    )(cond, w, b.reshape(1, -1))


def _proj_kernel(x_ref, mod_ref, w_ref, o_ref):
    u = x_ref[...] * (1.0 + mod_ref[1:2, :]) + mod_ref[0:1, :]
    o_ref[...] = jnp.dot(u.astype(BF16), w_ref[...], preferred_element_type=F32).astype(o_ref.dtype)


def _proj(x, mod, mod_row, w, out_dtype, tm):
    bsz, seq, d = x.shape
    n = w.shape[1]
    est = d * n * 2 + 2 * tm * d * 4 + 2 * tm * n * 4 + tm * n * 4
    return pl.pallas_call(
        _proj_kernel,
        grid=(bsz, seq // tm),
        in_specs=[pl.BlockSpec((None, tm, d), lambda b, i: (b, i, 0)),
                  pl.BlockSpec((None, 6, d), lambda b, i: (mod_row(b), 0, 0)),
                  pl.BlockSpec((d, n), lambda b, i: (0, 0))],
        out_specs=pl.BlockSpec((None, tm, n), lambda b, i: (b, i, 0)),
        out_shape=jax.ShapeDtypeStruct((bsz, seq, n), out_dtype),
        compiler_params=_params(("arbitrary", "arbitrary"), est),
        name="proj",
    )(x, mod, w)


def _filter_kernel(z_ref, w1_ref, b1_ref, w2_ref, b2_ref, w3_ref, b3_ref, fr_ref, w4f_ref, w4b_ref, dl_ref,
                   ks_ref, kd_ref):
    z = z_ref[...]
    dot = functools.partial(jnp.dot, precision=HIGHEST, preferred_element_type=F32)
    h = jnp.sin(fr_ref[0:1, :] * (dot(z, w1_ref[...]) + b1_ref[...]))
    h = jnp.sin(fr_ref[1:2, :] * (dot(h, w2_ref[...]) + b2_ref[...]))
    h = jnp.sin(fr_ref[2:3, :] * (dot(h, w3_ref[...]) + b3_ref[...]))
    decay = jnp.exp(-z[:, 0:1] * dl_ref[...])
    hf = dot(h, w4f_ref[...]) * decay
    hb = dot(h, w4b_ref[...]) * decay
    norm = jnp.sum(jnp.abs(hf) + jnp.abs(hb), axis=0, keepdims=True)
    hf = hf / norm
    hb = hb / norm
    row = lax.broadcasted_iota(jnp.int32, hb.shape, 0)
    hb = jnp.where(row == 0, 0.0, hb)
    ks_ref[...] = hf + hb
    kd_ref[...] = hb - hf


def _hyena_filters(zfeat, w1, b1, w2, b2, w3, b3, w4, freq, deltas, tc):
    seq = zfeat.shape[0]
    fo = w2.shape[0]
    nct = D_HY // tc
    full = lambda a: pl.BlockSpec(a.shape, lambda o, j: (0,) * a.ndim)
    args = (zfeat, w1, b1.reshape(1, -1), w2, b2.reshape(1, -1), w3, b3.reshape(1, -1), freq)
    out_spec = pl.BlockSpec((None, seq, tc), lambda o, j: (o, 0, j))
    return pl.pallas_call(
        _filter_kernel,
        grid=(HY_ORDER, nct),
        in_specs=[full(a) for a in args] + [
            pl.BlockSpec((fo, tc), lambda o, j: (0, (2 * o) * nct + j)),
            pl.BlockSpec((fo, tc), lambda o, j: (0, (2 * o + 1) * nct + j)),
            pl.BlockSpec((1, tc), lambda o, j: (0, j))],
        out_specs=[out_spec, out_spec],
        out_shape=[jax.ShapeDtypeStruct((HY_ORDER, seq, D_HY), F32)] * 2,
        compiler_params=_params(("arbitrary", "arbitrary"), 12 * seq * tc * 4),
        name="hyena_filter",
    )(*args, w4, w4, deltas)


def _spectrum_kernel(ff_ref, ks_ref, kd_ref, kp_ref, kq_ref):
    seq = ks_ref.shape[0]
    ks = ks_ref[...].astype(BF16)
    kd = kd_ref[...].astype(BF16)
    p = jnp.dot(ff_ref[0:seq, :], ks, preferred_element_type=F32)
    q = jnp.dot(ff_ref[seq:2 * seq, :], kd, preferred_element_type=F32)
    nyq = jnp.dot(ff_ref[seq:seq + 16, :], ks, preferred_element_type=F32)[0:1, :]
    row = lax.broadcasted_iota(jnp.int32, p.shape, 0)
    wf = jnp.where(row == 0, 0.5 / seq, 1.0 / seq)
    kp_ref[...] = p * wf
    kq_ref[...] = jnp.where(row == 0, nyq, q) * wf


def _spectrum(ffwd, ksum, kdiff, tc):
    seq = ksum.shape[1]
    spec = pl.BlockSpec((None, seq, tc), lambda o, j: (o, 0, j))
    return pl.pallas_call(
        _spectrum_kernel,
        grid=(HY_ORDER, D_HY // tc),
        in_specs=[pl.BlockSpec(ffwd.shape, lambda o, j: (0, 0)), spec, spec],
        out_specs=[spec, spec],
        out_shape=[jax.ShapeDtypeStruct(ksum.shape, F32)] * 2,
        compiler_params=_params(("arbitrary", "arbitrary"), ffwd.size * 2 + 12 * seq * tc * 4),
        name="hyena_spectrum",
    )(ffwd, ksum, kdiff)


def _short_conv(z, w, b):
    seq = z.shape[0]
    row = lax.broadcasted_iota(jnp.int32, z.shape, 0)
    prev = jnp.where(row == 0, 0.0, pltpu.roll(z, 1, 0))
    nxt = jnp.where(row == seq - 1, 0.0, pltpu.roll(z, seq - 1, 0))
    return prev * w[0:1, :] + z * w[1:2, :] + nxt * w[2:3, :] + b


def _dft_mul(u, ff_ref, kp_ref, kq_ref, o_ref):
    seq = u.shape[0]
    a = jnp.dot(ff_ref[...], u.astype(BF16), preferred_element_type=F32)
    p, q = a[0:seq], a[seq:2 * seq]
    kp, kq = kp_ref[...], kq_ref[...]
    row0 = lax.broadcasted_iota(jnp.int32, p.shape, 0) == 0
    o_ref[0:seq, :] = jnp.where(row0, kp * p, kp * p + kq * q).astype(o_ref.dtype)
    o_ref[seq:2 * seq, :] = jnp.where(row0, kq * q, kp * q - kq * p).astype(o_ref.dtype)


def _fwd_conv_kernel(z_ref, cw_ref, cb_ref, ff_ref, kp_ref, kq_ref, o_ref):
    _dft_mul(_short_conv(z_ref[...], cw_ref[...], cb_ref[...]), ff_ref, kp_ref, kq_ref, o_ref)


def _fwd_plain_kernel(u_ref, ff_ref, kp_ref, kq_ref, o_ref):
    _dft_mul(u_ref[...], ff_ref, kp_ref, kq_ref, o_ref)


def _inv_first_kernel(y_ref, fi_ref, zv_ref, zx_ref, cwv_ref, cbv_ref, cwx_ref, cbx_ref, skip_ref, o_ref):
    v = _short_conv(zv_ref[...], cwv_ref[...], cbv_ref[...])
    x1 = _short_conv(zx_ref[...], cwx_ref[...], cbx_ref[...])
    y = jnp.dot(fi_ref[...], y_ref[...], preferred_element_type=F32)
    o_ref[...] = x1 * (y + v * skip_ref[...])


def _inv_second_kernel(y_ref, fi_ref, g_ref, zx_ref, cwx_ref, cbx_ref, skip_ref, o_ref):
    x2 = _short_conv(zx_ref[...], cwx_ref[...], cbx_ref[...])
    y = jnp.dot(fi_ref[...], y_ref[...], preferred_element_type=F32)
    o_ref[...] = x2 * (y + g_ref[...] * skip_ref[...])


def _hyena(z, conv_w, conv_b, ffwd, finv, kp, kq, skip, tc):
    bsz, seq, _ = z.shape
    nct = D_HY // tc
    sem = ("arbitrary", "arbitrary")
    zcol = lambda g: pl.BlockSpec((None, seq, tc), lambda b, j: (b, 0, g * nct + j))
    wcol = lambda g: pl.BlockSpec((3, tc), lambda b, j: (0, g * nct + j))
    bcol = lambda g: pl.BlockSpec((1, tc), lambda b, j: (0, g * nct + j))
    act = pl.BlockSpec((None, seq, tc), lambda b, j: (b, 0, j))
    spec2 = pl.BlockSpec((None, 2 * seq, tc), lambda b, j: (b, 0, j))
    kspec = lambda o: pl.BlockSpec((None, seq, tc), lambda b, j: (o, 0, j))
    sspec = lambda o: pl.BlockSpec((None, 1, tc), lambda b, j: (o, 0, j))
    mat = pl.BlockSpec(ffwd.shape, lambda b, j: (0, 0))
    mati = pl.BlockSpec(finv.shape, lambda b, j: (0, 0))
    est = ffwd.size * 2 + 14 * seq * tc * 4
    cb = conv_b.reshape(1, -1)
    skip3 = skip.reshape(HY_ORDER, 1, D_HY)
    spec_shape = jax.ShapeDtypeStruct((bsz, 2 * seq, D_HY), BF16)
    act_shape = jax.ShapeDtypeStruct((bsz, seq, D_HY), F32)

    y1 = pl.pallas_call(
        _fwd_conv_kernel, grid=(bsz, nct),
        in_specs=[zcol(0), wcol(0), bcol(0), mat, kspec(0), kspec(0)],
        out_specs=spec2, out_shape=spec_shape, compiler_params=_params(sem, est), name="hyena_fwd1",
    )(z, conv_w, cb, ffwd, kp, kq)
    g = pl.pallas_call(
        _inv_first_kernel, grid=(bsz, nct),
        in_specs=[spec2, mati, zcol(0), zcol(1), wcol(0), bcol(0), wcol(1), bcol(1), sspec(0)],
        out_specs=act, out_shape=act_shape, compiler_params=_params(sem, est), name="hyena_inv1",
    )(y1, finv, z, z, conv_w, cb, conv_w, cb, skip3)
    y2 = pl.pallas_call(
        _fwd_plain_kernel, grid=(bsz, nct),
        in_specs=[act, mat, kspec(1), kspec(1)],
        out_specs=spec2, out_shape=spec_shape, compiler_params=_params(sem, est), name="hyena_fwd2",
    )(g, ffwd, kp, kq)
    return pl.pallas_call(
        _inv_second_kernel, grid=(bsz, nct),
        in_specs=[spec2, mati, act, zcol(2), wcol(2), bcol(2), sspec(1)],
        out_specs=act, out_shape=act_shape, compiler_params=_params(sem, est), name="hyena_inv2",
    )(y2, finv, g, z, conv_w, cb, skip3)


def _bias_kernel(rpb_ref, o_ref):
    h = pl.program_id(0)
    n_r, n_c = 2 * NA_KH - 1, 2 * NA_KW - 1
    cq = lax.broadcasted_iota(jnp.int32, (GRID_W, GRID_W), 0)
    ck = lax.broadcasted_iota(jnp.int32, (GRID_W, GRID_W), 1)
    rel = ck - jnp.clip(cq - NA_KW // 2, 0, GRID_W - NA_KW)
    ok = (rel >= 0) & (rel < NA_KW)
    dcol = jnp.clip(ck - cq + NA_KW - 1, 0, n_c - 1)
    for r in range(n_r):
        acc = jnp.zeros((GRID_W, GRID_W), F32)
        for c in range(n_c):
            acc = jnp.where(dcol == c, rpb_ref[(h * n_r + r) * n_c + c], acc)
        o_ref[r] = jnp.where(ok, acc, NEG_INF)


def _bias_tables(rpb):
    heads = rpb.shape[0]
    n_r = 2 * NA_KH - 1
    return pl.pallas_call(
        _bias_kernel,
        grid=(heads,),
        in_specs=[pl.BlockSpec(memory_space=pltpu.SMEM)],
        out_specs=pl.BlockSpec((None, n_r, GRID_W, GRID_W), lambda h: (h, 0, 0, 0)),
        out_shape=jax.ShapeDtypeStruct((heads, n_r, GRID_W, GRID_W), F32),
        compiler_params=_params(("arbitrary",), 1 << 20),
        name="na_bias",
    )(rpb.reshape(-1))


def _attn_kernel(q_ref, k_ref, v_ref, kc_ref, vc_ref, bias_ref, o_ref, *, rows, kh):
    r = pl.program_id(1)
    start = pl.multiple_of(jnp.clip(r - kh // 2, 0, rows - kh) * GRID_W, GRID_W)
    nk = kh * GRID_W
    lane = lax.broadcasted_iota(jnp.int32, (1, 2 * HEAD_DIM), 1)
    nt = (((1,), (1,)), ((), ()))
    n_pairs = q_ref.shape[1] // (2 * HEAD_DIM)
    for hp in range(n_pairs):
        cols = slice(hp * 2 * HEAD_DIM, (hp + 1) * 2 * HEAD_DIM)
        qp = q_ref[:, cols] * jnp.asarray(HEAD_DIM ** -0.5, BF16)
        kp = k_ref[pl.ds(start, nk), cols]
        vp = v_ref[pl.ds(start, nk), cols]
        kcp = kc_ref[:, cols]
        vcp = vc_ref[:, cols]
        halves = []
        for e in range(2):
            in_head = (lane >= e * HEAD_DIM) & (lane < (e + 1) * HEAD_DIM)
            qm = jnp.where(in_head, qp, jnp.zeros_like(qp))
            s_lat = lax.dot_general(qm, kp, nt, preferred_element_type=F32) + bias_ref[2 * hp + e]
            s_ctx = lax.dot_general(qm, kcp, nt, preferred_element_type=F32)
            m = jnp.maximum(jnp.max(s_lat, axis=-1, keepdims=True), jnp.max(s_ctx, axis=-1, keepdims=True))
            p_lat = jnp.exp(s_lat - m)
            p_ctx = jnp.exp(s_ctx - m)
            denom = jnp.sum(p_lat, axis=-1, keepdims=True) + jnp.sum(p_ctx, axis=-1, keepdims=True)
            o = (jnp.dot(p_lat.astype(BF16), vp, preferred_element_type=F32)
                 + jnp.dot(p_ctx.astype(BF16), vcp, preferred_element_type=F32))
            halves.append(o / denom)
        o_ref[:, cols] = jnp.where(lane < HEAD_DIM, halves[0], halves[1])


def _attention(qkv, kv_ctx, bands, band_of_row, rows, kh):
    bsz, seq, d3 = qkv.shape
    dn = d3 // 3
    ctx_len = kv_ctx.shape[1]
    heads, _, _, nk = bands.shape
    est = 2 * (2 * seq * dn * 2 + 2 * ctx_len * dn * 2 + heads * GRID_W * nk * 4 + GRID_W * dn * 6)
    return pl.pallas_call(
        functools.partial(_attn_kernel, rows=rows, kh=kh),
        grid=(bsz, rows),
        in_specs=[pl.BlockSpec((None, GRID_W, dn), lambda b, r: (b, r, 0)),
                  pl.BlockSpec((None, seq, dn), lambda b, r: (b, 0, 1)),
                  pl.BlockSpec((None, seq, dn), lambda b, r: (b, 0, 2)),
                  pl.BlockSpec((None, ctx_len, dn), lambda b, r: (b, 0, 0)),
                  pl.BlockSpec((None, ctx_len, dn), lambda b, r: (b, 0, 1)),
                  pl.BlockSpec((heads, None, GRID_W, nk), lambda b, r: (0, band_of_row(r), 0, 0))],
        out_specs=pl.BlockSpec((None, GRID_W, dn), lambda b, r: (b, r, 0)),
        out_shape=jax.ShapeDtypeStruct((bsz, seq, dn), F32),
        compiler_params=_params(("arbitrary", "arbitrary"), est),
        name="na_attention",
    )(qkv, qkv, qkv, kv_ctx, kv_ctx, bands)


def _rms(x, g):
    return x * lax.rsqrt(jnp.mean(x * x, axis=-1, keepdims=True) + LN_EPS) * g


def _layer_norm(x, g, b):
    xc = x - jnp.mean(x, axis=-1, keepdims=True)
    var = jnp.mean(xc * xc, axis=-1, keepdims=True)
    return xc * lax.rsqrt(var + LN_EPS) * g + b


def _merge_kernel(hy_ref, na_ref, x_ref, mod_ref, w_ref, ghy_ref, gna_ref, lg_ref, lb_ref, xo_ref, uo_ref, *, alpha):
    dh = hy_ref.shape[1]
    hy = _rms(hy_ref[...], ghy_ref[...]).astype(BF16)
    na = _rms(na_ref[...], gna_ref[...]).astype(BF16)
    y = (jnp.dot(hy, w_ref[0:dh, :], preferred_element_type=F32)
         + jnp.dot(na, w_ref[dh:, :], preferred_element_type=F32))
    xn = _layer_norm(alpha * x_ref[...] + mod_ref[2:3, :] * y, lg_ref[...], lb_ref[...])
    xo_ref[...] = xn
    uo_ref[...] = (xn * (1.0 + mod_ref[4:5, :]) + mod_ref[3:4, :]).astype(uo_ref.dtype)


def _merge(hy, na, x, mod, w_out, g_hy, g_na, ln_g, ln_b, alpha, tm):
    bsz, seq, d = x.shape
    dh, dn = hy.shape[2], na.shape[2]
    row = lambda a: a.reshape(1, -1)
    vec = lambda n: pl.BlockSpec((1, n), lambda b, i: (0, 0))
    tile = lambda n: pl.BlockSpec((None, tm, n), lambda b, i: (b, i, 0))
    est = d * d * 2 + 2 * tm * (dh + dn + 2 * d) * 4 + 2 * tm * d * 6 + 4 * tm * d * 4
    return pl.pallas_call(
        functools.partial(_merge_kernel, alpha=alpha),
        grid=(bsz, seq // tm),
        in_specs=[tile(dh), tile(dn), tile(d),
                  pl.BlockSpec((None, 6, d), lambda b, i: (b, 0, 0)),
                  pl.BlockSpec((d, d), lambda b, i: (0, 0)),
                  vec(dh), vec(dn), vec(d), vec(d)],
        out_specs=[tile(d), tile(d)],
        out_shape=[jax.ShapeDtypeStruct((bsz, seq, d), F32), jax.ShapeDtypeStruct((bsz, seq, d), BF16)],
        compiler_params=_params(("arbitrary", "arbitrary"), est),
        name="merge",
    )(hy, na, x, mod, w_out, row(g_hy), row(g_na), row(ln_g), row(ln_b))


def _topk_rows(s, k, payload=None):
    n = s.shape[0]
    pos = lax.broadcasted_iota(jnp.int32, s.shape, 0).astype(F32)
    vals, idxs = [], []
    for _ in range(k):
        m = jnp.max(s, axis=0, keepdims=True)
        first = jnp.min(jnp.where(s == m, pos, float(n)), axis=0, keepdims=True)
        hit = pos == first
        vals.append(m)
        if payload is None:
            idxs.append(first)
        else:
            idxs.append(jnp.max(jnp.where(hit, payload, -1.0), axis=0, keepdims=True))
        s = jnp.where(hit, -jnp.inf, s)
    return jnp.concatenate(vals, axis=0), jnp.concatenate(idxs, axis=0)


def _route_kernel(u_ref, wq_ref, keys_ref, idx_ref, gate_ref):
    half = keys_ref.shape[2]
    q = jnp.dot(u_ref[...], wq_ref[...], preferred_element_type=F32)
    nt = (((1,), (1,)), ((), ()))
    s_a = lax.dot_general(keys_ref[0], q[:, 0:half], nt, precision=HIGHEST, preferred_element_type=F32)
    s_b = lax.dot_general(keys_ref[1], q[:, half:2 * half], nt, precision=HIGHEST, preferred_element_type=F32)
    va, ia = _topk_rows(s_a, PEER_TOPK)
    vb, ib = _topk_rows(s_b, PEER_TOPK)
    cand_s = jnp.concatenate([va[i:i + 1, :] + vb for i in range(PEER_TOPK)], axis=0)
    cand_i = jnp.concatenate([ia[i:i + 1, :] * float(PEER_NKEYS) + ib for i in range(PEER_TOPK)], axis=0)
    top_s, top_i = _topk_rows(cand_s, PEER_TOPK, payload=cand_i)
    e = jnp.exp(top_s - top_s[0:1, :])
    idx_ref[...] = top_i.astype(jnp.int32)
    gate_ref[...] = e / jnp.sum(e, axis=0, keepdims=True)


def _route(u, wq, keys, tm):
    t, d = u.shape
    heads, _, nkeys, half = keys.shape
    est = 2 * (tm * d * 2 + d * 2 * half * 2 + 2 * nkeys * half * 4) + 40 * nkeys * tm * 4
    out_spec = pl.BlockSpec((PEER_TOPK, tm), lambda i, h: (h, i))
    return pl.pallas_call(
        _route_kernel,
        grid=(t // tm, heads),
        in_specs=[pl.BlockSpec((tm, d), lambda i, h: (i, 0)),
                  pl.BlockSpec((d, 2 * half), lambda i, h: (0, h)),
                  pl.BlockSpec((None, 2, nkeys, half), lambda i, h: (h, 0, 0, 0))],
        out_specs=[out_spec, out_spec],
        out_shape=[jax.ShapeDtypeStruct((heads * PEER_TOPK, t), jnp.int32),
                   jax.ShapeDtypeStruct((heads * PEER_TOPK, t), F32)],
        compiler_params=_params(("arbitrary", "arbitrary"), est),
        name="peer_route",
    )(u, wq, keys)


def _gate_matrix_kernel(idx_ref, gate_ref, o_ref):
    tw, nsel = idx_ref.shape
    key = lax.broadcasted_iota(jnp.int32, (PEER_NKEYS, nsel), 0)
    nt = (((1,), (1,)), ((), ()))

    def body(t, carry):
        idx = idx_ref[pl.ds(t, 1), :]
        gate = gate_ref[pl.ds(t, 1), :]
        a_hot = jnp.where((idx >> NKEYS_SHIFT) == key, gate, 0.0).astype(BF16)
        b_hot = jnp.where((idx & (PEER_NKEYS - 1)) == key, 1.0, 0.0).astype(BF16)
        o_ref[t] = lax.dot_general(a_hot, b_hot, nt, preferred_element_type=F32).astype(o_ref.dtype)
        return carry

    lax.fori_loop(0, tw, body, 0, unroll=4)


def _gate_matrix(idx, gate, tw):
    t, nsel = idx.shape
    est = 2 * (2 * tw * nsel * 4 + tw * PEER_NKEYS * PEER_NKEYS * 2)
    return pl.pallas_call(
        _gate_matrix_kernel,
        grid=(t // tw,),
        in_specs=[pl.BlockSpec((tw, nsel), lambda i: (i, 0)), pl.BlockSpec((tw, nsel), lambda i: (i, 0))],
        out_specs=pl.BlockSpec((tw, PEER_NKEYS, PEER_NKEYS), lambda i: (i, 0, 0)),
        out_shape=jax.ShapeDtypeStruct((t, PEER_NKEYS, PEER_NKEYS), BF16),
        compiler_params=_params(("arbitrary",), est),
        name="peer_gate_matrix",
    )(idx, gate)


def _peer_kernel(u_ref, eu_ref, ev_ref, w_ref, x_ref, mod_ref, lg_ref, lb_ref, o_ref, acc_ref, *, alpha):
    j = pl.program_id(1)

    @pl.when(j == 0)
    def _():
        acc_ref[...] = jnp.zeros_like(acc_ref)

    z = lax.dot_general(u_ref[...], eu_ref[...], (((1,), (1,)), ((), ())), preferred_element_type=F32)
    act = 0.5 * z * (1.0 + lax.erf(z * (2.0 ** -0.5)))
    h = (w_ref[...].astype(F32) * act).astype(BF16)
    acc_ref[...] += jnp.dot(h, ev_ref[...], preferred_element_type=F32)

    @pl.when(j == pl.num_programs(1) - 1)
    def _():
        r = alpha * x_ref[...] + mod_ref[5:6, :] * acc_ref[...]
        o_ref[...] = _layer_norm(r, lg_ref[...], lb_ref[...])


def _peer(u, eu, ev, w, x, mod, ln_g, ln_b, alpha, seq, tm, te):
    t, d = u.shape
    n_exp = eu.shape[0]
    vec = pl.BlockSpec((1, d), lambda i, j: (0, 0))
    est = 2 * (tm * d * 2 + 2 * te * d * 2 + tm * te * 2 + 2 * tm * d * 4) + tm * d * 4 + 6 * tm * te * 4
    return pl.pallas_call(
        functools.partial(_peer_kernel, alpha=alpha),
        grid=(t // tm, n_exp // te),
        in_specs=[pl.BlockSpec((tm, d), lambda i, j: (i, 0)),
                  pl.BlockSpec((te, d), lambda i, j: (j, 0)),
                  pl.BlockSpec((te, d), lambda i, j: (j, 0)),
                  pl.BlockSpec((tm, te), lambda i, j: (i, j)),
                  pl.BlockSpec((tm, d), lambda i, j: (i, 0)),
                  pl.BlockSpec((None, 6, d), lambda i, j: ((i * tm) // seq, 0, 0)),
                  vec, vec],
        out_specs=pl.BlockSpec((tm, d), lambda i, j: (i, 0)),
        out_shape=jax.ShapeDtypeStruct((t, d), F32),
        scratch_shapes=[pltpu.VMEM((tm, d), F32)],
        compiler_params=_params(("arbitrary", "arbitrary"), est),
        name="peer_experts",
    )(u, eu, ev, w, x, mod, ln_g.reshape(1, -1), ln_b.reshape(1, -1))


def _filter_features(seq):
    t = jnp.linspace(0.0, 1.0, seq, dtype=F32)[:, None]
    bands = (FILTER_EMB - 1) // 2
    w = 2.0 * math.pi * jnp.arange(seq, dtype=F32)[:, None] / seq
    f = jnp.linspace(1e-4, bands - 1, bands, dtype=F32)[None, :]
    z = jnp.concatenate([t, jnp.cos(f * w), -jnp.sin(f * w)], axis=-1)
    return jnp.pad(z, ((0, 0), (0, FILTER_PAD - FILTER_EMB)))


def _dft_matrices(seq):
    n2 = 2 * seq
    f = jnp.arange(seq, dtype=jnp.int32)[:, None]
    n = jnp.arange(seq, dtype=jnp.int32)[None, :]
    ang = ((f * n) % n2).astype(F32) * (2.0 * math.pi / n2)
    sin = jnp.where(f == 0, (1 - 2 * (n % 2)).astype(F32), jnp.sin(ang))
    fwd = jnp.concatenate([jnp.cos(ang), sin], axis=0).astype(BF16)
    return fwd, fwd.T


def kernel(x, c, ctx, c_ctx, w_ada, b_ada, w_in, conv_w, conv_b, filt_w1, filt_b1, filt_w2, filt_b2,
           filt_w3, filt_b3, filt_w4, filt_freq, hy_skip, na_rpb, hy_norm_g, na_norm_g, w_out,
           ln1_g, ln1_b, peer_wq, peer_keys, peer_u, peer_v, ln2_g, ln2_b):
    depth = w_ada.shape[0]
    assert depth == 1, "single-layer block: the context stream is never updated"
    bsz, seq, d = x.shape
    rows = seq // GRID_W
    kh = min(NA_KH, rows)
    assert kh == NA_KH and seq % GRID_W == 0
    alpha = (2.0 * depth) ** 0.25
    d_na = d - D_HY
    off_q = 3 * D_HY
    off_k = off_q + d_na

    n_cond = 8
    cond = jnp.concatenate([c, c_ctx[None], jnp.zeros((n_cond - bsz - 1, d), F32)], axis=0)
    mod = _ada(cond, w_ada[0], b_ada[0]).reshape(n_cond, 6, d)

    w_in_b = w_in[0].astype(BF16)
    z_hy = _proj(x, mod, lambda b: b, w_in_b[:, :off_q], F32, 256)
    qkv = _proj(x, mod, lambda b: b, w_in_b[:, off_q:], BF16, 256)
    kv_ctx = _proj(ctx, mod, lambda b: bsz, w_in_b[:, off_k:], BF16, 256)

    deltas = jnp.abs(jnp.linspace(math.log(DECAY_TARGET) / SLOW_DECAY_PCT,
                                  math.log(DECAY_TARGET) / FAST_DECAY_PCT, D_HY, dtype=F32))[None, :]
    padw = lambda a: jnp.pad(a, ((0, FILTER_PAD - a.shape[0]), (0, FILTER_PAD - a.shape[1])))
    padv = lambda a: jnp.pad(a, ((0, FILTER_PAD - a.shape[0]),))
    w4 = jnp.pad(filt_w4[0], ((0, FILTER_PAD - filt_w4.shape[1]), (0, 0)))
    freq = jnp.pad(filt_freq[0], ((0, 0), (0, FILTER_PAD - filt_freq.shape[2])))
    ksum, kdiff = _hyena_filters(_filter_features(seq), padw(filt_w1[0]), padv(filt_b1[0]), padw(filt_w2[0]),
                                 padv(filt_b2[0]), padw(filt_w3[0]), padv(filt_b3[0]), w4, freq, deltas, 256)
    ffwd, finv = _dft_matrices(seq)
    kp, kq = _spectrum(ffwd, ksum, kdiff, 256)
    hy_out = _hyena(z_hy, conv_w[0], conv_b[0], ffwd, finv, kp, kq, hy_skip[0], 256)

    tables = _bias_tables(na_rpb[0])
    bands = jnp.stack([jnp.concatenate([tables[:, o + i] for i in range(kh)], axis=-1) for o in range(kh)], axis=1)
    band_of_row = lambda r: jnp.clip(r - kh // 2, 0, rows - kh) - r + (NA_KH - 1)
    na_out = _attention(qkv, kv_ctx, bands, band_of_row, rows, kh)

    x_mid, u_peer = _merge(hy_out, na_out, x, mod, w_out[0].astype(BF16), hy_norm_g[0], na_norm_g[0],
                           ln1_g[0], ln1_b[0], alpha, 256)

    t = bsz * seq
    u2 = u_peer.reshape(t, d)
    idx_t, gate_t = _route(u2, peer_wq[0].astype(BF16), peer_keys[0], 256)
    w_gate = _gate_matrix(idx_t.T, gate_t.T, 64).reshape(t, PEER_NKEYS * PEER_NKEYS)
    out = _peer(u2, peer_u[0].astype(BF16), peer_v[0].astype(BF16), w_gate, x_mid.reshape(t, d), mod,
                ln2_g[0], ln2_b[0], alpha, seq, 512, 512)
    return out.reshape(bsz, seq, d)
```

```python
import functools
import math

import numpy as np
import jax
import jax.numpy as jnp
from jax import lax
from jax.experimental import pallas as pl
from jax.experimental.pallas import tpu as pltpu

F32 = jnp.float32
BF16 = jnp.bfloat16
HIGHEST = lax.Precision.HIGHEST

GRID_W = 64
HEAD_DIM = 64
D_HY = 1024
HY_ORDER = 2
FILTER_EMB = 33
FILTER_PAD = 128
DECAY_TARGET = 1e-2
FAST_DECAY_PCT = 0.3
SLOW_DECAY_PCT = 1.5
NA_KH = 8
NA_KW = 16
PEER_HEADS = 8
PEER_NKEYS = 128
NKEYS_SHIFT = PEER_NKEYS.bit_length() - 1
assert 1 << NKEYS_SHIFT == PEER_NKEYS
PEER_TOPK = 16
LN_EPS = 1e-5
NEG_INF = -1e30

V7X_VMEM_BYTES = 64 * 1024 * 1024
V7X_LANES = 128


def _vmem_limit(estimate_bytes):
    return int(min(estimate_bytes + 12 * 1024 * 1024, V7X_VMEM_BYTES - 6 * 1024 * 1024))


def _params(semantics, estimate_bytes):
    return pltpu.CompilerParams(dimension_semantics=semantics, vmem_limit_bytes=_vmem_limit(estimate_bytes))


def _ada_kernel(cond_ref, w_ref, b_ref, o_ref):
    s = jax.nn.silu(cond_ref[...])
    o_ref[...] = jnp.dot(s, w_ref[...], precision=HIGHEST, preferred_element_type=F32) + b_ref[...]


def _ada(cond, w, b):
    n, d = cond.shape
    tn = 1024
    return pl.pallas_call(
        _ada_kernel,
        grid=(w.shape[1] // tn,),
        in_specs=[pl.BlockSpec((n, d), lambda j: (0, 0)),
                  pl.BlockSpec((d, tn), lambda j: (0, j)),
                  pl.BlockSpec((1, tn), lambda j: (0, j))],
        out_specs=pl.BlockSpec((n, tn), lambda j: (0, j)),
        out_shape=jax.ShapeDtypeStruct((n, w.shape[1]), F32),
        compiler_params=_params(("arbitrary",), 2 * d * tn * 4),
        name="ada",
    )(cond, w, b.reshape(1, -1))


def _proj_kernel(x_ref, mod_ref, w_ref, o_ref):
    u = x_ref[...] * (1.0 + mod_ref[1:2, :]) + mod_ref[0:1, :]
    o_ref[...] = jnp.dot(u.astype(BF16), w_ref[...], preferred_element_type=F32).astype(o_ref.dtype)


def _proj(x, mod, mod_row, w, out_dtype, tm):
    bsz, seq, d = x.shape
    n = w.shape[1]
    est = d * n * 2 + 2 * tm * d * 4 + 2 * tm * n * 4 + tm * n * 4
    return pl.pallas_call(
        _proj_kernel,
        grid=(bsz, seq // tm),
        in_specs=[pl.BlockSpec((None, tm, d), lambda b, i: (b, i, 0)),
                  pl.BlockSpec((None, 6, d), lambda b, i: (mod_row(b), 0, 0)),
                  pl.BlockSpec((d, n), lambda b, i: (0, 0))],
        out_specs=pl.BlockSpec((None, tm, n), lambda b, i: (b, i, 0)),
        out_shape=jax.ShapeDtypeStruct((bsz, seq, n), out_dtype),
        compiler_params=_params(("arbitrary", "arbitrary"), est),
        name="proj",
    )(x, mod, w)


def _filter_kernel(z_ref, w1_ref, b1_ref, w2_ref, b2_ref, w3_ref, b3_ref, fr_ref, w4f_ref, w4b_ref, dl_ref,
                   ks_ref, kd_ref, h_ref):
    dot = functools.partial(jnp.dot, precision=HIGHEST, preferred_element_type=F32)

    @pl.when((pl.program_id(0) == 0) & (pl.program_id(1) == 0))
    def _():
        h = jnp.sin(fr_ref[0:1, :] * (dot(z_ref[...], w1_ref[...]) + b1_ref[...]))
        h = jnp.sin(fr_ref[1:2, :] * (dot(h, w2_ref[...]) + b2_ref[...]))
        h_ref[...] = jnp.sin(fr_ref[2:3, :] * (dot(h, w3_ref[...]) + b3_ref[...]))

    h = h_ref[...]
    decay = jnp.exp(-z_ref[:, 0:1] * dl_ref[...])
    hf = dot(h, w4f_ref[...]) * decay
    hb = dot(h, w4b_ref[...]) * decay
    norm = jnp.sum(jnp.abs(hf) + jnp.abs(hb), axis=0, keepdims=True)
    hf = hf / norm
    hb = hb / norm
    row = lax.broadcasted_iota(jnp.int32, hb.shape, 0)
    hb = jnp.where(row == 0, 0.0, hb)
    ks_ref[...] = hf + hb
    kd_ref[...] = hb - hf


def _hyena_filters(zfeat, w1, b1, w2, b2, w3, b3, w4, freq, deltas, tc):
    seq = zfeat.shape[0]
    fo = w2.shape[0]
    nct = D_HY // tc
    full = lambda a: pl.BlockSpec(a.shape, lambda o, j: (0,) * a.ndim)
    args = (zfeat, w1, b1.reshape(1, -1), w2, b2.reshape(1, -1), w3, b3.reshape(1, -1), freq)
    out_spec = pl.BlockSpec((None, seq, tc), lambda o, j: (o, 0, j))
    return pl.pallas_call(
        _filter_kernel,
        grid=(HY_ORDER, nct),
        in_specs=[full(a) for a in args] + [
            pl.BlockSpec((fo, tc), lambda o, j: (0, (2 * o) * nct + j)),
            pl.BlockSpec((fo, tc), lambda o, j: (0, (2 * o + 1) * nct + j)),
            pl.BlockSpec((1, tc), lambda o, j: (0, j))],
        out_specs=[out_spec, out_spec],
        out_shape=[jax.ShapeDtypeStruct((HY_ORDER, seq, D_HY), F32)] * 2,
        scratch_shapes=[pltpu.VMEM((seq, fo), F32)],
        compiler_params=_params(("arbitrary", "arbitrary"), 12 * seq * tc * 4),
        name="hyena_filter",
    )(*args, w4, w4, deltas)


def _spectrum_kernel(ff_ref, ks_ref, kd_ref, kp_ref, kq_ref):
    seq = ks_ref.shape[0]
    ks = ks_ref[...].astype(BF16)
    kd = kd_ref[...].astype(BF16)
    p = jnp.dot(ff_ref[0:seq, :], ks, preferred_element_type=F32)
    q = jnp.dot(ff_ref[seq:2 * seq, :], kd, preferred_element_type=F32)
    nyq = jnp.dot(ff_ref[seq:seq + 16, :], ks, preferred_element_type=F32)[0:1, :]
    row = lax.broadcasted_iota(jnp.int32, p.shape, 0)
    wf = jnp.where(row == 0, 0.5 / seq, 1.0 / seq)
    kp_ref[...] = p * wf
    kq_ref[...] = jnp.where(row == 0, nyq, q) * wf


def _spectrum(ffwd, ksum, kdiff, tc):
    seq = ksum.shape[1]
    spec = pl.BlockSpec((None, seq, tc), lambda o, j: (o, 0, j))
    return pl.pallas_call(
        _spectrum_kernel,
        grid=(HY_ORDER, D_HY // tc),
        in_specs=[pl.BlockSpec(ffwd.shape, lambda o, j: (0, 0)), spec, spec],
        out_specs=[spec, spec],
        out_shape=[jax.ShapeDtypeStruct(ksum.shape, F32)] * 2,
        compiler_params=_params(("arbitrary", "arbitrary"), ffwd.size * 2 + 12 * seq * tc * 4),
        name="hyena_spectrum",
    )(ffwd, ksum, kdiff)


def _short_conv(z, w, b):
    seq = z.shape[0]
    row = lax.broadcasted_iota(jnp.int32, z.shape, 0)
    prev = jnp.where(row == 0, 0.0, pltpu.roll(z, 1, 0))
    nxt = jnp.where(row == seq - 1, 0.0, pltpu.roll(z, seq - 1, 0))
    return prev * w[0:1, :] + z * w[1:2, :] + nxt * w[2:3, :] + b


def _dft_mul(u, ff_ref, kp_ref, kq_ref, o_ref):
    seq = u.shape[0]
    a = jnp.dot(ff_ref[...], u.astype(BF16), preferred_element_type=F32)
    p, q = a[0:seq], a[seq:2 * seq]
    kp, kq = kp_ref[...], kq_ref[...]
    row0 = lax.broadcasted_iota(jnp.int32, p.shape, 0) == 0
    o_ref[0:seq, :] = jnp.where(row0, kp * p, kp * p + kq * q).astype(o_ref.dtype)
    o_ref[seq:2 * seq, :] = jnp.where(row0, kq * q, kp * q - kq * p).astype(o_ref.dtype)


def _fwd_conv_kernel(z_ref, cw_ref, cb_ref, ff_ref, kp_ref, kq_ref, o_ref):
    _dft_mul(_short_conv(z_ref[...], cw_ref[...], cb_ref[...]), ff_ref, kp_ref, kq_ref, o_ref)


def _fwd_plain_kernel(u_ref, ff_ref, kp_ref, kq_ref, o_ref):
    _dft_mul(u_ref[...], ff_ref, kp_ref, kq_ref, o_ref)


def _inv_first_kernel(y_ref, fi_ref, zv_ref, zx_ref, cwv_ref, cbv_ref, cwx_ref, cbx_ref, skip_ref, o_ref):
    v = _short_conv(zv_ref[...], cwv_ref[...], cbv_ref[...])
    x1 = _short_conv(zx_ref[...], cwx_ref[...], cbx_ref[...])
    y = jnp.dot(fi_ref[...], y_ref[...], preferred_element_type=F32)
    o_ref[...] = x1 * (y + v * skip_ref[...])


def _inv_second_kernel(y_ref, fi_ref, g_ref, zx_ref, cwx_ref, cbx_ref, skip_ref, o_ref):
    x2 = _short_conv(zx_ref[...], cwx_ref[...], cbx_ref[...])
    y = jnp.dot(fi_ref[...], y_ref[...], preferred_element_type=F32)
    o_ref[...] = x2 * (y + g_ref[...] * skip_ref[...])


def _hyena(z, conv_w, conv_b, ffwd, finv, kp, kq, skip, tc):
    bsz, seq, _ = z.shape
    nct = D_HY // tc
    sem = ("arbitrary", "arbitrary")
    zcol = lambda g: pl.BlockSpec((None, seq, tc), lambda b, j: (b, 0, g * nct + j))
    wcol = lambda g: pl.BlockSpec((3, tc), lambda b, j: (0, g * nct + j))
    bcol = lambda g: pl.BlockSpec((1, tc), lambda b, j: (0, g * nct + j))
    act = pl.BlockSpec((None, seq, tc), lambda b, j: (b, 0, j))
    spec2 = pl.BlockSpec((None, 2 * seq, tc), lambda b, j: (b, 0, j))
    kspec = lambda o: pl.BlockSpec((None, seq, tc), lambda b, j: (o, 0, j))
    sspec = lambda o: pl.BlockSpec((None, 1, tc), lambda b, j: (o, 0, j))
    mat = pl.BlockSpec(ffwd.shape, lambda b, j: (0, 0))
    mati = pl.BlockSpec(finv.shape, lambda b, j: (0, 0))
    est = ffwd.size * 2 + 14 * seq * tc * 4
    cb = conv_b.reshape(1, -1)
    skip3 = skip.reshape(HY_ORDER, 1, D_HY)
    spec_shape = jax.ShapeDtypeStruct((bsz, 2 * seq, D_HY), BF16)
    act_shape = jax.ShapeDtypeStruct((bsz, seq, D_HY), F32)

    y1 = pl.pallas_call(
        _fwd_conv_kernel, grid=(bsz, nct),
        in_specs=[zcol(0), wcol(0), bcol(0), mat, kspec(0), kspec(0)],
        out_specs=spec2, out_shape=spec_shape, compiler_params=_params(sem, est), name="hyena_fwd1",
    )(z, conv_w, cb, ffwd, kp, kq)
    g = pl.pallas_call(
        _inv_first_kernel, grid=(bsz, nct),
        in_specs=[spec2, mati, zcol(0), zcol(1), wcol(0), bcol(0), wcol(1), bcol(1), sspec(0)],
        out_specs=act, out_shape=act_shape, compiler_params=_params(sem, est), name="hyena_inv1",
    )(y1, finv, z, z, conv_w, cb, conv_w, cb, skip3)
    y2 = pl.pallas_call(
        _fwd_plain_kernel, grid=(bsz, nct),
        in_specs=[act, mat, kspec(1), kspec(1)],
        out_specs=spec2, out_shape=spec_shape, compiler_params=_params(sem, est), name="hyena_fwd2",
    )(g, ffwd, kp, kq)
    return pl.pallas_call(
        _inv_second_kernel, grid=(bsz, nct),
        in_specs=[spec2, mati, act, zcol(2), wcol(2), bcol(2), sspec(1)],
        out_specs=act, out_shape=act_shape, compiler_params=_params(sem, est), name="hyena_inv2",
    )(y2, finv, g, z, conv_w, cb, skip3)


def _bias_kernel(rpb_ref, o_ref):
    h = pl.program_id(0)
    n_r, n_c = 2 * NA_KH - 1, 2 * NA_KW - 1
    cq = lax.broadcasted_iota(jnp.int32, (GRID_W, GRID_W), 0)
    ck = lax.broadcasted_iota(jnp.int32, (GRID_W, GRID_W), 1)
    rel = ck - jnp.clip(cq - NA_KW // 2, 0, GRID_W - NA_KW)
    ok = (rel >= 0) & (rel < NA_KW)
    dcol = jnp.clip(ck - cq + NA_KW - 1, 0, n_c - 1)
    for r in range(n_r):
        acc = jnp.zeros((GRID_W, GRID_W), F32)
        for c in range(n_c):
            acc = jnp.where(dcol == c, rpb_ref[(h * n_r + r) * n_c + c], acc)
        o_ref[r] = jnp.where(ok, acc, NEG_INF)


def _bias_tables(rpb):
    heads = rpb.shape[0]
    n_r = 2 * NA_KH - 1
    return pl.pallas_call(
        _bias_kernel,
        grid=(heads,),
        in_specs=[pl.BlockSpec(memory_space=pltpu.SMEM)],
        out_specs=pl.BlockSpec((None, n_r, GRID_W, GRID_W), lambda h: (h, 0, 0, 0)),
        out_shape=jax.ShapeDtypeStruct((heads, n_r, GRID_W, GRID_W), F32),
        compiler_params=_params(("arbitrary",), 1 << 20),
        name="na_bias",
    )(rpb.reshape(-1))


def _attn_kernel(q_ref, k_ref, v_ref, kc_ref, vc_ref, bias_ref, o_ref, *, rows, kh):
    r = pl.program_id(1)
    start = pl.multiple_of(jnp.clip(r - kh // 2, 0, rows - kh) * GRID_W, GRID_W)
    nk = kh * GRID_W
    lane = lax.broadcasted_iota(jnp.int32, (1, 2 * HEAD_DIM), 1)
    nt = (((1,), (1,)), ((), ()))
    n_pairs = q_ref.shape[1] // (2 * HEAD_DIM)
    for hp in range(n_pairs):
        cols = slice(hp * 2 * HEAD_DIM, (hp + 1) * 2 * HEAD_DIM)
        qp = q_ref[:, cols] * jnp.asarray(HEAD_DIM ** -0.5, BF16)
        kp = k_ref[pl.ds(start, nk), cols]
        vp = v_ref[pl.ds(start, nk), cols]
        kcp = kc_ref[:, cols]
        vcp = vc_ref[:, cols]
        zero = jnp.zeros_like(qp)
        qm = jnp.concatenate([jnp.where(lane < HEAD_DIM, qp, zero), jnp.where(lane < HEAD_DIM, zero, qp)], axis=0)
        bias = jnp.concatenate([bias_ref[2 * hp], bias_ref[2 * hp + 1]], axis=0)
        s_lat = lax.dot_general(qm, kp, nt, preferred_element_type=F32) + bias
        s_ctx = lax.dot_general(qm, kcp, nt, preferred_element_type=F32)
        m = jnp.maximum(jnp.max(s_lat, axis=-1, keepdims=True), jnp.max(s_ctx, axis=-1, keepdims=True))
        p_lat = jnp.exp(s_lat - m)
        p_ctx = jnp.exp(s_ctx - m)
        denom = jnp.sum(p_lat, axis=-1, keepdims=True) + jnp.sum(p_ctx, axis=-1, keepdims=True)
        o = (jnp.dot(p_lat.astype(BF16), vp, preferred_element_type=F32)
             + jnp.dot(p_ctx.astype(BF16), vcp, preferred_element_type=F32)) / denom
        o_ref[:, cols] = jnp.where(lane < HEAD_DIM, o[0:GRID_W], o[GRID_W:2 * GRID_W])


def _attention(qkv, kv_ctx, bands, band_of_row, rows, kh):
    bsz, seq, d3 = qkv.shape
    dn = d3 // 3
    ctx_len = kv_ctx.shape[1]
    heads, _, _, nk = bands.shape
    est = 2 * (2 * seq * dn * 2 + 2 * ctx_len * dn * 2 + heads * GRID_W * nk * 4 + GRID_W * dn * 6)
    return pl.pallas_call(
        functools.partial(_attn_kernel, rows=rows, kh=kh),
        grid=(bsz, rows),
        in_specs=[pl.BlockSpec((None, GRID_W, dn), lambda b, r: (b, r, 0)),
                  pl.BlockSpec((None, seq, dn), lambda b, r: (b, 0, 1)),
                  pl.BlockSpec((None, seq, dn), lambda b, r: (b, 0, 2)),
                  pl.BlockSpec((None, ctx_len, dn), lambda b, r: (b, 0, 0)),
                  pl.BlockSpec((None, ctx_len, dn), lambda b, r: (b, 0, 1)),
                  pl.BlockSpec((heads, None, GRID_W, nk), lambda b, r: (0, band_of_row(r), 0, 0))],
        out_specs=pl.BlockSpec((None, GRID_W, dn), lambda b, r: (b, r, 0)),
        out_shape=jax.ShapeDtypeStruct((bsz, seq, dn), F32),
        compiler_params=_params(("arbitrary", "arbitrary"), est),
        name="na_attention",
    )(qkv, qkv, qkv, kv_ctx, kv_ctx, bands)


def _rms(x, g):
    return x * lax.rsqrt(jnp.mean(x * x, axis=-1, keepdims=True) + LN_EPS) * g


def _layer_norm(x, g, b):
    xc = x - jnp.mean(x, axis=-1, keepdims=True)
    var = jnp.mean(xc * xc, axis=-1, keepdims=True)
    return xc * lax.rsqrt(var + LN_EPS) * g + b


def _merge_kernel(hy_ref, na_ref, x_ref, mod_ref, w_ref, ghy_ref, gna_ref, lg_ref, lb_ref, xo_ref, uo_ref, *, alpha):
    dh = hy_ref.shape[1]
    hy = _rms(hy_ref[...], ghy_ref[...]).astype(BF16)
    na = _rms(na_ref[...], gna_ref[...]).astype(BF16)
    y = (jnp.dot(hy, w_ref[0:dh, :], preferred_element_type=F32)
         + jnp.dot(na, w_ref[dh:, :], preferred_element_type=F32))
    xn = _layer_norm(alpha * x_ref[...] + mod_ref[2:3, :] * y, lg_ref[...], lb_ref[...])
    xo_ref[...] = xn
    uo_ref[...] = (xn * (1.0 + mod_ref[4:5, :]) + mod_ref[3:4, :]).astype(uo_ref.dtype)


def _merge(hy, na, x, mod, w_out, g_hy, g_na, ln_g, ln_b, alpha, tm):
    bsz, seq, d = x.shape
    dh, dn = hy.shape[2], na.shape[2]
    row = lambda a: a.reshape(1, -1)
    vec = lambda n: pl.BlockSpec((1, n), lambda b, i: (0, 0))
    tile = lambda n: pl.BlockSpec((None, tm, n), lambda b, i: (b, i, 0))
    est = d * d * 2 + 2 * tm * (dh + dn + 2 * d) * 4 + 2 * tm * d * 6 + 4 * tm * d * 4
    return pl.pallas_call(
        functools.partial(_merge_kernel, alpha=alpha),
        grid=(bsz, seq // tm),
        in_specs=[tile(dh), tile(dn), tile(d),
                  pl.BlockSpec((None, 6, d), lambda b, i: (b, 0, 0)),
                  pl.BlockSpec((d, d), lambda b, i: (0, 0)),
                  vec(dh), vec(dn), vec(d), vec(d)],
        out_specs=[tile(d), tile(d)],
        out_shape=[jax.ShapeDtypeStruct((bsz, seq, d), F32), jax.ShapeDtypeStruct((bsz, seq, d), BF16)],
        compiler_params=_params(("arbitrary", "arbitrary"), est),
        name="merge",
    )(hy, na, x, mod, w_out, row(g_hy), row(g_na), row(ln_g), row(ln_b))


def _first(a, b):
    return (a[0] > b[0]) | ((a[0] == b[0]) & (a[1] < b[1]))


def _order_pair(xs, i, j):
    a, b = xs[i], xs[j]
    f = _first(a, b)
    xs[i] = (jnp.maximum(a[0], b[0]),) + tuple(jnp.where(f, p, r) for p, r in zip(a[1:], b[1:]))
    xs[j] = (jnp.minimum(a[0], b[0]),) + tuple(jnp.where(f, r, p) for p, r in zip(a[1:], b[1:]))


def _bitonic_merge(xs):
    n = len(xs)
    j = n // 2
    while j >= 1:
        for i in range(n):
            if i & j == 0:
                _order_pair(xs, i, i | j)
        j //= 2
    return xs


def _bitonic_sort(xs):
    n = len(xs)
    k = 2
    while k <= n:
        j = k // 2
        while j >= 1:
            for i in range(n):
                if i & j == 0:
                    if i & k == 0:
                        _order_pair(xs, i, i | j)
                    else:
                        _order_pair(xs, i | j, i)
            j //= 2
        k *= 2
    return xs


def _leading_half(a, b):
    out = []
    for x, y in zip(a, reversed(b)):
        f = _first(x, y)
        out.append((jnp.maximum(x[0], y[0]),) + tuple(jnp.where(f, p, r) for p, r in zip(x[1:], y[1:])))
    return out


def _sorted_top(items, k):
    runs = [_bitonic_sort(items[i:i + k]) for i in range(0, len(items), k)]
    while len(runs) > 1:
        runs = [_bitonic_merge(_leading_half(runs[i], runs[i + 1])) for i in range(0, len(runs), 2)]
    return runs[0]


def _pitch(rows):
    return rows + 8


def _route_kernel(u_ref, wq_ref, keys_ref, idx_ref, gate_ref, s_ref):
    nkeys, half = keys_ref.shape[1], keys_ref.shape[2]
    k = PEER_TOPK
    groups = u_ref.shape[0] // V7X_LANES
    pitch = _pitch(nkeys)
    tile = (groups, V7X_LANES)
    q = jnp.dot(u_ref[...], wq_ref[...], preferred_element_type=F32)
    nt = (((1,), (1,)), ((), ()))
    for p in range(2):
        s = lax.dot_general(keys_ref[p], q[:, p * half:(p + 1) * half], nt, precision=HIGHEST,
                            preferred_element_type=F32)
        for g in range(groups):
            s_ref[p, g * pitch:g * pitch + nkeys, :] = s[:, g * V7X_LANES:(g + 1) * V7X_LANES]

    def scores(p):
        return [(s_ref[p, pl.ds(n, groups, stride=pitch), :], jnp.full(tile, float(n), F32)) for n in range(nkeys)]

    top_a = _sorted_top(scores(0), k)
    top_b = _sorted_top(scores(1), k)

    def cand(i, j):
        return (top_a[i][0] + top_b[j][0], jnp.full(tile, float(i * k + j), F32),
                top_a[i][1] * float(nkeys) + top_b[j][1])

    pad = (jnp.full(tile, -jnp.inf, F32), jnp.full(tile, float(k * k), F32), jnp.zeros(tile, F32))
    run = lambda i: [cand(i, j) for j in range(k // (i + 1))]
    tail = [cand(i, 0) for i in range(k // 2, k)]
    assert k == 16
    r0 = run(0)
    r1 = _bitonic_merge(run(1) + tail[::-1])
    r2 = _bitonic_sort(run(2) + run(3) + run(4) + run(5) + run(6))
    r3 = run(7) + [pad] * (k - 2)
    best = _leading_half(_bitonic_merge(_leading_half(r0, r1)), _bitonic_merge(_leading_half(r2, r3)))

    m = functools.reduce(jnp.maximum, [c[0] for c in best])
    e = [jnp.exp(c[0] - m) for c in best]
    inv = 1.0 / functools.reduce(jnp.add, e)
    for n in range(k):
        idx_ref[n] = best[n][2].astype(jnp.int32)
        gate_ref[n] = e[n] * inv


def _route(u, wq, keys, tm):
    t, d = u.shape
    heads, _, nkeys, half = keys.shape
    groups = tm // V7X_LANES
    est = 2 * (tm * d * 2 + d * 2 * half * 2 + 2 * nkeys * half * 4) + 8 * nkeys * tm * 4
    out_spec = pl.BlockSpec((PEER_TOPK, groups, V7X_LANES), lambda i, h: (h, i, 0))
    return pl.pallas_call(
        _route_kernel,
        grid=(t // tm, heads),
        in_specs=[pl.BlockSpec((tm, d), lambda i, h: (i, 0)),
                  pl.BlockSpec((d, 2 * half), lambda i, h: (0, h)),
                  pl.BlockSpec((None, 2, nkeys, half), lambda i, h: (h, 0, 0, 0))],
        out_specs=[out_spec, out_spec],
        out_shape=[jax.ShapeDtypeStruct((heads * PEER_TOPK, t // V7X_LANES, V7X_LANES), jnp.int32),
                   jax.ShapeDtypeStruct((heads * PEER_TOPK, t // V7X_LANES, V7X_LANES), F32)],
        scratch_shapes=[pltpu.VMEM((2, groups * _pitch(nkeys), V7X_LANES), F32)],
        compiler_params=_params(("arbitrary", "arbitrary"), est),
        name="peer_route",
    )(u, wq, keys)


def _gate_matrix_kernel(idx_ref, gate_ref, o_ref, w_ref):
    tw, nsel = idx_ref.shape
    pitch = _pitch(PEER_NKEYS)
    key = lax.broadcasted_iota(jnp.int32, (PEER_NKEYS, nsel), 0)
    nt = (((1,), (1,)), ((), ()))

    def body(t, carry):
        idx = idx_ref[pl.ds(t, 1), :]
        gate = gate_ref[pl.ds(t, 1), :]
        a_hot = jnp.where((idx >> NKEYS_SHIFT) == key, gate, 0.0).astype(BF16)
        b_hot = jnp.where((idx & (PEER_NKEYS - 1)) == key, 1.0, 0.0).astype(BF16)
        w_ref[pl.ds(pl.multiple_of(t * pitch, 8), PEER_NKEYS), :] = lax.dot_general(
            a_hot, b_hot, nt, preferred_element_type=F32)
        return carry

    lax.fori_loop(0, tw, body, 0, unroll=16)
    for a in range(PEER_NKEYS):
        o_ref[:, a * PEER_NKEYS:(a + 1) * PEER_NKEYS] = w_ref[pl.ds(a, tw, stride=pitch), :].astype(o_ref.dtype)


def _gate_matrix(idx, gate, tw):
    t, nsel = idx.shape
    n_exp = PEER_NKEYS * PEER_NKEYS
    est = 2 * (2 * tw * nsel * 4 + tw * n_exp * 2) + tw * _pitch(PEER_NKEYS) * PEER_NKEYS * 4
    return pl.pallas_call(
        _gate_matrix_kernel,
        grid=(t // tw,),
        in_specs=[pl.BlockSpec((tw, nsel), lambda i: (i, 0)), pl.BlockSpec((tw, nsel), lambda i: (i, 0))],
        out_specs=pl.BlockSpec((tw, n_exp), lambda i: (i, 0)),
        out_shape=jax.ShapeDtypeStruct((t, n_exp), BF16),
        scratch_shapes=[pltpu.VMEM((tw * _pitch(PEER_NKEYS), PEER_NKEYS), F32)],
        compiler_params=_params(("arbitrary",), est),
        name="peer_gate_matrix",
    )(idx, gate)


def _peer_kernel(u_ref, eu_ref, ev_ref, w_ref, x_ref, mod_ref, lg_ref, lb_ref, o_ref, *, alpha):
    j = pl.program_id(1)

    @pl.when(j == 0)
    def _():
        o_ref[...] = jnp.zeros_like(o_ref)

    z = lax.dot_general(u_ref[...], eu_ref[...], (((1,), (1,)), ((), ())), preferred_element_type=F32)
    act = 0.5 * z * (1.0 + lax.erf(z * (2.0 ** -0.5)))
    h = (w_ref[...].astype(F32) * act).astype(BF16)
    o_ref[...] += jnp.dot(h, ev_ref[...], preferred_element_type=F32)

    @pl.when(j == pl.num_programs(1) - 1)
    def _():
        r = alpha * x_ref[...] + mod_ref[5:6, :] * o_ref[...]
        o_ref[...] = _layer_norm(r, lg_ref[...], lb_ref[...])


def _peer(u, eu, ev, w, x, mod, ln_g, ln_b, alpha, seq, tm, te):
    t, d = u.shape
    n_exp = eu.shape[0]
    vec = pl.BlockSpec((1, d), lambda i, j: (0, 0))
    once = pl.Buffered(1)
    est = tm * d * 2 + tm * d * 4 + 2 * (2 * te * d * 2 + tm * te * 2 + tm * d * 4) + 4 * tm * te * 4
    return pl.pallas_call(
        functools.partial(_peer_kernel, alpha=alpha),
        grid=(t // tm, n_exp // te),
        in_specs=[pl.BlockSpec((tm, d), lambda i, j: (i, 0), pipeline_mode=once),
                  pl.BlockSpec((te, d), lambda i, j: (j, 0)),
                  pl.BlockSpec((te, d), lambda i, j: (j, 0)),
                  pl.BlockSpec((tm, te), lambda i, j: (i, j)),
                  pl.BlockSpec((tm, d), lambda i, j: (i, 0), pipeline_mode=once),
                  pl.BlockSpec((None, 6, d), lambda i, j: ((i * tm) // seq, 0, 0)),
                  vec, vec],
        out_specs=pl.BlockSpec((tm, d), lambda i, j: (i, 0)),
        out_shape=jax.ShapeDtypeStruct((t, d), F32),
        compiler_params=_params(("arbitrary", "arbitrary"), est),
        name="peer_experts",
    )(u, eu, ev, w, x, mod, ln_g.reshape(1, -1), ln_b.reshape(1, -1))


def _filter_features(seq):
    t = jnp.linspace(0.0, 1.0, seq, dtype=F32)[:, None]
    bands = (FILTER_EMB - 1) // 2
    w = 2.0 * math.pi * jnp.arange(seq, dtype=F32)[:, None] / seq
    f = jnp.linspace(1e-4, bands - 1, bands, dtype=F32)[None, :]
    z = jnp.concatenate([t, jnp.cos(f * w), -jnp.sin(f * w)], axis=-1)
    return jnp.pad(z, ((0, 0), (0, FILTER_PAD - FILTER_EMB)))


@functools.lru_cache(maxsize=None)
def _dft_matrices(seq):
    n2 = 2 * seq
    f = np.arange(seq, dtype=np.int64)[:, None]
    n = np.arange(seq, dtype=np.int64)[None, :]
    ang = ((f * n) % n2).astype(np.float64) * (2.0 * math.pi / n2)
    sin = np.where(f == 0, (1 - 2 * (n % 2)).astype(np.float64), np.sin(ang))
    return np.concatenate([np.cos(ang), sin], axis=0).astype(np.float32)


def kernel(x, c, ctx, c_ctx, w_ada, b_ada, w_in, conv_w, conv_b, filt_w1, filt_b1, filt_w2, filt_b2,
           filt_w3, filt_b3, filt_w4, filt_freq, hy_skip, na_rpb, hy_norm_g, na_norm_g, w_out,
           ln1_g, ln1_b, peer_wq, peer_keys, peer_u, peer_v, ln2_g, ln2_b):
    depth = w_ada.shape[0]
    assert depth == 1, "single-layer block: the context stream is never updated"
    bsz, seq, d = x.shape
    rows = seq // GRID_W
    kh = min(NA_KH, rows)
    assert kh == NA_KH and seq % GRID_W == 0
    alpha = (2.0 * depth) ** 0.25
    d_na = d - D_HY
    off_q = 3 * D_HY
    off_k = off_q + d_na

    n_cond = 8
    cond = jnp.concatenate([c, c_ctx[None], jnp.zeros((n_cond - bsz - 1, d), F32)], axis=0)
    mod = _ada(cond, w_ada[0], b_ada[0]).reshape(n_cond, 6, d)

    w_in_b = w_in[0].astype(BF16)
    z_hy = _proj(x, mod, lambda b: b, w_in_b[:, :off_q], F32, 256)
    qkv = _proj(x, mod, lambda b: b, w_in_b[:, off_q:], BF16, 256)
    kv_ctx = _proj(ctx, mod, lambda b: bsz, w_in_b[:, off_k:], BF16, 256)

    deltas = jnp.abs(jnp.linspace(math.log(DECAY_TARGET) / SLOW_DECAY_PCT,
                                  math.log(DECAY_TARGET) / FAST_DECAY_PCT, D_HY, dtype=F32))[None, :]
    padw = lambda a: jnp.pad(a, ((0, FILTER_PAD - a.shape[0]), (0, FILTER_PAD - a.shape[1])))
    padv = lambda a: jnp.pad(a, ((0, FILTER_PAD - a.shape[0]),))
    w4 = jnp.pad(filt_w4[0], ((0, FILTER_PAD - filt_w4.shape[1]), (0, 0)))
    freq = jnp.pad(filt_freq[0], ((0, 0), (0, FILTER_PAD - filt_freq.shape[2])))
    ksum, kdiff = _hyena_filters(_filter_features(seq), padw(filt_w1[0]), padv(filt_b1[0]), padw(filt_w2[0]),
                                 padv(filt_b2[0]), padw(filt_w3[0]), padv(filt_b3[0]), w4, freq, deltas, 256)
    ffwd = jnp.asarray(_dft_matrices(seq)).astype(BF16)
    finv = ffwd.T
    kp, kq = _spectrum(ffwd, ksum, kdiff, 256)
    hy_out = _hyena(z_hy, conv_w[0], conv_b[0], ffwd, finv, kp, kq, hy_skip[0], 256)

    tables = _bias_tables(na_rpb[0])
    bands = jnp.stack([jnp.concatenate([tables[:, o + i] for i in range(kh)], axis=-1) for o in range(kh)], axis=1)
    band_of_row = lambda r: jnp.clip(r - kh // 2, 0, rows - kh) - r + (NA_KH - 1)
    na_out = _attention(qkv, kv_ctx, bands, band_of_row, rows, kh)

    x_mid, u_peer = _merge(hy_out, na_out, x, mod, w_out[0].astype(BF16), hy_norm_g[0], na_norm_g[0],
                           ln1_g[0], ln1_b[0], alpha, 256)

    t = bsz * seq
    u2 = u_peer.reshape(t, d)
    idx_t, gate_t = _route(u2, peer_wq[0].astype(BF16), peer_keys[0], 1024)
    w_gate = _gate_matrix(idx_t.reshape(-1, t).T, gate_t.reshape(-1, t).T, 64)
    out = _peer(u2, peer_u[0].astype(BF16), peer_v[0].astype(BF16), w_gate, x_mid.reshape(t, d), mod,
                ln2_g[0], ln2_b[0], alpha, seq, 1024, 512)
    return out.reshape(bsz, seq, d)
```

```python
import functools
import math

import numpy as np
import jax
import jax.numpy as jnp
from jax import lax
from jax.experimental import pallas as pl
from jax.experimental.pallas import tpu as pltpu

F32 = jnp.float32
BF16 = jnp.bfloat16
HIGHEST = lax.Precision.HIGHEST

GRID_W = 64
HEAD_DIM = 64
D_HY = 1024
HY_ORDER = 2
FILTER_EMB = 33
FILTER_PAD = 128
DECAY_TARGET = 1e-2
FAST_DECAY_PCT = 0.3
SLOW_DECAY_PCT = 1.5
NA_KH = 8
NA_KW = 16
PEER_HEADS = 8
PEER_NKEYS = 128
NKEYS_SHIFT = PEER_NKEYS.bit_length() - 1
assert 1 << NKEYS_SHIFT == PEER_NKEYS
PEER_TOPK = 16
LN_EPS = 1e-5
NEG_INF = -1e30

V7X_VMEM_BYTES = 64 * 1024 * 1024
V7X_LANES = 128


def _vmem_limit(estimate_bytes):
    return int(min(estimate_bytes + 12 * 1024 * 1024, V7X_VMEM_BYTES - 6 * 1024 * 1024))


def _params(semantics, estimate_bytes):
    return pltpu.CompilerParams(dimension_semantics=semantics, vmem_limit_bytes=_vmem_limit(estimate_bytes))


def _ada_kernel(cond_ref, w_ref, b_ref, o_ref):
    s = jax.nn.silu(cond_ref[...])
    o_ref[...] = jnp.dot(s, w_ref[...], precision=HIGHEST, preferred_element_type=F32) + b_ref[...]


def _ada(cond, w, b):
    n, d = cond.shape
    tn = 1024
    return pl.pallas_call(
        _ada_kernel,
        grid=(w.shape[1] // tn,),
        in_specs=[pl.BlockSpec((n, d), lambda j: (0, 0)),
                  pl.BlockSpec((d, tn), lambda j: (0, j)),
                  pl.BlockSpec((1, tn), lambda j: (0, j))],
        out_specs=pl.BlockSpec((n, tn), lambda j: (0, j)),
        out_shape=jax.ShapeDtypeStruct((n, w.shape[1]), F32),
        compiler_params=_params(("arbitrary",), 2 * d * tn * 4),
        name="ada",
    )(cond, w, b.reshape(1, -1))


def _proj_kernel(x_ref, mod_ref, w_ref, o_ref):
    u = x_ref[...] * (1.0 + mod_ref[1:2, :]) + mod_ref[0:1, :]
    o_ref[...] = jnp.dot(u.astype(BF16), w_ref[...], preferred_element_type=F32).astype(o_ref.dtype)


def _proj(x, mod, mod_row, w, out_dtype, tm):
    bsz, seq, d = x.shape
    n = w.shape[1]
    est = d * n * 2 + 2 * tm * d * 4 + 2 * tm * n * 4 + tm * n * 4
    return pl.pallas_call(
        _proj_kernel,
        grid=(bsz, seq // tm),
        in_specs=[pl.BlockSpec((None, tm, d), lambda b, i: (b, i, 0)),
                  pl.BlockSpec((None, 6, d), lambda b, i: (mod_row(b), 0, 0)),
                  pl.BlockSpec((d, n), lambda b, i: (0, 0))],
        out_specs=pl.BlockSpec((None, tm, n), lambda b, i: (b, i, 0)),
        out_shape=jax.ShapeDtypeStruct((bsz, seq, n), out_dtype),
        compiler_params=_params(("arbitrary", "arbitrary"), est),
        name="proj",
    )(x, mod, w)


def _filter_kernel(z_ref, w1_ref, b1_ref, w2_ref, b2_ref, w3_ref, b3_ref, fr_ref, w4f_ref, w4b_ref, dl_ref,
                   ks_ref, kd_ref, h_ref):
    dot = functools.partial(jnp.dot, precision=HIGHEST, preferred_element_type=F32)

    @pl.when((pl.program_id(0) == 0) & (pl.program_id(1) == 0))
    def _():
        h = jnp.sin(fr_ref[0:1, :] * (dot(z_ref[...], w1_ref[...]) + b1_ref[...]))
        h = jnp.sin(fr_ref[1:2, :] * (dot(h, w2_ref[...]) + b2_ref[...]))
        h_ref[...] = jnp.sin(fr_ref[2:3, :] * (dot(h, w3_ref[...]) + b3_ref[...]))

    h = h_ref[...]
    decay = jnp.exp(-z_ref[:, 0:1] * dl_ref[...])
    hf = dot(h, w4f_ref[...]) * decay
    hb = dot(h, w4b_ref[...]) * decay
    norm = jnp.sum(jnp.abs(hf) + jnp.abs(hb), axis=0, keepdims=True)
    hf = hf / norm
    hb = hb / norm
    row = lax.broadcasted_iota(jnp.int32, hb.shape, 0)
    hb = jnp.where(row == 0, 0.0, hb)
    ks_ref[...] = hf + hb
    kd_ref[...] = hb - hf


def _hyena_filters(zfeat, w1, b1, w2, b2, w3, b3, w4, freq, deltas, tc):
    seq = zfeat.shape[0]
    fo = w2.shape[0]
    nct = D_HY // tc
    full = lambda a: pl.BlockSpec(a.shape, lambda o, j: (0,) * a.ndim)
    args = (zfeat, w1, b1.reshape(1, -1), w2, b2.reshape(1, -1), w3, b3.reshape(1, -1), freq)
    out_spec = pl.BlockSpec((None, seq, tc), lambda o, j: (o, 0, j))
    return pl.pallas_call(
        _filter_kernel,
        grid=(HY_ORDER, nct),
        in_specs=[full(a) for a in args] + [
            pl.BlockSpec((fo, tc), lambda o, j: (0, (2 * o) * nct + j)),
            pl.BlockSpec((fo, tc), lambda o, j: (0, (2 * o + 1) * nct + j)),
            pl.BlockSpec((1, tc), lambda o, j: (0, j))],
        out_specs=[out_spec, out_spec],
        out_shape=[jax.ShapeDtypeStruct((HY_ORDER, seq, D_HY), F32)] * 2,
        scratch_shapes=[pltpu.VMEM((seq, fo), F32)],
        compiler_params=_params(("arbitrary", "arbitrary"), 12 * seq * tc * 4),
        name="hyena_filter",
    )(*args, w4, w4, deltas)


def _spectrum_kernel(ff_ref, ks_ref, kd_ref, kp_ref, kq_ref):
    seq = ks_ref.shape[0]
    ks = ks_ref[...].astype(BF16)
    kd = kd_ref[...].astype(BF16)
    p = jnp.dot(ff_ref[0:seq, :], ks, preferred_element_type=F32)
    q = jnp.dot(ff_ref[seq:2 * seq, :], kd, preferred_element_type=F32)
    nyq = jnp.dot(ff_ref[seq:seq + 16, :], ks, preferred_element_type=F32)[0:1, :]
    row = lax.broadcasted_iota(jnp.int32, p.shape, 0)
    wf = jnp.where(row == 0, 0.5 / seq, 1.0 / seq)
    kp_ref[...] = p * wf
    kq_ref[...] = jnp.where(row == 0, nyq, q) * wf


def _spectrum(ffwd, ksum, kdiff, tc):
    seq = ksum.shape[1]
    spec = pl.BlockSpec((None, seq, tc), lambda o, j: (o, 0, j))
    return pl.pallas_call(
        _spectrum_kernel,
        grid=(HY_ORDER, D_HY // tc),
        in_specs=[pl.BlockSpec(ffwd.shape, lambda o, j: (0, 0)), spec, spec],
        out_specs=[spec, spec],
        out_shape=[jax.ShapeDtypeStruct(ksum.shape, F32)] * 2,
        compiler_params=_params(("arbitrary", "arbitrary"), ffwd.size * 2 + 12 * seq * tc * 4),
        name="hyena_spectrum",
    )(ffwd, ksum, kdiff)


def _short_conv(z, w, b):
    seq = z.shape[0]
    row = lax.broadcasted_iota(jnp.int32, z.shape, 0)
    prev = jnp.where(row == 0, 0.0, pltpu.roll(z, 1, 0))
    nxt = jnp.where(row == seq - 1, 0.0, pltpu.roll(z, seq - 1, 0))
    return prev * w[0:1, :] + z * w[1:2, :] + nxt * w[2:3, :] + b


def _dft_mul(u, ff_ref, kp_ref, kq_ref, o_ref):
    seq = u.shape[0]
    a = jnp.dot(ff_ref[...], u.astype(BF16), preferred_element_type=F32)
    p, q = a[0:seq], a[seq:2 * seq]
    kp, kq = kp_ref[...], kq_ref[...]
    row0 = lax.broadcasted_iota(jnp.int32, p.shape, 0) == 0
    o_ref[0:seq, :] = jnp.where(row0, kp * p, kp * p + kq * q).astype(o_ref.dtype)
    o_ref[seq:2 * seq, :] = jnp.where(row0, kq * q, kp * q - kq * p).astype(o_ref.dtype)


def _fwd_conv_kernel(z_ref, cw_ref, cb_ref, ff_ref, kp_ref, kq_ref, o_ref):
    _dft_mul(_short_conv(z_ref[...], cw_ref[...], cb_ref[...]), ff_ref, kp_ref, kq_ref, o_ref)


def _fwd_plain_kernel(u_ref, ff_ref, kp_ref, kq_ref, o_ref):
    _dft_mul(u_ref[...], ff_ref, kp_ref, kq_ref, o_ref)


def _inv_first_kernel(y_ref, fi_ref, zv_ref, zx_ref, cwv_ref, cbv_ref, cwx_ref, cbx_ref, skip_ref, o_ref):
    v = _short_conv(zv_ref[...], cwv_ref[...], cbv_ref[...])
    x1 = _short_conv(zx_ref[...], cwx_ref[...], cbx_ref[...])
    y = jnp.dot(fi_ref[...], y_ref[...], preferred_element_type=F32)
    o_ref[...] = x1 * (y + v * skip_ref[...])


def _inv_second_kernel(y_ref, fi_ref, g_ref, zx_ref, cwx_ref, cbx_ref, skip_ref, o_ref):
    x2 = _short_conv(zx_ref[...], cwx_ref[...], cbx_ref[...])
    y = jnp.dot(fi_ref[...], y_ref[...], preferred_element_type=F32)
    o_ref[...] = x2 * (y + g_ref[...] * skip_ref[...])


def _hyena(z, conv_w, conv_b, ffwd, finv, kp, kq, skip, tc):
    bsz, seq, _ = z.shape
    nct = D_HY // tc
    sem = ("arbitrary", "arbitrary")
    zcol = lambda g: pl.BlockSpec((None, seq, tc), lambda b, j: (b, 0, g * nct + j))
    wcol = lambda g: pl.BlockSpec((3, tc), lambda b, j: (0, g * nct + j))
    bcol = lambda g: pl.BlockSpec((1, tc), lambda b, j: (0, g * nct + j))
    act = pl.BlockSpec((None, seq, tc), lambda b, j: (b, 0, j))
    spec2 = pl.BlockSpec((None, 2 * seq, tc), lambda b, j: (b, 0, j))
    kspec = lambda o: pl.BlockSpec((None, seq, tc), lambda b, j: (o, 0, j))
    sspec = lambda o: pl.BlockSpec((None, 1, tc), lambda b, j: (o, 0, j))
    mat = pl.BlockSpec(ffwd.shape, lambda b, j: (0, 0))
    mati = pl.BlockSpec(finv.shape, lambda b, j: (0, 0))
    est = ffwd.size * 2 + 14 * seq * tc * 4
    cb = conv_b.reshape(1, -1)
    skip3 = skip.reshape(HY_ORDER, 1, D_HY)
    spec_shape = jax.ShapeDtypeStruct((bsz, 2 * seq, D_HY), BF16)
    act_shape = jax.ShapeDtypeStruct((bsz, seq, D_HY), F32)

    y1 = pl.pallas_call(
        _fwd_conv_kernel, grid=(bsz, nct),
        in_specs=[zcol(0), wcol(0), bcol(0), mat, kspec(0), kspec(0)],
        out_specs=spec2, out_shape=spec_shape, compiler_params=_params(sem, est), name="hyena_fwd1",
    )(z, conv_w, cb, ffwd, kp, kq)
    g = pl.pallas_call(
        _inv_first_kernel, grid=(bsz, nct),
        in_specs=[spec2, mati, zcol(0), zcol(1), wcol(0), bcol(0), wcol(1), bcol(1), sspec(0)],
        out_specs=act, out_shape=act_shape, compiler_params=_params(sem, est), name="hyena_inv1",
    )(y1, finv, z, z, conv_w, cb, conv_w, cb, skip3)
    y2 = pl.pallas_call(
        _fwd_plain_kernel, grid=(bsz, nct),
        in_specs=[act, mat, kspec(1), kspec(1)],
        out_specs=spec2, out_shape=spec_shape, compiler_params=_params(sem, est), name="hyena_fwd2",
    )(g, ffwd, kp, kq)
    return pl.pallas_call(
        _inv_second_kernel, grid=(bsz, nct),
        in_specs=[spec2, mati, act, zcol(2), wcol(2), bcol(2), sspec(1)],
        out_specs=act, out_shape=act_shape, compiler_params=_params(sem, est), name="hyena_inv2",
    )(y2, finv, g, z, conv_w, cb, skip3)


def _bias_kernel(rpb_ref, o_ref):
    h = pl.program_id(0)
    n_r, n_c = 2 * NA_KH - 1, 2 * NA_KW - 1
    cq = lax.broadcasted_iota(jnp.int32, (GRID_W, GRID_W), 0)
    ck = lax.broadcasted_iota(jnp.int32, (GRID_W, GRID_W), 1)
    rel = ck - jnp.clip(cq - NA_KW // 2, 0, GRID_W - NA_KW)
    ok = (rel >= 0) & (rel < NA_KW)
    dcol = jnp.clip(ck - cq + NA_KW - 1, 0, n_c - 1)
    for r in range(n_r):
        acc = jnp.zeros((GRID_W, GRID_W), F32)
        for c in range(n_c):
            acc = jnp.where(dcol == c, rpb_ref[(h * n_r + r) * n_c + c], acc)
        o_ref[r] = jnp.where(ok, acc, NEG_INF)


def _bias_tables(rpb):
    heads = rpb.shape[0]
    n_r = 2 * NA_KH - 1
    return pl.pallas_call(
        _bias_kernel,
        grid=(heads,),
        in_specs=[pl.BlockSpec(memory_space=pltpu.SMEM)],
        out_specs=pl.BlockSpec((None, n_r, GRID_W, GRID_W), lambda h: (h, 0, 0, 0)),
        out_shape=jax.ShapeDtypeStruct((heads, n_r, GRID_W, GRID_W), F32),
        compiler_params=_params(("arbitrary",), 1 << 20),
        name="na_bias",
    )(rpb.reshape(-1))


def _attn_kernel(q_ref, k_ref, v_ref, kc_ref, vc_ref, bias_ref, o_ref, *, rows, kh):
    r = pl.program_id(1)
    start = pl.multiple_of(jnp.clip(r - kh // 2, 0, rows - kh) * GRID_W, GRID_W)
    nk = kh * GRID_W
    lane = lax.broadcasted_iota(jnp.int32, (1, 2 * HEAD_DIM), 1)
    nt = (((1,), (1,)), ((), ()))
    n_pairs = q_ref.shape[1] // (2 * HEAD_DIM)
    for hp in range(n_pairs):
        cols = slice(hp * 2 * HEAD_DIM, (hp + 1) * 2 * HEAD_DIM)
        qp = q_ref[:, cols] * jnp.asarray(HEAD_DIM ** -0.5, BF16)
        kp = k_ref[pl.ds(start, nk), cols]
        vp = v_ref[pl.ds(start, nk), cols]
        kcp = kc_ref[:, cols]
        vcp = vc_ref[:, cols]
        zero = jnp.zeros_like(qp)
        qm = jnp.concatenate([jnp.where(lane < HEAD_DIM, qp, zero), jnp.where(lane < HEAD_DIM, zero, qp)], axis=0)
        bias = jnp.concatenate([bias_ref[2 * hp], bias_ref[2 * hp + 1]], axis=0)
        s_lat = lax.dot_general(qm, kp, nt, preferred_element_type=F32) + bias
        s_ctx = lax.dot_general(qm, kcp, nt, preferred_element_type=F32)
        m = jnp.maximum(jnp.max(s_lat, axis=-1, keepdims=True), jnp.max(s_ctx, axis=-1, keepdims=True))
        p_lat = jnp.exp(s_lat - m)
        p_ctx = jnp.exp(s_ctx - m)
        denom = jnp.sum(p_lat, axis=-1, keepdims=True) + jnp.sum(p_ctx, axis=-1, keepdims=True)
        o = (jnp.dot(p_lat.astype(BF16), vp, preferred_element_type=F32)
             + jnp.dot(p_ctx.astype(BF16), vcp, preferred_element_type=F32)) / denom
        o_ref[:, cols] = jnp.where(lane < HEAD_DIM, o[0:GRID_W], o[GRID_W:2 * GRID_W])


def _attention(qkv, kv_ctx, bands, band_of_row, rows, kh):
    bsz, seq, d3 = qkv.shape
    dn = d3 // 3
    ctx_len = kv_ctx.shape[1]
    heads, _, _, nk = bands.shape
    est = 2 * (2 * seq * dn * 2 + 2 * ctx_len * dn * 2 + heads * GRID_W * nk * 4 + GRID_W * dn * 6)
    return pl.pallas_call(
        functools.partial(_attn_kernel, rows=rows, kh=kh),
        grid=(bsz, rows),
        in_specs=[pl.BlockSpec((None, GRID_W, dn), lambda b, r: (b, r, 0)),
                  pl.BlockSpec((None, seq, dn), lambda b, r: (b, 0, 1)),
                  pl.BlockSpec((None, seq, dn), lambda b, r: (b, 0, 2)),
                  pl.BlockSpec((None, ctx_len, dn), lambda b, r: (b, 0, 0)),
                  pl.BlockSpec((None, ctx_len, dn), lambda b, r: (b, 0, 1)),
                  pl.BlockSpec((heads, None, GRID_W, nk), lambda b, r: (0, band_of_row(r), 0, 0))],
        out_specs=pl.BlockSpec((None, GRID_W, dn), lambda b, r: (b, r, 0)),
        out_shape=jax.ShapeDtypeStruct((bsz, seq, dn), F32),
        compiler_params=_params(("arbitrary", "arbitrary"), est),
        name="na_attention",
    )(qkv, qkv, qkv, kv_ctx, kv_ctx, bands)


def _rms(x, g):
    return x * lax.rsqrt(jnp.mean(x * x, axis=-1, keepdims=True) + LN_EPS) * g


def _layer_norm(x, g, b):
    xc = x - jnp.mean(x, axis=-1, keepdims=True)
    var = jnp.mean(xc * xc, axis=-1, keepdims=True)
    return xc * lax.rsqrt(var + LN_EPS) * g + b


def _merge_kernel(hy_ref, na_ref, x_ref, mod_ref, w_ref, ghy_ref, gna_ref, lg_ref, lb_ref, xo_ref, uo_ref, *, alpha):
    dh = hy_ref.shape[1]
    hy = _rms(hy_ref[...], ghy_ref[...]).astype(BF16)
    na = _rms(na_ref[...], gna_ref[...]).astype(BF16)
    y = (jnp.dot(hy, w_ref[0:dh, :], preferred_element_type=F32)
         + jnp.dot(na, w_ref[dh:, :], preferred_element_type=F32))
    xn = _layer_norm(alpha * x_ref[...] + mod_ref[2:3, :] * y, lg_ref[...], lb_ref[...])
    xo_ref[...] = xn
    uo_ref[...] = (xn * (1.0 + mod_ref[4:5, :]) + mod_ref[3:4, :]).astype(uo_ref.dtype)


def _merge(hy, na, x, mod, w_out, g_hy, g_na, ln_g, ln_b, alpha, tm):
    bsz, seq, d = x.shape
    dh, dn = hy.shape[2], na.shape[2]
    row = lambda a: a.reshape(1, -1)
    vec = lambda n: pl.BlockSpec((1, n), lambda b, i: (0, 0))
    tile = lambda n: pl.BlockSpec((None, tm, n), lambda b, i: (b, i, 0))
    est = d * d * 2 + 2 * tm * (dh + dn + 2 * d) * 4 + 2 * tm * d * 6 + 4 * tm * d * 4
    return pl.pallas_call(
        functools.partial(_merge_kernel, alpha=alpha),
        grid=(bsz, seq // tm),
        in_specs=[tile(dh), tile(dn), tile(d),
                  pl.BlockSpec((None, 6, d), lambda b, i: (b, 0, 0)),
                  pl.BlockSpec((d, d), lambda b, i: (0, 0)),
                  vec(dh), vec(dn), vec(d), vec(d)],
        out_specs=[tile(d), tile(d)],
        out_shape=[jax.ShapeDtypeStruct((bsz, seq, d), F32), jax.ShapeDtypeStruct((bsz, seq, d), BF16)],
        compiler_params=_params(("arbitrary", "arbitrary"), est),
        name="merge",
    )(hy, na, x, mod, w_out, row(g_hy), row(g_na), row(ln_g), row(ln_b))


def _first(a, b):
    return (a[0] > b[0]) | ((a[0] == b[0]) & (a[1] < b[1]))


def _order_pair(xs, i, j):
    a, b = xs[i], xs[j]
    f = _first(a, b)
    xs[i] = (jnp.maximum(a[0], b[0]),) + tuple(jnp.where(f, p, r) for p, r in zip(a[1:], b[1:]))
    xs[j] = (jnp.minimum(a[0], b[0]),) + tuple(jnp.where(f, r, p) for p, r in zip(a[1:], b[1:]))


def _bitonic_merge(xs):
    n = len(xs)
    j = n // 2
    while j >= 1:
        for i in range(n):
            if i & j == 0:
                _order_pair(xs, i, i | j)
        j //= 2
    return xs


def _bitonic_sort(xs):
    n = len(xs)
    k = 2
    while k <= n:
        j = k // 2
        while j >= 1:
            for i in range(n):
                if i & j == 0:
                    if i & k == 0:
                        _order_pair(xs, i, i | j)
                    else:
                        _order_pair(xs, i | j, i)
            j //= 2
        k *= 2
    return xs


def _leading_half(a, b):
    out = []
    for x, y in zip(a, reversed(b)):
        f = _first(x, y)
        out.append((jnp.maximum(x[0], y[0]),) + tuple(jnp.where(f, p, r) for p, r in zip(x[1:], y[1:])))
    return out


def _sorted_top(items, k):
    runs = [_bitonic_sort(items[i:i + k]) for i in range(0, len(items), k)]
    while len(runs) > 1:
        runs = [_bitonic_merge(_leading_half(runs[i], runs[i + 1])) for i in range(0, len(runs), 2)]
    return runs[0]


def _pitch(rows):
    return rows + 8


def _route_kernel(u_ref, wq_ref, keys_ref, idx_ref, gate_ref, s_ref):
    nkeys, half = keys_ref.shape[1], keys_ref.shape[2]
    k = PEER_TOPK
    groups = u_ref.shape[0] // V7X_LANES
    pitch = _pitch(nkeys)
    tile = (groups, V7X_LANES)
    q = jnp.dot(u_ref[...], wq_ref[...], preferred_element_type=F32)
    nt = (((1,), (1,)), ((), ()))
    for p in range(2):
        s = lax.dot_general(keys_ref[p], q[:, p * half:(p + 1) * half], nt, precision=HIGHEST,
                            preferred_element_type=F32)
        for g in range(groups):
            s_ref[p, g * pitch:g * pitch + nkeys, :] = s[:, g * V7X_LANES:(g + 1) * V7X_LANES]

    def scores(p):
        return [(s_ref[p, pl.ds(n, groups, stride=pitch), :], jnp.full(tile, float(n), F32)) for n in range(nkeys)]

    top_a = _sorted_top(scores(0), k)
    top_b = _sorted_top(scores(1), k)

    def cand(i, j):
        return (top_a[i][0] + top_b[j][0], jnp.full(tile, float(i * k + j), F32),
                top_a[i][1] * float(nkeys) + top_b[j][1])

    pad = (jnp.full(tile, -jnp.inf, F32), jnp.full(tile, float(k * k), F32), jnp.zeros(tile, F32))
    run = lambda i: [cand(i, j) for j in range(k // (i + 1))]
    tail = [cand(i, 0) for i in range(k // 2, k)]
    assert k == 16
    r0 = run(0)
    r1 = _bitonic_merge(run(1) + tail[::-1])
    r2 = _bitonic_sort(run(2) + run(3) + run(4) + run(5) + run(6))
    r3 = run(7) + [pad] * (k - 2)
    best = _leading_half(_bitonic_merge(_leading_half(r0, r1)), _bitonic_merge(_leading_half(r2, r3)))

    m = functools.reduce(jnp.maximum, [c[0] for c in best])
    e = [jnp.exp(c[0] - m) for c in best]
    inv = 1.0 / functools.reduce(jnp.add, e)
    for n in range(k):
        idx_ref[n] = best[n][2].astype(jnp.int32)
        gate_ref[n] = e[n] * inv


def _route(u, wq, keys, tm):
    t, d = u.shape
    heads, _, nkeys, half = keys.shape
    groups = tm // V7X_LANES
    est = 2 * (tm * d * 2 + d * 2 * half * 2 + 2 * nkeys * half * 4) + 8 * nkeys * tm * 4
    out_spec = pl.BlockSpec((PEER_TOPK, groups, V7X_LANES), lambda i, h: (h, i, 0))
    return pl.pallas_call(
        _route_kernel,
        grid=(t // tm, heads),
        in_specs=[pl.BlockSpec((tm, d), lambda i, h: (i, 0)),
                  pl.BlockSpec((d, 2 * half), lambda i, h: (0, h)),
                  pl.BlockSpec((None, 2, nkeys, half), lambda i, h: (h, 0, 0, 0))],
        out_specs=[out_spec, out_spec],
        out_shape=[jax.ShapeDtypeStruct((heads * PEER_TOPK, t // V7X_LANES, V7X_LANES), jnp.int32),
                   jax.ShapeDtypeStruct((heads * PEER_TOPK, t // V7X_LANES, V7X_LANES), F32)],
        scratch_shapes=[pltpu.VMEM((2, groups * _pitch(nkeys), V7X_LANES), F32)],
        compiler_params=_params(("arbitrary", "arbitrary"), est),
        name="peer_route",
    )(u, wq, keys)


def _gate_matrix_kernel(idx_ref, gate_ref, o_ref, w_ref):
    tw, nsel = idx_ref.shape
    pitch = _pitch(PEER_NKEYS)
    key = lax.broadcasted_iota(jnp.int32, (PEER_NKEYS, nsel), 0)
    nt = (((1,), (1,)), ((), ()))

    def body(t, carry):
        idx = idx_ref[pl.ds(t, 1), :]
        gate = gate_ref[pl.ds(t, 1), :]
        a_hot = jnp.where((idx >> NKEYS_SHIFT) == key, gate, 0.0).astype(BF16)
        b_hot = jnp.where((idx & (PEER_NKEYS - 1)) == key, 1.0, 0.0).astype(BF16)
        w_ref[pl.ds(pl.multiple_of(t * pitch, 8), PEER_NKEYS), :] = lax.dot_general(
            a_hot, b_hot, nt, preferred_element_type=F32)
        return carry

    lax.fori_loop(0, tw, body, 0, unroll=16)
    for a in range(PEER_NKEYS):
        o_ref[:, a * PEER_NKEYS:(a + 1) * PEER_NKEYS] = w_ref[pl.ds(a, tw, stride=pitch), :].astype(o_ref.dtype)


def _gate_matrix(idx, gate, tw):
    t, nsel = idx.shape
    n_exp = PEER_NKEYS * PEER_NKEYS
    est = 2 * (2 * tw * nsel * 4 + tw * n_exp * 2) + tw * _pitch(PEER_NKEYS) * PEER_NKEYS * 4
    return pl.pallas_call(
        _gate_matrix_kernel,
        grid=(t // tw,),
        in_specs=[pl.BlockSpec((tw, nsel), lambda i: (i, 0)), pl.BlockSpec((tw, nsel), lambda i: (i, 0))],
        out_specs=pl.BlockSpec((tw, n_exp), lambda i: (i, 0)),
        out_shape=jax.ShapeDtypeStruct((t, n_exp), BF16),
        scratch_shapes=[pltpu.VMEM((tw * _pitch(PEER_NKEYS), PEER_NKEYS), F32)],
        compiler_params=_params(("arbitrary",), est),
        name="peer_gate_matrix",
    )(idx, gate)


def _peer_kernel(u_ref, eut_ref, ev_ref, w_ref, o_ref, *, n_chunk):
    @pl.when(pl.program_id(1) == 0)
    def _():
        o_ref[...] = jnp.zeros_like(o_ref)

    z = jnp.dot(u_ref[...], eut_ref[...], preferred_element_type=F32)
    act = 0.5 * z * (1.0 + lax.erf(z * (2.0 ** -0.5)))
    h = (w_ref[...].astype(F32) * act).astype(BF16)
    for c in range(0, o_ref.shape[1], n_chunk):
        o_ref[:, c:c + n_chunk] += jnp.dot(h, ev_ref[:, c:c + n_chunk], preferred_element_type=F32)


def _final_ln_kernel(x_ref, f_ref, mod_ref, lg_ref, lb_ref, o_ref, *, alpha):
    r = alpha * x_ref[...] + mod_ref[5:6, :] * f_ref[...]
    o_ref[...] = _layer_norm(r, lg_ref[...], lb_ref[...])


def _final_ln(x, f, mod, ln_g, ln_b, alpha, seq, tm):
    t, d = x.shape
    tile = pl.BlockSpec((tm, d), lambda i: (i, 0))
    vec = pl.BlockSpec((1, d), lambda i: (0, 0))
    return pl.pallas_call(
        functools.partial(_final_ln_kernel, alpha=alpha),
        grid=(t // tm,),
        in_specs=[tile, tile, pl.BlockSpec((None, 6, d), lambda i: ((i * tm) // seq, 0, 0)), vec, vec],
        out_specs=tile,
        out_shape=jax.ShapeDtypeStruct((t, d), F32),
        compiler_params=_params(("arbitrary",), 8 * tm * d * 4),
        name="final_ln",
    )(x, f, mod, ln_g.reshape(1, -1), ln_b.reshape(1, -1))


def _peer(u, eut, ev, w, tm, te, n_chunk):
    t, d = u.shape
    n_exp = ev.shape[0]
    once = pl.Buffered(1)
    est = tm * d * 2 + tm * d * 4 + 2 * (2 * te * d * 2 + tm * te * 2) + 4 * tm * te * 4 + tm * n_chunk * 4
    return pl.pallas_call(
        functools.partial(_peer_kernel, n_chunk=n_chunk),
        grid=(t // tm, n_exp // te),
        in_specs=[pl.BlockSpec((tm, d), lambda i, j: (i, 0), pipeline_mode=once),
                  pl.BlockSpec((d, te), lambda i, j: (0, j)),
                  pl.BlockSpec((te, d), lambda i, j: (j, 0)),
                  pl.BlockSpec((tm, te), lambda i, j: (i, j))],
        out_specs=pl.BlockSpec((tm, d), lambda i, j: (i, 0), pipeline_mode=once),
        out_shape=jax.ShapeDtypeStruct((t, d), F32),
        compiler_params=_params(("arbitrary", "arbitrary"), est),
        name="peer_experts",
    )(u, eut, ev, w)


def _filter_features(seq):
    t = jnp.linspace(0.0, 1.0, seq, dtype=F32)[:, None]
    bands = (FILTER_EMB - 1) // 2
    w = 2.0 * math.pi * jnp.arange(seq, dtype=F32)[:, None] / seq
    f = jnp.linspace(1e-4, bands - 1, bands, dtype=F32)[None, :]
    z = jnp.concatenate([t, jnp.cos(f * w), -jnp.sin(f * w)], axis=-1)
    return jnp.pad(z, ((0, 0), (0, FILTER_PAD - FILTER_EMB)))


@functools.lru_cache(maxsize=None)
def _dft_matrices(seq):
    n2 = 2 * seq
    f = np.arange(seq, dtype=np.int64)[:, None]
    n = np.arange(seq, dtype=np.int64)[None, :]
    ang = ((f * n) % n2).astype(np.float64) * (2.0 * math.pi / n2)
    sin = np.where(f == 0, (1 - 2 * (n % 2)).astype(np.float64), np.sin(ang))
    return np.concatenate([np.cos(ang), sin], axis=0).astype(np.float32)


def kernel(x, c, ctx, c_ctx, w_ada, b_ada, w_in, conv_w, conv_b, filt_w1, filt_b1, filt_w2, filt_b2,
           filt_w3, filt_b3, filt_w4, filt_freq, hy_skip, na_rpb, hy_norm_g, na_norm_g, w_out,
           ln1_g, ln1_b, peer_wq, peer_keys, peer_u, peer_v, ln2_g, ln2_b):
    depth = w_ada.shape[0]
    assert depth == 1, "single-layer block: the context stream is never updated"
    bsz, seq, d = x.shape
    rows = seq // GRID_W
    kh = min(NA_KH, rows)
    assert kh == NA_KH and seq % GRID_W == 0
    alpha = (2.0 * depth) ** 0.25
    d_na = d - D_HY
    off_q = 3 * D_HY
    off_k = off_q + d_na

    n_cond = 8
    cond = jnp.concatenate([c, c_ctx[None], jnp.zeros((n_cond - bsz - 1, d), F32)], axis=0)
    mod = _ada(cond, w_ada[0], b_ada[0]).reshape(n_cond, 6, d)

    w_in_b = w_in[0].astype(BF16)
    z_hy = _proj(x, mod, lambda b: b, w_in_b[:, :off_q], F32, 512)
    qkv = _proj(x, mod, lambda b: b, w_in_b[:, off_q:], BF16, 512)
    kv_ctx = _proj(ctx, mod, lambda b: bsz, w_in_b[:, off_k:], BF16, 256)

    deltas = jnp.abs(jnp.linspace(math.log(DECAY_TARGET) / SLOW_DECAY_PCT,
                                  math.log(DECAY_TARGET) / FAST_DECAY_PCT, D_HY, dtype=F32))[None, :]
    padw = lambda a: jnp.pad(a, ((0, FILTER_PAD - a.shape[0]), (0, FILTER_PAD - a.shape[1])))
    padv = lambda a: jnp.pad(a, ((0, FILTER_PAD - a.shape[0]),))
    w4 = jnp.pad(filt_w4[0], ((0, FILTER_PAD - filt_w4.shape[1]), (0, 0)))
    freq = jnp.pad(filt_freq[0], ((0, 0), (0, FILTER_PAD - filt_freq.shape[2])))
    ksum, kdiff = _hyena_filters(_filter_features(seq), padw(filt_w1[0]), padv(filt_b1[0]), padw(filt_w2[0]),
                                 padv(filt_b2[0]), padw(filt_w3[0]), padv(filt_b3[0]), w4, freq, deltas, 256)
    ffwd = jnp.asarray(_dft_matrices(seq)).astype(BF16)
    finv = ffwd.T
    kp, kq = _spectrum(ffwd, ksum, kdiff, 256)
    hy_out = _hyena(z_hy, conv_w[0], conv_b[0], ffwd, finv, kp, kq, hy_skip[0], 256)

    tables = _bias_tables(na_rpb[0])
    bands = jnp.stack([jnp.concatenate([tables[:, o + i] for i in range(kh)], axis=-1) for o in range(kh)], axis=1)
    band_of_row = lambda r: jnp.clip(r - kh // 2, 0, rows - kh) - r + (NA_KH - 1)
    na_out = _attention(qkv, kv_ctx, bands, band_of_row, rows, kh)

    x_mid, u_peer = _merge(hy_out, na_out, x, mod, w_out[0].astype(BF16), hy_norm_g[0], na_norm_g[0],
                           ln1_g[0], ln1_b[0], alpha, 256)

    t = bsz * seq
    u2 = u_peer.reshape(t, d)
    idx_t, gate_t = _route(u2, peer_wq[0].astype(BF16), peer_keys[0], 1024)
    w_gate = _gate_matrix(idx_t.reshape(-1, t).T, gate_t.reshape(-1, t).T, 64)
    f = _peer(u2, peer_u[0].astype(BF16).T, peer_v[0].astype(BF16), w_gate, 2048, 512, 512)
    out = _final_ln(x_mid.reshape(t, d), f, mod, ln2_g[0], ln2_b[0], alpha, seq, 512)
    return out.reshape(bsz, seq, d)
```

```python
import functools
import math

import numpy as np
import jax
import jax.numpy as jnp
from jax import lax
from jax.experimental import pallas as pl
from jax.experimental.pallas import tpu as pltpu

F32 = jnp.float32
BF16 = jnp.bfloat16
HIGHEST = lax.Precision.HIGHEST

GRID_W = 64
HEAD_DIM = 64
D_HY = 1024
HY_ORDER = 2
FILTER_EMB = 33
FILTER_PAD = 128
DECAY_TARGET = 1e-2
FAST_DECAY_PCT = 0.3
SLOW_DECAY_PCT = 1.5
NA_KH = 8
NA_KW = 16
PEER_HEADS = 8
PEER_NKEYS = 128
NKEYS_SHIFT = PEER_NKEYS.bit_length() - 1
assert 1 << NKEYS_SHIFT == PEER_NKEYS
PEER_TOPK = 16
LN_EPS = 1e-5
NEG_INF = -1e30

V7X_VMEM_BYTES = 64 * 1024 * 1024
V7X_LANES = 128


def _vmem_limit(estimate_bytes):
    return int(min(estimate_bytes + 12 * 1024 * 1024, V7X_VMEM_BYTES - 6 * 1024 * 1024))


def _params(semantics, estimate_bytes):
    return pltpu.CompilerParams(dimension_semantics=semantics, vmem_limit_bytes=_vmem_limit(estimate_bytes))


def _ada_kernel(cond_ref, w_ref, b_ref, o_ref):
    s = jax.nn.silu(cond_ref[...])
    o_ref[...] = jnp.dot(s, w_ref[...], precision=HIGHEST, preferred_element_type=F32) + b_ref[...]


def _ada(cond, w, b):
    n, d = cond.shape
    tn = 1024
    return pl.pallas_call(
        _ada_kernel,
        grid=(w.shape[1] // tn,),
        in_specs=[pl.BlockSpec((n, d), lambda j: (0, 0)),
                  pl.BlockSpec((d, tn), lambda j: (0, j)),
                  pl.BlockSpec((1, tn), lambda j: (0, j))],
        out_specs=pl.BlockSpec((n, tn), lambda j: (0, j)),
        out_shape=jax.ShapeDtypeStruct((n, w.shape[1]), F32),
        compiler_params=_params(("arbitrary",), 2 * d * tn * 4),
        name="ada",
    )(cond, w, b.reshape(1, -1))


def _proj_kernel(x_ref, mod_ref, w_ref, o_ref):
    u = x_ref[...] * (1.0 + mod_ref[1:2, :]) + mod_ref[0:1, :]
    o_ref[...] = jnp.dot(u.astype(BF16), w_ref[...], preferred_element_type=F32).astype(o_ref.dtype)


def _proj(x, mod, mod_row, w, out_dtype, tm):
    bsz, seq, d = x.shape
    n = w.shape[1]
    est = d * n * 2 + 2 * tm * d * 4 + 2 * tm * n * 4 + tm * n * 4
    return pl.pallas_call(
        _proj_kernel,
        grid=(bsz, seq // tm),
        in_specs=[pl.BlockSpec((None, tm, d), lambda b, i: (b, i, 0)),
                  pl.BlockSpec((None, 6, d), lambda b, i: (mod_row(b), 0, 0)),
                  pl.BlockSpec((d, n), lambda b, i: (0, 0))],
        out_specs=pl.BlockSpec((None, tm, n), lambda b, i: (b, i, 0)),
        out_shape=jax.ShapeDtypeStruct((bsz, seq, n), out_dtype),
        compiler_params=_params(("arbitrary", "arbitrary"), est),
        name="proj",
    )(x, mod, w)


def _filter_kernel(z_ref, w1_ref, b1_ref, w2_ref, b2_ref, w3_ref, b3_ref, fr_ref, w4f_ref, w4b_ref, dl_ref,
                   ks_ref, kd_ref, h_ref):
    dot = functools.partial(jnp.dot, precision=HIGHEST, preferred_element_type=F32)

    @pl.when((pl.program_id(0) == 0) & (pl.program_id(1) == 0))
    def _():
        h = jnp.sin(fr_ref[0:1, :] * (dot(z_ref[...], w1_ref[...]) + b1_ref[...]))
        h = jnp.sin(fr_ref[1:2, :] * (dot(h, w2_ref[...]) + b2_ref[...]))
        h_ref[...] = jnp.sin(fr_ref[2:3, :] * (dot(h, w3_ref[...]) + b3_ref[...]))

    h = h_ref[...]
    decay = jnp.exp(-z_ref[:, 0:1] * dl_ref[...])
    hf = dot(h, w4f_ref[...]) * decay
    hb = dot(h, w4b_ref[...]) * decay
    norm = jnp.sum(jnp.abs(hf) + jnp.abs(hb), axis=0, keepdims=True)
    hf = hf / norm
    hb = hb / norm
    row = lax.broadcasted_iota(jnp.int32, hb.shape, 0)
    hb = jnp.where(row == 0, 0.0, hb)
    ks_ref[...] = hf + hb
    kd_ref[...] = hb - hf


def _hyena_filters(zfeat, w1, b1, w2, b2, w3, b3, w4, freq, deltas, tc):
    seq = zfeat.shape[0]
    fo = w2.shape[0]
    nct = D_HY // tc
    full = lambda a: pl.BlockSpec(a.shape, lambda o, j: (0,) * a.ndim)
    args = (zfeat, w1, b1.reshape(1, -1), w2, b2.reshape(1, -1), w3, b3.reshape(1, -1), freq)
    out_spec = pl.BlockSpec((None, seq, tc), lambda o, j: (o, 0, j))
    return pl.pallas_call(
        _filter_kernel,
        grid=(HY_ORDER, nct),
        in_specs=[full(a) for a in args] + [
            pl.BlockSpec((fo, tc), lambda o, j: (0, (2 * o) * nct + j)),
            pl.BlockSpec((fo, tc), lambda o, j: (0, (2 * o + 1) * nct + j)),
            pl.BlockSpec((1, tc), lambda o, j: (0, j))],
        out_specs=[out_spec, out_spec],
        out_shape=[jax.ShapeDtypeStruct((HY_ORDER, seq, D_HY), F32)] * 2,
        scratch_shapes=[pltpu.VMEM((seq, fo), F32)],
        compiler_params=_params(("arbitrary", "arbitrary"), 12 * seq * tc * 4),
        name="hyena_filter",
    )(*args, w4, w4, deltas)


def _spectrum_kernel(ff_ref, ks_ref, kd_ref, kp_ref, kq_ref):
    seq = ks_ref.shape[0]
    ks = ks_ref[...].astype(BF16)
    kd = kd_ref[...].astype(BF16)
    p = jnp.dot(ff_ref[0:seq, :], ks, preferred_element_type=F32)
    q = jnp.dot(ff_ref[seq:2 * seq, :], kd, preferred_element_type=F32)
    nyq = jnp.dot(ff_ref[seq:seq + 16, :], ks, preferred_element_type=F32)[0:1, :]
    row = lax.broadcasted_iota(jnp.int32, p.shape, 0)
    wf = jnp.where(row == 0, 0.5 / seq, 1.0 / seq)
    kp_ref[...] = p * wf
    kq_ref[...] = jnp.where(row == 0, nyq, q) * wf


def _spectrum(ffwd, ksum, kdiff, tc):
    seq = ksum.shape[1]
    spec = pl.BlockSpec((None, seq, tc), lambda o, j: (o, 0, j))
    return pl.pallas_call(
        _spectrum_kernel,
        grid=(HY_ORDER, D_HY // tc),
        in_specs=[pl.BlockSpec(ffwd.shape, lambda o, j: (0, 0)), spec, spec],
        out_specs=[spec, spec],
        out_shape=[jax.ShapeDtypeStruct(ksum.shape, F32)] * 2,
        compiler_params=_params(("arbitrary", "arbitrary"), ffwd.size * 2 + 12 * seq * tc * 4),
        name="hyena_spectrum",
    )(ffwd, ksum, kdiff)


def _short_conv(z, w, b):
    seq = z.shape[0]
    row = lax.broadcasted_iota(jnp.int32, z.shape, 0)
    prev = jnp.where(row == 0, 0.0, pltpu.roll(z, 1, 0))
    nxt = jnp.where(row == seq - 1, 0.0, pltpu.roll(z, seq - 1, 0))
    return prev * w[0:1, :] + z * w[1:2, :] + nxt * w[2:3, :] + b


def _dft_mul(u, ff_ref, kp_ref, kq_ref, o_ref):
    seq = u.shape[0]
    a = jnp.dot(ff_ref[...], u.astype(BF16), preferred_element_type=F32)
    p, q = a[0:seq], a[seq:2 * seq]
    kp, kq = kp_ref[...], kq_ref[...]
    row0 = lax.broadcasted_iota(jnp.int32, p.shape, 0) == 0
    o_ref[0:seq, :] = jnp.where(row0, kp * p, kp * p + kq * q).astype(o_ref.dtype)
    o_ref[seq:2 * seq, :] = jnp.where(row0, kq * q, kp * q - kq * p).astype(o_ref.dtype)


def _fwd_conv_kernel(z_ref, cw_ref, cb_ref, ff_ref, kp_ref, kq_ref, o_ref):
    _dft_mul(_short_conv(z_ref[...], cw_ref[...], cb_ref[...]), ff_ref, kp_ref, kq_ref, o_ref)


def _fwd_plain_kernel(u_ref, ff_ref, kp_ref, kq_ref, o_ref):
    _dft_mul(u_ref[...], ff_ref, kp_ref, kq_ref, o_ref)


def _inv_first_kernel(y_ref, fi_ref, zv_ref, zx_ref, cwv_ref, cbv_ref, cwx_ref, cbx_ref, skip_ref, o_ref):
    v = _short_conv(zv_ref[...], cwv_ref[...], cbv_ref[...])
    x1 = _short_conv(zx_ref[...], cwx_ref[...], cbx_ref[...])
    y = jnp.dot(fi_ref[...], y_ref[...], preferred_element_type=F32)
    o_ref[...] = x1 * (y + v * skip_ref[...])


def _inv_second_kernel(y_ref, fi_ref, g_ref, zx_ref, cwx_ref, cbx_ref, skip_ref, o_ref):
    x2 = _short_conv(zx_ref[...], cwx_ref[...], cbx_ref[...])
    y = jnp.dot(fi_ref[...], y_ref[...], preferred_element_type=F32)
    o_ref[...] = x2 * (y + g_ref[...] * skip_ref[...])


def _hyena(z, conv_w, conv_b, ffwd, finv, kp, kq, skip, tc):
    bsz, seq, _ = z.shape
    nct = D_HY // tc
    sem = ("arbitrary", "arbitrary")
    zcol = lambda g: pl.BlockSpec((None, seq, tc), lambda b, j: (b, 0, g * nct + j))
    wcol = lambda g: pl.BlockSpec((3, tc), lambda b, j: (0, g * nct + j))
    bcol = lambda g: pl.BlockSpec((1, tc), lambda b, j: (0, g * nct + j))
    act = pl.BlockSpec((None, seq, tc), lambda b, j: (b, 0, j))
    spec2 = pl.BlockSpec((None, 2 * seq, tc), lambda b, j: (b, 0, j))
    kspec = lambda o: pl.BlockSpec((None, seq, tc), lambda b, j: (o, 0, j))
    sspec = lambda o: pl.BlockSpec((None, 1, tc), lambda b, j: (o, 0, j))
    mat = pl.BlockSpec(ffwd.shape, lambda b, j: (0, 0))
    mati = pl.BlockSpec(finv.shape, lambda b, j: (0, 0))
    est = ffwd.size * 2 + 14 * seq * tc * 4
    cb = conv_b.reshape(1, -1)
    skip3 = skip.reshape(HY_ORDER, 1, D_HY)
    spec_shape = jax.ShapeDtypeStruct((bsz, 2 * seq, D_HY), BF16)
    act_shape = jax.ShapeDtypeStruct((bsz, seq, D_HY), F32)

    y1 = pl.pallas_call(
        _fwd_conv_kernel, grid=(bsz, nct),
        in_specs=[zcol(0), wcol(0), bcol(0), mat, kspec(0), kspec(0)],
        out_specs=spec2, out_shape=spec_shape, compiler_params=_params(sem, est), name="hyena_fwd1",
    )(z, conv_w, cb, ffwd, kp, kq)
    g = pl.pallas_call(
        _inv_first_kernel, grid=(bsz, nct),
        in_specs=[spec2, mati, zcol(0), zcol(1), wcol(0), bcol(0), wcol(1), bcol(1), sspec(0)],
        out_specs=act, out_shape=act_shape, compiler_params=_params(sem, est), name="hyena_inv1",
    )(y1, finv, z, z, conv_w, cb, conv_w, cb, skip3)
    y2 = pl.pallas_call(
        _fwd_plain_kernel, grid=(bsz, nct),
        in_specs=[act, mat, kspec(1), kspec(1)],
        out_specs=spec2, out_shape=spec_shape, compiler_params=_params(sem, est), name="hyena_fwd2",
    )(g, ffwd, kp, kq)
    return pl.pallas_call(
        _inv_second_kernel, grid=(bsz, nct),
        in_specs=[spec2, mati, act, zcol(2), wcol(2), bcol(2), sspec(1)],
        out_specs=act, out_shape=act_shape, compiler_params=_params(sem, est), name="hyena_inv2",
    )(y2, finv, g, z, conv_w, cb, skip3)


def _bias_kernel(rpb_ref, o_ref):
    h = pl.program_id(0)
    n_r, n_c = 2 * NA_KH - 1, 2 * NA_KW - 1
    cq = lax.broadcasted_iota(jnp.int32, (GRID_W, GRID_W), 0)
    ck = lax.broadcasted_iota(jnp.int32, (GRID_W, GRID_W), 1)
    rel = ck - jnp.clip(cq - NA_KW // 2, 0, GRID_W - NA_KW)
    ok = (rel >= 0) & (rel < NA_KW)
    dcol = jnp.clip(ck - cq + NA_KW - 1, 0, n_c - 1)
    for r in range(n_r):
        acc = jnp.zeros((GRID_W, GRID_W), F32)
        for c in range(n_c):
            acc = jnp.where(dcol == c, rpb_ref[(h * n_r + r) * n_c + c], acc)
        o_ref[r] = jnp.where(ok, acc, NEG_INF)


def _bias_tables(rpb):
    heads = rpb.shape[0]
    n_r = 2 * NA_KH - 1
    return pl.pallas_call(
        _bias_kernel,
        grid=(heads,),
        in_specs=[pl.BlockSpec(memory_space=pltpu.SMEM)],
        out_specs=pl.BlockSpec((None, n_r, GRID_W, GRID_W), lambda h: (h, 0, 0, 0)),
        out_shape=jax.ShapeDtypeStruct((heads, n_r, GRID_W, GRID_W), F32),
        compiler_params=_params(("arbitrary",), 1 << 20),
        name="na_bias",
    )(rpb.reshape(-1))


def _attn_kernel(q_ref, k_ref, v_ref, kc_ref, vc_ref, bias0_ref, bias1_ref, o_ref, *, rows, kh):
    r2 = pl.program_id(1)
    nk = kh * GRID_W
    lane = lax.broadcasted_iota(jnp.int32, (1, 2 * HEAD_DIM), 1)
    nt = (((1,), (1,)), ((), ()))
    n_pairs = q_ref.shape[1] // (2 * HEAD_DIM)
    starts = [pl.multiple_of(jnp.clip(2 * r2 + i - kh // 2, 0, rows - kh) * GRID_W, GRID_W) for i in range(2)]
    bias_refs = (bias0_ref, bias1_ref)
    for hp in range(n_pairs):
        cols = slice(hp * 2 * HEAD_DIM, (hp + 1) * 2 * HEAD_DIM)
        qp = q_ref[:, cols] * jnp.asarray(HEAD_DIM ** -0.5, BF16)
        zero = jnp.zeros((GRID_W, 2 * HEAD_DIM), BF16)
        parts = []
        for i in range(2):
            qi = qp[i * GRID_W:(i + 1) * GRID_W]
            parts += [jnp.where(lane < HEAD_DIM, qi, zero), jnp.where(lane < HEAD_DIM, zero, qi)]
        qm = jnp.concatenate(parts, axis=0)
        s_ctx = lax.dot_general(qm, kc_ref[:, cols], nt, preferred_element_type=F32)
        m_ctx = jnp.max(s_ctx, axis=-1, keepdims=True)
        s_lat, m_all = [], []
        for i in range(2):
            blk = slice(2 * i * GRID_W, 2 * (i + 1) * GRID_W)
            bias = jnp.concatenate([bias_refs[i][2 * hp], bias_refs[i][2 * hp + 1]], axis=0)
            s = lax.dot_general(qm[blk], k_ref[pl.ds(starts[i], nk), cols], nt, preferred_element_type=F32) + bias
            s_lat.append(s)
            m_all.append(jnp.maximum(jnp.max(s, axis=-1, keepdims=True), m_ctx[blk]))
        p_ctx = jnp.exp(s_ctx - jnp.concatenate(m_all, axis=0))
        o_ctx = jnp.dot(p_ctx.astype(BF16), vc_ref[:, cols], preferred_element_type=F32)
        l_ctx = jnp.sum(p_ctx, axis=-1, keepdims=True)
        for i in range(2):
            blk = slice(2 * i * GRID_W, 2 * (i + 1) * GRID_W)
            p = jnp.exp(s_lat[i] - m_all[i])
            o = jnp.dot(p.astype(BF16), v_ref[pl.ds(starts[i], nk), cols], preferred_element_type=F32) + o_ctx[blk]
            o = o / (jnp.sum(p, axis=-1, keepdims=True) + l_ctx[blk])
            o_ref[i * GRID_W:(i + 1) * GRID_W, cols] = jnp.where(lane < HEAD_DIM, o[0:GRID_W], o[GRID_W:2 * GRID_W])


def _attention(qkv, kv_ctx, bands, band_of_row, rows, kh):
    bsz, seq, d3 = qkv.shape
    dn = d3 // 3
    ctx_len = kv_ctx.shape[1]
    heads, _, _, nk = bands.shape
    assert rows % 2 == 0
    est = 2 * (2 * seq * dn * 2 + 2 * ctx_len * dn * 2 + 2 * heads * GRID_W * nk * 4 + 2 * GRID_W * dn * 6)
    band = lambda i: pl.BlockSpec((heads, None, GRID_W, nk), lambda b, r: (0, band_of_row(2 * r + i), 0, 0))
    return pl.pallas_call(
        functools.partial(_attn_kernel, rows=rows, kh=kh),
        grid=(bsz, rows // 2),
        in_specs=[pl.BlockSpec((None, 2 * GRID_W, dn), lambda b, r: (b, r, 0)),
                  pl.BlockSpec((None, seq, dn), lambda b, r: (b, 0, 1)),
                  pl.BlockSpec((None, seq, dn), lambda b, r: (b, 0, 2)),
                  pl.BlockSpec((None, ctx_len, dn), lambda b, r: (b, 0, 0)),
                  pl.BlockSpec((None, ctx_len, dn), lambda b, r: (b, 0, 1)),
                  band(0), band(1)],
        out_specs=pl.BlockSpec((None, 2 * GRID_W, dn), lambda b, r: (b, r, 0)),
        out_shape=jax.ShapeDtypeStruct((bsz, seq, dn), F32),
        compiler_params=_params(("arbitrary", "arbitrary"), est),
        name="na_attention",
    )(qkv, qkv, qkv, kv_ctx, kv_ctx, bands, bands)


def _rms(x, g):
    return x * lax.rsqrt(jnp.mean(x * x, axis=-1, keepdims=True) + LN_EPS) * g


def _layer_norm(x, g, b):
    xc = x - jnp.mean(x, axis=-1, keepdims=True)
    var = jnp.mean(xc * xc, axis=-1, keepdims=True)
    return xc * lax.rsqrt(var + LN_EPS) * g + b


def _merge_kernel(hy_ref, na_ref, x_ref, mod_ref, w_ref, ghy_ref, gna_ref, lg_ref, lb_ref, xo_ref, uo_ref, *, alpha):
    dh = hy_ref.shape[1]
    hy = _rms(hy_ref[...], ghy_ref[...]).astype(BF16)
    na = _rms(na_ref[...], gna_ref[...]).astype(BF16)
    y = (jnp.dot(hy, w_ref[0:dh, :], preferred_element_type=F32)
         + jnp.dot(na, w_ref[dh:, :], preferred_element_type=F32))
    xn = _layer_norm(alpha * x_ref[...] + mod_ref[2:3, :] * y, lg_ref[...], lb_ref[...])
    xo_ref[...] = xn
    uo_ref[...] = (xn * (1.0 + mod_ref[4:5, :]) + mod_ref[3:4, :]).astype(uo_ref.dtype)


def _merge(hy, na, x, mod, w_out, g_hy, g_na, ln_g, ln_b, alpha, tm):
    bsz, seq, d = x.shape
    dh, dn = hy.shape[2], na.shape[2]
    row = lambda a: a.reshape(1, -1)
    vec = lambda n: pl.BlockSpec((1, n), lambda b, i: (0, 0))
    tile = lambda n: pl.BlockSpec((None, tm, n), lambda b, i: (b, i, 0))
    est = d * d * 2 + 2 * tm * (dh + dn + 2 * d) * 4 + 2 * tm * d * 6 + 4 * tm * d * 4
    return pl.pallas_call(
        functools.partial(_merge_kernel, alpha=alpha),
        grid=(bsz, seq // tm),
        in_specs=[tile(dh), tile(dn), tile(d),
                  pl.BlockSpec((None, 6, d), lambda b, i: (b, 0, 0)),
                  pl.BlockSpec((d, d), lambda b, i: (0, 0)),
                  vec(dh), vec(dn), vec(d), vec(d)],
        out_specs=[tile(d), tile(d)],
        out_shape=[jax.ShapeDtypeStruct((bsz, seq, d), F32), jax.ShapeDtypeStruct((bsz, seq, d), BF16)],
        compiler_params=_params(("arbitrary", "arbitrary"), est),
        name="merge",
    )(hy, na, x, mod, w_out, row(g_hy), row(g_na), row(ln_g), row(ln_b))


def _first(a, b):
    return (a[0] > b[0]) | ((a[0] == b[0]) & (a[1] < b[1]))


def _order_pair(xs, i, j):
    a, b = xs[i], xs[j]
    f = _first(a, b)
    xs[i] = (jnp.maximum(a[0], b[0]),) + tuple(jnp.where(f, p, r) for p, r in zip(a[1:], b[1:]))
    xs[j] = (jnp.minimum(a[0], b[0]),) + tuple(jnp.where(f, r, p) for p, r in zip(a[1:], b[1:]))


def _bitonic_merge(xs):
    n = len(xs)
    j = n // 2
    while j >= 1:
        for i in range(n):
            if i & j == 0:
                _order_pair(xs, i, i | j)
        j //= 2
    return xs


def _bitonic_sort(xs):
    n = len(xs)
    k = 2
    while k <= n:
        j = k // 2
        while j >= 1:
            for i in range(n):
                if i & j == 0:
                    if i & k == 0:
                        _order_pair(xs, i, i | j)
                    else:
                        _order_pair(xs, i | j, i)
            j //= 2
        k *= 2
    return xs


def _leading_half(a, b):
    out = []
    for x, y in zip(a, reversed(b)):
        f = _first(x, y)
        out.append((jnp.maximum(x[0], y[0]),) + tuple(jnp.where(f, p, r) for p, r in zip(x[1:], y[1:])))
    return out


def _sorted_top(items, k):
    runs = [_bitonic_sort(items[i:i + k]) for i in range(0, len(items), k)]
    while len(runs) > 1:
        runs = [_bitonic_merge(_leading_half(runs[i], runs[i + 1])) for i in range(0, len(runs), 2)]
    return runs[0]


def _pitch(rows):
    return rows + 8


def _route_kernel(u_ref, wq_ref, keys_ref, idx_ref, gate_ref, s_ref):
    nkeys, half = keys_ref.shape[1], keys_ref.shape[2]
    k = PEER_TOPK
    groups = u_ref.shape[0] // V7X_LANES
    pitch = _pitch(nkeys)
    tile = (groups, V7X_LANES)
    q = jnp.dot(u_ref[...], wq_ref[...], preferred_element_type=F32)
    nt = (((1,), (1,)), ((), ()))
    def split(v):
        hi = v.astype(BF16)
        return hi, (v - hi.astype(F32)).astype(BF16)

    for p in range(2):
        k_hi, k_lo = split(keys_ref[p])
        q_hi, q_lo = split(q[:, p * half:(p + 1) * half])
        dot = lambda a, b: lax.dot_general(a, b, nt, preferred_element_type=F32)
        s = dot(k_hi, q_hi) + (dot(k_hi, q_lo) + dot(k_lo, q_hi))
        for g in range(groups):
            s_ref[p, g * pitch:g * pitch + nkeys, :] = s[:, g * V7X_LANES:(g + 1) * V7X_LANES]

    def scores(p):
        return [(s_ref[p, pl.ds(n, groups, stride=pitch), :], jnp.full(tile, float(n), F32)) for n in range(nkeys)]

    top_a = _sorted_top(scores(0), k)
    top_b = _sorted_top(scores(1), k)

    def cand(i, j):
        return (top_a[i][0] + top_b[j][0], jnp.full(tile, float(i * k + j), F32),
                top_a[i][1] * float(nkeys) + top_b[j][1])

    pad = (jnp.full(tile, -jnp.inf, F32), jnp.full(tile, float(k * k), F32), jnp.zeros(tile, F32))
    run = lambda i: [cand(i, j) for j in range(k // (i + 1))]
    tail = [cand(i, 0) for i in range(k // 2, k)]
    assert k == 16
    r0 = run(0)
    r1 = _bitonic_merge(run(1) + tail[::-1])
    r2 = _bitonic_sort(run(2) + run(3) + run(4) + run(5) + run(6))
    r3 = run(7) + [pad] * (k - 2)
    best = _leading_half(_bitonic_merge(_leading_half(r0, r1)), _bitonic_merge(_leading_half(r2, r3)))

    m = functools.reduce(jnp.maximum, [c[0] for c in best])
    e = [jnp.exp(c[0] - m) for c in best]
    inv = 1.0 / functools.reduce(jnp.add, e)
    for n in range(k):
        idx_ref[n] = best[n][2].astype(jnp.int32)
        gate_ref[n] = e[n] * inv


def _route(u, wq, keys, tm):
    t, d = u.shape
    heads, _, nkeys, half = keys.shape
    groups = tm // V7X_LANES
    est = 2 * (tm * d * 2 + d * 2 * half * 2 + 2 * nkeys * half * 4) + 8 * nkeys * tm * 4
    out_spec = pl.BlockSpec((PEER_TOPK, groups, V7X_LANES), lambda i, h: (h, i, 0))
    return pl.pallas_call(
        _route_kernel,
        grid=(t // tm, heads),
        in_specs=[pl.BlockSpec((tm, d), lambda i, h: (i, 0)),
                  pl.BlockSpec((d, 2 * half), lambda i, h: (0, h)),
                  pl.BlockSpec((None, 2, nkeys, half), lambda i, h: (h, 0, 0, 0))],
        out_specs=[out_spec, out_spec],
        out_shape=[jax.ShapeDtypeStruct((heads * PEER_TOPK, t // V7X_LANES, V7X_LANES), jnp.int32),
                   jax.ShapeDtypeStruct((heads * PEER_TOPK, t // V7X_LANES, V7X_LANES), F32)],
        scratch_shapes=[pltpu.VMEM((2, groups * _pitch(nkeys), V7X_LANES), F32)],
        compiler_params=_params(("arbitrary", "arbitrary"), est),
        name="peer_route",
    )(u, wq, keys)


def _gate_matrix_kernel(idx_ref, gate_ref, o_ref, w_ref):
    tw, nsel = idx_ref.shape
    pitch = _pitch(PEER_NKEYS)
    key = lax.broadcasted_iota(jnp.int32, (PEER_NKEYS, nsel), 0)
    nt = (((1,), (1,)), ((), ()))

    zero = jnp.zeros((PEER_NKEYS, nsel), BF16)

    def body(p, carry):
        a_hot, b_hot = [], []
        for t in (2 * p, 2 * p + 1):
            idx = idx_ref[pl.ds(t, 1), :]
            gate = gate_ref[pl.ds(t, 1), :]
            a_hot.append(jnp.where((idx >> NKEYS_SHIFT) == key, gate, 0.0).astype(BF16))
            b_hot.append(jnp.where((idx & (PEER_NKEYS - 1)) == key, 1.0, 0.0).astype(BF16))
        lhs = jnp.concatenate(a_hot, axis=1)
        rhs = jnp.concatenate([jnp.concatenate([b_hot[0], zero], axis=1),
                               jnp.concatenate([zero, b_hot[1]], axis=1)], axis=0)
        planes = lax.dot_general(lhs, rhs, nt, preferred_element_type=F32)
        for k in range(2):
            w_ref[pl.ds(pl.multiple_of((2 * p + k) * pitch, 8), PEER_NKEYS), :] = (
                planes[:, k * PEER_NKEYS:(k + 1) * PEER_NKEYS])
        return carry

    lax.fori_loop(0, tw // 2, body, 0, unroll=8)
    for a in range(PEER_NKEYS):
        o_ref[:, a * PEER_NKEYS:(a + 1) * PEER_NKEYS] = w_ref[pl.ds(a, tw, stride=pitch), :].astype(o_ref.dtype)


def _gate_matrix(idx, gate, tw):
    t, nsel = idx.shape
    n_exp = PEER_NKEYS * PEER_NKEYS
    est = 2 * (2 * tw * nsel * 4 + tw * n_exp * 2) + tw * _pitch(PEER_NKEYS) * PEER_NKEYS * 4
    return pl.pallas_call(
        _gate_matrix_kernel,
        grid=(t // tw,),
        in_specs=[pl.BlockSpec((tw, nsel), lambda i: (i, 0)), pl.BlockSpec((tw, nsel), lambda i: (i, 0))],
        out_specs=pl.BlockSpec((tw, n_exp), lambda i: (i, 0)),
        out_shape=jax.ShapeDtypeStruct((t, n_exp), BF16),
        scratch_shapes=[pltpu.VMEM((tw * _pitch(PEER_NKEYS), PEER_NKEYS), F32)],
        compiler_params=_params(("arbitrary",), est),
        name="peer_gate_matrix",
    )(idx, gate)


def _peer_kernel(u_ref, eu_ref, ev_ref, w_ref, o_ref, *, n_chunk):
    @pl.when(pl.program_id(1) == 0)
    def _():
        o_ref[...] = jnp.zeros_like(o_ref)

    z = lax.dot_general(u_ref[...], eu_ref[...], (((1,), (1,)), ((), ())), preferred_element_type=F32)
    act = 0.5 * z * (1.0 + lax.erf(z * (2.0 ** -0.5)))
    h = (w_ref[...].astype(F32) * act).astype(BF16)
    for c in range(0, o_ref.shape[1], n_chunk):
        o_ref[:, c:c + n_chunk] += jnp.dot(h, ev_ref[:, c:c + n_chunk], preferred_element_type=F32)


def _final_ln_kernel(x_ref, f_ref, mod_ref, lg_ref, lb_ref, o_ref, *, alpha):
    r = alpha * x_ref[...] + mod_ref[5:6, :] * f_ref[...]
    o_ref[...] = _layer_norm(r, lg_ref[...], lb_ref[...])


def _final_ln(x, f, mod, ln_g, ln_b, alpha, seq, tm):
    t, d = x.shape
    tile = pl.BlockSpec((tm, d), lambda i: (i, 0))
    vec = pl.BlockSpec((1, d), lambda i: (0, 0))
    return pl.pallas_call(
        functools.partial(_final_ln_kernel, alpha=alpha),
        grid=(t // tm,),
        in_specs=[tile, tile, pl.BlockSpec((None, 6, d), lambda i: ((i * tm) // seq, 0, 0)), vec, vec],
        out_specs=tile,
        out_shape=jax.ShapeDtypeStruct((t, d), F32),
        compiler_params=_params(("arbitrary",), 8 * tm * d * 4),
        name="final_ln",
    )(x, f, mod, ln_g.reshape(1, -1), ln_b.reshape(1, -1))


def _peer(u, eu, ev, w, tm, te, n_chunk):
    t, d = u.shape
    n_exp = ev.shape[0]
    once = pl.Buffered(1)
    est = tm * d * 2 + tm * d * 4 + 2 * (2 * te * d * 2 + tm * te * 2) + 4 * tm * te * 4 + tm * n_chunk * 4
    return pl.pallas_call(
        functools.partial(_peer_kernel, n_chunk=n_chunk),
        grid=(t // tm, n_exp // te),
        in_specs=[pl.BlockSpec((tm, d), lambda i, j: (i, 0), pipeline_mode=once),
                  pl.BlockSpec((te, d), lambda i, j: (j, 0)),
                  pl.BlockSpec((te, d), lambda i, j: (j, 0)),
                  pl.BlockSpec((tm, te), lambda i, j: (i, j))],
        out_specs=pl.BlockSpec((tm, d), lambda i, j: (i, 0), pipeline_mode=once),
        out_shape=jax.ShapeDtypeStruct((t, d), F32),
        compiler_params=_params(("arbitrary", "arbitrary"), est),
        name="peer_experts",
    )(u, eu, ev, w)


def _filter_features(seq):
    t = jnp.linspace(0.0, 1.0, seq, dtype=F32)[:, None]
    bands = (FILTER_EMB - 1) // 2
    w = 2.0 * math.pi * jnp.arange(seq, dtype=F32)[:, None] / seq
    f = jnp.linspace(1e-4, bands - 1, bands, dtype=F32)[None, :]
    z = jnp.concatenate([t, jnp.cos(f * w), -jnp.sin(f * w)], axis=-1)
    return jnp.pad(z, ((0, 0), (0, FILTER_PAD - FILTER_EMB)))


@functools.lru_cache(maxsize=None)
def _dft_matrices(seq):
    n2 = 2 * seq
    f = np.arange(seq, dtype=np.int64)[:, None]
    n = np.arange(seq, dtype=np.int64)[None, :]
    ang = ((f * n) % n2).astype(np.float64) * (2.0 * math.pi / n2)
    sin = np.where(f == 0, (1 - 2 * (n % 2)).astype(np.float64), np.sin(ang))
    return np.concatenate([np.cos(ang), sin], axis=0).astype(np.float32)


def kernel(x, c, ctx, c_ctx, w_ada, b_ada, w_in, conv_w, conv_b, filt_w1, filt_b1, filt_w2, filt_b2,
           filt_w3, filt_b3, filt_w4, filt_freq, hy_skip, na_rpb, hy_norm_g, na_norm_g, w_out,
           ln1_g, ln1_b, peer_wq, peer_keys, peer_u, peer_v, ln2_g, ln2_b):
    depth = w_ada.shape[0]
    assert depth == 1, "single-layer block: the context stream is never updated"
    bsz, seq, d = x.shape
    rows = seq // GRID_W
    kh = min(NA_KH, rows)
    assert kh == NA_KH and seq % GRID_W == 0
    alpha = (2.0 * depth) ** 0.25
    d_na = d - D_HY
    off_q = 3 * D_HY
    off_k = off_q + d_na

    n_cond = 8
    cond = jnp.concatenate([c, c_ctx[None], jnp.zeros((n_cond - bsz - 1, d), F32)], axis=0)
    mod = _ada(cond, w_ada[0], b_ada[0]).reshape(n_cond, 6, d)

    w_in_b = w_in[0].astype(BF16)
    z_hy = _proj(x, mod, lambda b: b, w_in_b[:, :off_q], F32, 512)
    qkv = _proj(x, mod, lambda b: b, w_in_b[:, off_q:], BF16, 512)
    kv_ctx = _proj(ctx, mod, lambda b: bsz, w_in_b[:, off_k:], BF16, 256)

    deltas = jnp.abs(jnp.linspace(math.log(DECAY_TARGET) / SLOW_DECAY_PCT,
                                  math.log(DECAY_TARGET) / FAST_DECAY_PCT, D_HY, dtype=F32))[None, :]
    padw = lambda a: jnp.pad(a, ((0, FILTER_PAD - a.shape[0]), (0, FILTER_PAD - a.shape[1])))
    padv = lambda a: jnp.pad(a, ((0, FILTER_PAD - a.shape[0]),))
    w4 = jnp.pad(filt_w4[0], ((0, FILTER_PAD - filt_w4.shape[1]), (0, 0)))
    freq = jnp.pad(filt_freq[0], ((0, 0), (0, FILTER_PAD - filt_freq.shape[2])))
    ksum, kdiff = _hyena_filters(_filter_features(seq), padw(filt_w1[0]), padv(filt_b1[0]), padw(filt_w2[0]),
                                 padv(filt_b2[0]), padw(filt_w3[0]), padv(filt_b3[0]), w4, freq, deltas, 256)
    ffwd = jnp.asarray(_dft_matrices(seq)).astype(BF16)
    finv = ffwd.T
    kp, kq = _spectrum(ffwd, ksum, kdiff, 256)
    hy_out = _hyena(z_hy, conv_w[0], conv_b[0], ffwd, finv, kp, kq, hy_skip[0], 256)

    tables = _bias_tables(na_rpb[0])
    bands = jnp.stack([jnp.concatenate([tables[:, o + i] for i in range(kh)], axis=-1) for o in range(kh)], axis=1)
    band_of_row = lambda r: jnp.clip(r - kh // 2, 0, rows - kh) - r + (NA_KH - 1)
    na_out = _attention(qkv, kv_ctx, bands, band_of_row, rows, kh)

    x_mid, u_peer = _merge(hy_out, na_out, x, mod, w_out[0].astype(BF16), hy_norm_g[0], na_norm_g[0],
                           ln1_g[0], ln1_b[0], alpha, 256)

    t = bsz * seq
    u2 = u_peer.reshape(t, d)
    idx_t, gate_t = _route(u2, peer_wq[0].astype(BF16), peer_keys[0], 1024)
    w_gate = _gate_matrix(idx_t.reshape(-1, t).T, gate_t.reshape(-1, t).T, 64)
    f = _peer(u2, peer_u[0].astype(BF16), peer_v[0].astype(BF16), w_gate, 2048, 512, 512)
    out = _final_ln(x_mid.reshape(t, d), f, mod, ln2_g[0], ln2_b[0], alpha, seq, 512)
    return out.reshape(bsz, seq, d)
```

```python
import functools
import math

import numpy as np
import jax
import jax.numpy as jnp
from jax import lax
from jax.experimental import pallas as pl
from jax.experimental.pallas import tpu as pltpu

F32 = jnp.float32
BF16 = jnp.bfloat16
HIGHEST = lax.Precision.HIGHEST

GRID_W = 64
HEAD_DIM = 64
D_HY = 1024
HY_ORDER = 2
FILTER_EMB = 33
FILTER_PAD = 128
DECAY_TARGET = 1e-2
FAST_DECAY_PCT = 0.3
SLOW_DECAY_PCT = 1.5
NA_KH = 8
NA_KW = 16
PEER_HEADS = 8
PEER_NKEYS = 128
NKEYS_SHIFT = PEER_NKEYS.bit_length() - 1
assert 1 << NKEYS_SHIFT == PEER_NKEYS
PEER_TOPK = 16
LN_EPS = 1e-5
NEG_INF = -1e30

V7X_VMEM_BYTES = 64 * 1024 * 1024
V7X_LANES = 128
V7X_MXU_DEPTH = 256
NA_GROUP = V7X_MXU_DEPTH // HEAD_DIM


def _vmem_limit(estimate_bytes):
    return int(min(estimate_bytes + 12 * 1024 * 1024, V7X_VMEM_BYTES - 6 * 1024 * 1024))


def _params(semantics, estimate_bytes):
    return pltpu.CompilerParams(dimension_semantics=semantics, vmem_limit_bytes=_vmem_limit(estimate_bytes))


def _ada_kernel(cond_ref, w_ref, b_ref, o_ref):
    s = jax.nn.silu(cond_ref[...])
    o_ref[...] = jnp.dot(s, w_ref[...], precision=HIGHEST, preferred_element_type=F32) + b_ref[...]


def _ada(cond, w, b):
    n, d = cond.shape
    tn = 1024
    return pl.pallas_call(
        _ada_kernel,
        grid=(w.shape[1] // tn,),
        in_specs=[pl.BlockSpec((n, d), lambda j: (0, 0)),
                  pl.BlockSpec((d, tn), lambda j: (0, j)),
                  pl.BlockSpec((1, tn), lambda j: (0, j))],
        out_specs=pl.BlockSpec((n, tn), lambda j: (0, j)),
        out_shape=jax.ShapeDtypeStruct((n, w.shape[1]), F32),
        compiler_params=_params(("arbitrary",), 2 * d * tn * 4),
        name="ada",
    )(cond, w, b.reshape(1, -1))


def _proj_kernel(x_ref, mod_ref, w_ref, o_ref):
    u = x_ref[...] * (1.0 + mod_ref[1:2, :]) + mod_ref[0:1, :]
    o_ref[...] = jnp.dot(u.astype(BF16), w_ref[...], preferred_element_type=F32).astype(o_ref.dtype)


def _proj(x, mod, mod_row, w, out_dtype, tm):
    bsz, seq, d = x.shape
    n = w.shape[1]
    est = d * n * 2 + 2 * tm * d * 4 + 2 * tm * n * 4 + tm * n * 4
    return pl.pallas_call(
        _proj_kernel,
        grid=(bsz, seq // tm),
        in_specs=[pl.BlockSpec((None, tm, d), lambda b, i: (b, i, 0)),
                  pl.BlockSpec((None, 6, d), lambda b, i: (mod_row(b), 0, 0)),
                  pl.BlockSpec((d, n), lambda b, i: (0, 0))],
        out_specs=pl.BlockSpec((None, tm, n), lambda b, i: (b, i, 0)),
        out_shape=jax.ShapeDtypeStruct((bsz, seq, n), out_dtype),
        compiler_params=_params(("arbitrary", "arbitrary"), est),
        name="proj",
    )(x, mod, w)


def _filter_kernel(z_ref, w1_ref, b1_ref, w2_ref, b2_ref, w3_ref, b3_ref, fr_ref, w4f_ref, w4b_ref, dl_ref,
                   ks_ref, kd_ref, h_ref):
    dot = functools.partial(jnp.dot, precision=HIGHEST, preferred_element_type=F32)

    @pl.when((pl.program_id(0) == 0) & (pl.program_id(1) == 0))
    def _():
        h = jnp.sin(fr_ref[0:1, :] * (dot(z_ref[...], w1_ref[...]) + b1_ref[...]))
        h = jnp.sin(fr_ref[1:2, :] * (dot(h, w2_ref[...]) + b2_ref[...]))
        h_ref[...] = jnp.sin(fr_ref[2:3, :] * (dot(h, w3_ref[...]) + b3_ref[...]))

    h = h_ref[...]
    decay = jnp.exp(-z_ref[:, 0:1] * dl_ref[...])
    hf = dot(h, w4f_ref[...]) * decay
    hb = dot(h, w4b_ref[...]) * decay
    norm = jnp.sum(jnp.abs(hf) + jnp.abs(hb), axis=0, keepdims=True)
    hf = hf / norm
    hb = hb / norm
    row = lax.broadcasted_iota(jnp.int32, hb.shape, 0)
    hb = jnp.where(row == 0, 0.0, hb)
    ks_ref[...] = hf + hb
    kd_ref[...] = hb - hf


def _hyena_filters(zfeat, w1, b1, w2, b2, w3, b3, w4, freq, deltas, tc):
    seq = zfeat.shape[0]
    fo = w2.shape[0]
    nct = D_HY // tc
    full = lambda a: pl.BlockSpec(a.shape, lambda o, j: (0,) * a.ndim)
    args = (zfeat, w1, b1.reshape(1, -1), w2, b2.reshape(1, -1), w3, b3.reshape(1, -1), freq)
    out_spec = pl.BlockSpec((None, seq, tc), lambda o, j: (o, 0, j))
    return pl.pallas_call(
        _filter_kernel,
        grid=(HY_ORDER, nct),
        in_specs=[full(a) for a in args] + [
            pl.BlockSpec((fo, tc), lambda o, j: (0, (2 * o) * nct + j)),
            pl.BlockSpec((fo, tc), lambda o, j: (0, (2 * o + 1) * nct + j)),
            pl.BlockSpec((1, tc), lambda o, j: (0, j))],
        out_specs=[out_spec, out_spec],
        out_shape=[jax.ShapeDtypeStruct((HY_ORDER, seq, D_HY), F32)] * 2,
        scratch_shapes=[pltpu.VMEM((seq, fo), F32)],
        compiler_params=_params(("arbitrary", "arbitrary"), 12 * seq * tc * 4),
        name="hyena_filter",
    )(*args, w4, w4, deltas)


def _spectrum_kernel(ff_ref, ks_ref, kd_ref, kp_ref, kq_ref):
    seq = ks_ref.shape[0]
    ks = ks_ref[...].astype(BF16)
    kd = kd_ref[...].astype(BF16)
    p = jnp.dot(ff_ref[0:seq, :], ks, preferred_element_type=F32)
    q = jnp.dot(ff_ref[seq:2 * seq, :], kd, preferred_element_type=F32)
    nyq = jnp.dot(ff_ref[seq:seq + 16, :], ks, preferred_element_type=F32)[0:1, :]
    row = lax.broadcasted_iota(jnp.int32, p.shape, 0)
    wf = jnp.where(row == 0, 0.5 / seq, 1.0 / seq)
    kp_ref[...] = p * wf
    kq_ref[...] = jnp.where(row == 0, nyq, q) * wf


def _spectrum(ffwd, ksum, kdiff, tc):
    seq = ksum.shape[1]
    spec = pl.BlockSpec((None, seq, tc), lambda o, j: (o, 0, j))
    return pl.pallas_call(
        _spectrum_kernel,
        grid=(HY_ORDER, D_HY // tc),
        in_specs=[pl.BlockSpec(ffwd.shape, lambda o, j: (0, 0)), spec, spec],
        out_specs=[spec, spec],
        out_shape=[jax.ShapeDtypeStruct(ksum.shape, F32)] * 2,
        compiler_params=_params(("arbitrary", "arbitrary"), ffwd.size * 2 + 12 * seq * tc * 4),
        name="hyena_spectrum",
    )(ffwd, ksum, kdiff)


def _short_conv(z, w, b):
    seq = z.shape[0]
    row = lax.broadcasted_iota(jnp.int32, z.shape, 0)
    prev = jnp.where(row == 0, 0.0, pltpu.roll(z, 1, 0))
    nxt = jnp.where(row == seq - 1, 0.0, pltpu.roll(z, seq - 1, 0))
    return prev * w[0:1, :] + z * w[1:2, :] + nxt * w[2:3, :] + b


def _dft_mul(u, ff_ref, kp_ref, kq_ref, o_ref):
    seq = u.shape[0]
    a = jnp.dot(ff_ref[...], u.astype(BF16), preferred_element_type=F32)
    p, q = a[0:seq], a[seq:2 * seq]
    kp, kq = kp_ref[...], kq_ref[...]
    row0 = lax.broadcasted_iota(jnp.int32, p.shape, 0) == 0
    o_ref[0:seq, :] = jnp.where(row0, kp * p, kp * p + kq * q).astype(o_ref.dtype)
    o_ref[seq:2 * seq, :] = jnp.where(row0, kq * q, kp * q - kq * p).astype(o_ref.dtype)


def _fwd_conv_kernel(z_ref, cw_ref, cb_ref, ff_ref, kp_ref, kq_ref, o_ref):
    _dft_mul(_short_conv(z_ref[...], cw_ref[...], cb_ref[...]), ff_ref, kp_ref, kq_ref, o_ref)


def _fwd_plain_kernel(u_ref, ff_ref, kp_ref, kq_ref, o_ref):
    _dft_mul(u_ref[...], ff_ref, kp_ref, kq_ref, o_ref)


def _inv_first_kernel(y_ref, fi_ref, zv_ref, zx_ref, cwv_ref, cbv_ref, cwx_ref, cbx_ref, skip_ref, o_ref):
    v = _short_conv(zv_ref[...], cwv_ref[...], cbv_ref[...])
    x1 = _short_conv(zx_ref[...], cwx_ref[...], cbx_ref[...])
    y = jnp.dot(fi_ref[...], y_ref[...], preferred_element_type=F32)
    o_ref[...] = x1 * (y + v * skip_ref[...])


def _inv_second_kernel(y_ref, fi_ref, g_ref, zx_ref, cwx_ref, cbx_ref, skip_ref, o_ref):
    x2 = _short_conv(zx_ref[...], cwx_ref[...], cbx_ref[...])
    y = jnp.dot(fi_ref[...], y_ref[...], preferred_element_type=F32)
    o_ref[...] = x2 * (y + g_ref[...] * skip_ref[...])


def _hyena(z, conv_w, conv_b, ffwd, finv, kp, kq, skip, tc):
    bsz, seq, _ = z.shape
    nct = D_HY // tc
    sem = ("arbitrary", "arbitrary")
    zcol = lambda g: pl.BlockSpec((None, seq, tc), lambda b, j: (b, 0, g * nct + j))
    wcol = lambda g: pl.BlockSpec((3, tc), lambda b, j: (0, g * nct + j))
    bcol = lambda g: pl.BlockSpec((1, tc), lambda b, j: (0, g * nct + j))
    act = pl.BlockSpec((None, seq, tc), lambda b, j: (b, 0, j))
    spec2 = pl.BlockSpec((None, 2 * seq, tc), lambda b, j: (b, 0, j))
    kspec = lambda o: pl.BlockSpec((None, seq, tc), lambda b, j: (o, 0, j))
    sspec = lambda o: pl.BlockSpec((None, 1, tc), lambda b, j: (o, 0, j))
    mat = pl.BlockSpec(ffwd.shape, lambda b, j: (0, 0))
    mati = pl.BlockSpec(finv.shape, lambda b, j: (0, 0))
    est = ffwd.size * 2 + 14 * seq * tc * 4
    cb = conv_b.reshape(1, -1)
    skip3 = skip.reshape(HY_ORDER, 1, D_HY)
    spec_shape = jax.ShapeDtypeStruct((bsz, 2 * seq, D_HY), BF16)
    act_shape = jax.ShapeDtypeStruct((bsz, seq, D_HY), F32)

    y1 = pl.pallas_call(
        _fwd_conv_kernel, grid=(bsz, nct),
        in_specs=[zcol(0), wcol(0), bcol(0), mat, kspec(0), kspec(0)],
        out_specs=spec2, out_shape=spec_shape, compiler_params=_params(sem, est), name="hyena_fwd1",
    )(z, conv_w, cb, ffwd, kp, kq)
    g = pl.pallas_call(
        _inv_first_kernel, grid=(bsz, nct),
        in_specs=[spec2, mati, zcol(0), zcol(1), wcol(0), bcol(0), wcol(1), bcol(1), sspec(0)],
        out_specs=act, out_shape=act_shape, compiler_params=_params(sem, est), name="hyena_inv1",
    )(y1, finv, z, z, conv_w, cb, conv_w, cb, skip3)
    y2 = pl.pallas_call(
        _fwd_plain_kernel, grid=(bsz, nct),
        in_specs=[act, mat, kspec(1), kspec(1)],
        out_specs=spec2, out_shape=spec_shape, compiler_params=_params(sem, est), name="hyena_fwd2",
    )(g, ffwd, kp, kq)
    return pl.pallas_call(
        _inv_second_kernel, grid=(bsz, nct),
        in_specs=[spec2, mati, act, zcol(2), wcol(2), bcol(2), sspec(1)],
        out_specs=act, out_shape=act_shape, compiler_params=_params(sem, est), name="hyena_inv2",
    )(y2, finv, g, z, conv_w, cb, skip3)


def _bias_kernel(rpb_ref, o_ref):
    h = pl.program_id(0)
    n_r, n_c = 2 * NA_KH - 1, 2 * NA_KW - 1
    cq = lax.broadcasted_iota(jnp.int32, (GRID_W, GRID_W), 0)
    ck = lax.broadcasted_iota(jnp.int32, (GRID_W, GRID_W), 1)
    rel = ck - jnp.clip(cq - NA_KW // 2, 0, GRID_W - NA_KW)
    ok = (rel >= 0) & (rel < NA_KW)
    dcol = jnp.clip(ck - cq + NA_KW - 1, 0, n_c - 1)
    for r in range(n_r):
        acc = jnp.zeros((GRID_W, GRID_W), F32)
        for c in range(n_c):
            acc = jnp.where(dcol == c, rpb_ref[(h * n_r + r) * n_c + c], acc)
        o_ref[r] = jnp.where(ok, acc, NEG_INF)


def _bias_tables(rpb):
    heads = rpb.shape[0]
    n_r = 2 * NA_KH - 1
    return pl.pallas_call(
        _bias_kernel,
        grid=(heads,),
        in_specs=[pl.BlockSpec(memory_space=pltpu.SMEM)],
        out_specs=pl.BlockSpec((None, n_r, GRID_W, GRID_W), lambda h: (h, 0, 0, 0)),
        out_shape=jax.ShapeDtypeStruct((heads, n_r, GRID_W, GRID_W), F32),
        compiler_params=_params(("arbitrary",), 1 << 20),
        name="na_bias",
    )(rpb.reshape(-1))


def _attn_kernel(q_ref, k_ref, v_ref, kc_ref, vc_ref, bias0_ref, bias1_ref, o_ref, *, rows, kh):
    r2 = pl.program_id(1)
    nk = kh * GRID_W
    gw = NA_GROUP * HEAD_DIM
    lane = lax.broadcasted_iota(jnp.int32, (1, gw), 1)
    head_of_lane = [(lane >= e * HEAD_DIM) & (lane < (e + 1) * HEAD_DIM) for e in range(NA_GROUP)]
    nt = (((1,), (1,)), ((), ()))
    starts = [pl.multiple_of(jnp.clip(2 * r2 + i - kh // 2, 0, rows - kh) * GRID_W, GRID_W) for i in range(2)]
    bias_refs = (bias0_ref, bias1_ref)
    per_row = NA_GROUP * GRID_W
    zero = jnp.zeros((GRID_W, gw), BF16)
    for hg in range(q_ref.shape[1] // gw):
        cols = slice(hg * gw, (hg + 1) * gw)
        qp = q_ref[:, cols] * jnp.asarray(HEAD_DIM ** -0.5, BF16)
        qm = jnp.concatenate([jnp.where(head_of_lane[e], qp[i * GRID_W:(i + 1) * GRID_W], zero)
                              for i in range(2) for e in range(NA_GROUP)], axis=0)
        s_ctx = lax.dot_general(qm, kc_ref[:, cols], nt, preferred_element_type=F32)
        m_ctx = jnp.max(s_ctx, axis=-1, keepdims=True)
        s_lat, m_all = [], []
        for i in range(2):
            blk = slice(i * per_row, (i + 1) * per_row)
            bias = jnp.concatenate([bias_refs[i][hg * NA_GROUP + e] for e in range(NA_GROUP)], axis=0)
            s = lax.dot_general(qm[blk], k_ref[pl.ds(starts[i], nk), cols], nt, preferred_element_type=F32) + bias
            s_lat.append(s)
            m_all.append(jnp.maximum(jnp.max(s, axis=-1, keepdims=True), m_ctx[blk]))
        p_ctx = jnp.exp(s_ctx - jnp.concatenate(m_all, axis=0))
        o_ctx = jnp.dot(p_ctx.astype(BF16), vc_ref[:, cols], preferred_element_type=F32)
        l_ctx = jnp.sum(p_ctx, axis=-1, keepdims=True)
        for i in range(2):
            blk = slice(i * per_row, (i + 1) * per_row)
            p = jnp.exp(s_lat[i] - m_all[i])
            o = jnp.dot(p.astype(BF16), v_ref[pl.ds(starts[i], nk), cols], preferred_element_type=F32) + o_ctx[blk]
            o = o / (jnp.sum(p, axis=-1, keepdims=True) + l_ctx[blk])
            out = o[0:GRID_W]
            for e in range(1, NA_GROUP):
                out = jnp.where(head_of_lane[e], o[e * GRID_W:(e + 1) * GRID_W], out)
            o_ref[i * GRID_W:(i + 1) * GRID_W, cols] = out


def _attention(qkv, kv_ctx, bands, band_of_row, rows, kh):
    bsz, seq, d3 = qkv.shape
    dn = d3 // 3
    ctx_len = kv_ctx.shape[1]
    heads, _, _, nk = bands.shape
    assert rows % 2 == 0
    est = 2 * (2 * seq * dn * 2 + 2 * ctx_len * dn * 2 + 2 * heads * GRID_W * nk * 4 + 2 * GRID_W * dn * 6)
    band = lambda i: pl.BlockSpec((heads, None, GRID_W, nk), lambda b, r: (0, band_of_row(2 * r + i), 0, 0))
    return pl.pallas_call(
        functools.partial(_attn_kernel, rows=rows, kh=kh),
        grid=(bsz, rows // 2),
        in_specs=[pl.BlockSpec((None, 2 * GRID_W, dn), lambda b, r: (b, r, 0)),
                  pl.BlockSpec((None, seq, dn), lambda b, r: (b, 0, 1)),
                  pl.BlockSpec((None, seq, dn), lambda b, r: (b, 0, 2)),
                  pl.BlockSpec((None, ctx_len, dn), lambda b, r: (b, 0, 0)),
                  pl.BlockSpec((None, ctx_len, dn), lambda b, r: (b, 0, 1)),
                  band(0), band(1)],
        out_specs=pl.BlockSpec((None, 2 * GRID_W, dn), lambda b, r: (b, r, 0)),
        out_shape=jax.ShapeDtypeStruct((bsz, seq, dn), F32),
        compiler_params=_params(("arbitrary", "arbitrary"), est),
        name="na_attention",
    )(qkv, qkv, qkv, kv_ctx, kv_ctx, bands, bands)


def _rms(x, g):
    return x * lax.rsqrt(jnp.mean(x * x, axis=-1, keepdims=True) + LN_EPS) * g


def _layer_norm(x, g, b):
    xc = x - jnp.mean(x, axis=-1, keepdims=True)
    var = jnp.mean(xc * xc, axis=-1, keepdims=True)
    return xc * lax.rsqrt(var + LN_EPS) * g + b


def _merge_kernel(hy_ref, na_ref, x_ref, mod_ref, w_ref, ghy_ref, gna_ref, lg_ref, lb_ref, xo_ref, uo_ref, *, alpha):
    dh = hy_ref.shape[1]
    hy = _rms(hy_ref[...], ghy_ref[...]).astype(BF16)
    na = _rms(na_ref[...], gna_ref[...]).astype(BF16)
    y = (jnp.dot(hy, w_ref[0:dh, :], preferred_element_type=F32)
         + jnp.dot(na, w_ref[dh:, :], preferred_element_type=F32))
    xn = _layer_norm(alpha * x_ref[...] + mod_ref[2:3, :] * y, lg_ref[...], lb_ref[...])
    xo_ref[...] = xn
    uo_ref[...] = (xn * (1.0 + mod_ref[4:5, :]) + mod_ref[3:4, :]).astype(uo_ref.dtype)


def _merge(hy, na, x, mod, w_out, g_hy, g_na, ln_g, ln_b, alpha, tm):
    bsz, seq, d = x.shape
    dh, dn = hy.shape[2], na.shape[2]
    row = lambda a: a.reshape(1, -1)
    vec = lambda n: pl.BlockSpec((1, n), lambda b, i: (0, 0))
    tile = lambda n: pl.BlockSpec((None, tm, n), lambda b, i: (b, i, 0))
    est = d * d * 2 + 2 * tm * (dh + dn + 2 * d) * 4 + 2 * tm * d * 6 + 4 * tm * d * 4
    return pl.pallas_call(
        functools.partial(_merge_kernel, alpha=alpha),
        grid=(bsz, seq // tm),
        in_specs=[tile(dh), tile(dn), tile(d),
                  pl.BlockSpec((None, 6, d), lambda b, i: (b, 0, 0)),
                  pl.BlockSpec((d, d), lambda b, i: (0, 0)),
                  vec(dh), vec(dn), vec(d), vec(d)],
        out_specs=[tile(d), tile(d)],
        out_shape=[jax.ShapeDtypeStruct((bsz, seq, d), F32), jax.ShapeDtypeStruct((bsz, seq, d), BF16)],
        compiler_params=_params(("arbitrary", "arbitrary"), est),
        name="merge",
    )(hy, na, x, mod, w_out, row(g_hy), row(g_na), row(ln_g), row(ln_b))


def _first(a, b):
    return (a[0] > b[0]) | ((a[0] == b[0]) & (a[1] < b[1]))


def _order_pair(xs, i, j):
    a, b = xs[i], xs[j]
    f = _first(a, b)
    xs[i] = (jnp.maximum(a[0], b[0]),) + tuple(jnp.where(f, p, r) for p, r in zip(a[1:], b[1:]))
    xs[j] = (jnp.minimum(a[0], b[0]),) + tuple(jnp.where(f, r, p) for p, r in zip(a[1:], b[1:]))


def _bitonic_merge(xs):
    n = len(xs)
    j = n // 2
    while j >= 1:
        for i in range(n):
            if i & j == 0:
                _order_pair(xs, i, i | j)
        j //= 2
    return xs


def _sort(xs):
    n = len(xs)
    p = 1
    while p < n:
        k = p
        while k >= 1:
            for j in range(k % p, n - k, 2 * k):
                for i in range(min(k, n - j - k)):
                    if (i + j) // (2 * p) == (i + j + k) // (2 * p):
                        _order_pair(xs, i + j, i + j + k)
            k //= 2
        p *= 2
    return xs


def _leading_half(a, b):
    out = []
    for x, y in zip(a, reversed(b)):
        f = _first(x, y)
        out.append((jnp.maximum(x[0], y[0]),) + tuple(jnp.where(f, p, r) for p, r in zip(x[1:], y[1:])))
    return out


def _sorted_top(items, k):
    runs = [_sort(items[i:i + k]) for i in range(0, len(items), k)]
    while len(runs) > 1:
        runs = [_bitonic_merge(_leading_half(runs[i], runs[i + 1])) for i in range(0, len(runs), 2)]
    return runs[0]


def _pitch(rows):
    return rows + 8


def _route_kernel(u_ref, wq_ref, keys_ref, idx_ref, gate_ref, s_ref):
    nkeys, half = keys_ref.shape[1], keys_ref.shape[2]
    k = PEER_TOPK
    groups = u_ref.shape[0] // V7X_LANES
    pitch = _pitch(nkeys)
    tile = (groups, V7X_LANES)
    q = jnp.dot(u_ref[...], wq_ref[...], preferred_element_type=F32)
    nt = (((1,), (1,)), ((), ()))
    def split(v):
        hi = v.astype(BF16)
        return hi, (v - hi.astype(F32)).astype(BF16)

    for p in range(2):
        k_hi, k_lo = split(keys_ref[p])
        q_hi, q_lo = split(q[:, p * half:(p + 1) * half])
        dot = lambda a, b: lax.dot_general(a, b, nt, preferred_element_type=F32)
        s = dot(k_hi, q_hi) + (dot(k_hi, q_lo) + dot(k_lo, q_hi))
        for g in range(groups):
            s_ref[p, g * pitch:g * pitch + nkeys, :] = s[:, g * V7X_LANES:(g + 1) * V7X_LANES]

    def scores(p):
        return [(s_ref[p, pl.ds(n, groups, stride=pitch), :], jnp.full(tile, float(n), F32)) for n in range(nkeys)]

    top_a = _sorted_top(scores(0), k)
    top_b = _sorted_top(scores(1), k)

    def cand(i, j):
        return (top_a[i][0] + top_b[j][0], jnp.full(tile, float(i * k + j), F32),
                top_a[i][1] * float(nkeys) + top_b[j][1])

    pad = (jnp.full(tile, -jnp.inf, F32), jnp.full(tile, float(k * k), F32), jnp.zeros(tile, F32))
    run = lambda i: [cand(i, j) for j in range(k // (i + 1))]
    tail = [cand(i, 0) for i in range(k // 2, k)]
    assert k == 16
    r0 = run(0)
    r1 = _bitonic_merge(run(1) + tail[::-1])
    r2 = _sort(run(2) + run(3) + run(4) + run(5) + run(6))
    r3 = run(7) + [pad] * (k - 2)
    best = _leading_half(_bitonic_merge(_leading_half(r0, r1)), _bitonic_merge(_leading_half(r2, r3)))

    m = functools.reduce(jnp.maximum, [c[0] for c in best])
    e = [jnp.exp(c[0] - m) for c in best]
    inv = 1.0 / functools.reduce(jnp.add, e)
    for n in range(k):
        idx_ref[n] = best[n][2].astype(jnp.int32)
        gate_ref[n] = e[n] * inv


def _route(u, wq, keys, tm):
    t, d = u.shape
    heads, _, nkeys, half = keys.shape
    groups = tm // V7X_LANES
    est = 2 * (tm * d * 2 + d * 2 * half * 2 + 2 * nkeys * half * 4) + 8 * nkeys * tm * 4
    out_spec = pl.BlockSpec((PEER_TOPK, groups, V7X_LANES), lambda i, h: (h, i, 0))
    return pl.pallas_call(
        _route_kernel,
        grid=(t // tm, heads),
        in_specs=[pl.BlockSpec((tm, d), lambda i, h: (i, 0)),
                  pl.BlockSpec((d, 2 * half), lambda i, h: (0, h)),
                  pl.BlockSpec((None, 2, nkeys, half), lambda i, h: (h, 0, 0, 0))],
        out_specs=[out_spec, out_spec],
        out_shape=[jax.ShapeDtypeStruct((heads * PEER_TOPK, t // V7X_LANES, V7X_LANES), jnp.int32),
                   jax.ShapeDtypeStruct((heads * PEER_TOPK, t // V7X_LANES, V7X_LANES), F32)],
        scratch_shapes=[pltpu.VMEM((2, groups * _pitch(nkeys), V7X_LANES), F32)],
        compiler_params=_params(("arbitrary", "arbitrary"), est),
        name="peer_route",
    )(u, wq, keys)


def _gate_matrix_kernel(idx_ref, gate_ref, o_ref, w_ref):
    tw, nsel = idx_ref.shape
    pitch = _pitch(PEER_NKEYS)
    key = lax.broadcasted_iota(jnp.int32, (PEER_NKEYS, nsel), 0)
    nt = (((1,), (1,)), ((), ()))

    zero = jnp.zeros((PEER_NKEYS, nsel), BF16)

    def body(p, carry):
        a_hot, b_hot = [], []
        for t in (2 * p, 2 * p + 1):
            idx = idx_ref[pl.ds(t, 1), :]
            gate = gate_ref[pl.ds(t, 1), :]
            a_hot.append(jnp.where((idx >> NKEYS_SHIFT) == key, gate, 0.0).astype(BF16))
            b_hot.append(jnp.where((idx & (PEER_NKEYS - 1)) == key, 1.0, 0.0).astype(BF16))
        lhs = jnp.concatenate(a_hot, axis=1)
        rhs = jnp.concatenate([jnp.concatenate([b_hot[0], zero], axis=1),
                               jnp.concatenate([zero, b_hot[1]], axis=1)], axis=0)
        planes = lax.dot_general(lhs, rhs, nt, preferred_element_type=F32)
        for k in range(2):
            w_ref[pl.ds(pl.multiple_of((2 * p + k) * pitch, 8), PEER_NKEYS), :] = (
                planes[:, k * PEER_NKEYS:(k + 1) * PEER_NKEYS])
        return carry

    lax.fori_loop(0, tw // 2, body, 0, unroll=8)
    for a in range(PEER_NKEYS):
        o_ref[:, a * PEER_NKEYS:(a + 1) * PEER_NKEYS] = w_ref[pl.ds(a, tw, stride=pitch), :].astype(o_ref.dtype)


def _gate_matrix(idx, gate, tw):
    t, nsel = idx.shape
    n_exp = PEER_NKEYS * PEER_NKEYS
    est = 2 * (2 * tw * nsel * 4 + tw * n_exp * 2) + tw * _pitch(PEER_NKEYS) * PEER_NKEYS * 4
    return pl.pallas_call(
        _gate_matrix_kernel,
        grid=(t // tw,),
        in_specs=[pl.BlockSpec((tw, nsel), lambda i: (i, 0)), pl.BlockSpec((tw, nsel), lambda i: (i, 0))],
        out_specs=pl.BlockSpec((tw, n_exp), lambda i: (i, 0)),
        out_shape=jax.ShapeDtypeStruct((t, n_exp), BF16),
        scratch_shapes=[pltpu.VMEM((tw * _pitch(PEER_NKEYS), PEER_NKEYS), F32)],
        compiler_params=_params(("arbitrary",), est),
        name="peer_gate_matrix",
    )(idx, gate)


def _peer_kernel(u_ref, eu_ref, ev_ref, w_ref, o_ref, *, n_chunk):
    @pl.when(pl.program_id(1) == 0)
    def _():
        o_ref[...] = jnp.zeros_like(o_ref)

    z = lax.dot_general(u_ref[...], eu_ref[...], (((1,), (1,)), ((), ())), preferred_element_type=F32)
    act = 0.5 * z * (1.0 + lax.erf(z * (2.0 ** -0.5)))
    h = (w_ref[...].astype(F32) * act).astype(BF16)
    for c in range(0, o_ref.shape[1], n_chunk):
        o_ref[:, c:c + n_chunk] += jnp.dot(h, ev_ref[:, c:c + n_chunk], preferred_element_type=F32)


def _final_ln_kernel(x_ref, f_ref, mod_ref, lg_ref, lb_ref, o_ref, *, alpha):
    r = alpha * x_ref[...] + mod_ref[5:6, :] * f_ref[...]
    o_ref[...] = _layer_norm(r, lg_ref[...], lb_ref[...])


def _final_ln(x, f, mod, ln_g, ln_b, alpha, seq, tm):
    t, d = x.shape
    tile = pl.BlockSpec((tm, d), lambda i: (i, 0))
    vec = pl.BlockSpec((1, d), lambda i: (0, 0))
    return pl.pallas_call(
        functools.partial(_final_ln_kernel, alpha=alpha),
        grid=(t // tm,),
        in_specs=[tile, tile, pl.BlockSpec((None, 6, d), lambda i: ((i * tm) // seq, 0, 0)), vec, vec],
        out_specs=tile,
        out_shape=jax.ShapeDtypeStruct((t, d), F32),
        compiler_params=_params(("arbitrary",), 8 * tm * d * 4),
        name="final_ln",
    )(x, f, mod, ln_g.reshape(1, -1), ln_b.reshape(1, -1))


def _peer(u, eu, ev, w, tm, te, n_chunk):
    t, d = u.shape
    n_exp = ev.shape[0]
    once = pl.Buffered(1)
    est = tm * d * 2 + tm * d * 4 + 2 * (2 * te * d * 2 + tm * te * 2) + 4 * tm * te * 4 + tm * n_chunk * 4
    return pl.pallas_call(
        functools.partial(_peer_kernel, n_chunk=n_chunk),
        grid=(t // tm, n_exp // te),
        in_specs=[pl.BlockSpec((tm, d), lambda i, j: (i, 0), pipeline_mode=once),
                  pl.BlockSpec((te, d), lambda i, j: (j, 0)),
                  pl.BlockSpec((te, d), lambda i, j: (j, 0)),
                  pl.BlockSpec((tm, te), lambda i, j: (i, j))],
        out_specs=pl.BlockSpec((tm, d), lambda i, j: (i, 0), pipeline_mode=once),
        out_shape=jax.ShapeDtypeStruct((t, d), F32),
        compiler_params=_params(("arbitrary", "arbitrary"), est),
        name="peer_experts",
    )(u, eu, ev, w)


def _filter_features(seq):
    t = jnp.linspace(0.0, 1.0, seq, dtype=F32)[:, None]
    bands = (FILTER_EMB - 1) // 2
    w = 2.0 * math.pi * jnp.arange(seq, dtype=F32)[:, None] / seq
    f = jnp.linspace(1e-4, bands - 1, bands, dtype=F32)[None, :]
    z = jnp.concatenate([t, jnp.cos(f * w), -jnp.sin(f * w)], axis=-1)
    return jnp.pad(z, ((0, 0), (0, FILTER_PAD - FILTER_EMB)))


@functools.lru_cache(maxsize=None)
def _dft_matrices(seq):
    n2 = 2 * seq
    f = np.arange(seq, dtype=np.int64)[:, None]
    n = np.arange(seq, dtype=np.int64)[None, :]
    ang = ((f * n) % n2).astype(np.float64) * (2.0 * math.pi / n2)
    sin = np.where(f == 0, (1 - 2 * (n % 2)).astype(np.float64), np.sin(ang))
    return np.concatenate([np.cos(ang), sin], axis=0).astype(np.float32)


def kernel(x, c, ctx, c_ctx, w_ada, b_ada, w_in, conv_w, conv_b, filt_w1, filt_b1, filt_w2, filt_b2,
           filt_w3, filt_b3, filt_w4, filt_freq, hy_skip, na_rpb, hy_norm_g, na_norm_g, w_out,
           ln1_g, ln1_b, peer_wq, peer_keys, peer_u, peer_v, ln2_g, ln2_b):
    depth = w_ada.shape[0]
    assert depth == 1, "single-layer block: the context stream is never updated"
    bsz, seq, d = x.shape
    rows = seq // GRID_W
    kh = min(NA_KH, rows)
    assert kh == NA_KH and seq % GRID_W == 0
    alpha = (2.0 * depth) ** 0.25
    d_na = d - D_HY
    off_q = 3 * D_HY
    off_k = off_q + d_na

    n_cond = 8
    cond = jnp.concatenate([c, c_ctx[None], jnp.zeros((n_cond - bsz - 1, d), F32)], axis=0)
    mod = _ada(cond, w_ada[0], b_ada[0]).reshape(n_cond, 6, d)

    w_cols = lambda lo, hi: w_in[0][:, lo:hi].astype(BF16)
    z_hy = _proj(x, mod, lambda b: b, w_cols(0, off_q), F32, 512)
    qkv = _proj(x, mod, lambda b: b, w_cols(off_q, None), BF16, 512)
    kv_ctx = _proj(ctx, mod, lambda b: bsz, w_cols(off_k, None), BF16, 256)

    deltas = jnp.abs(jnp.linspace(math.log(DECAY_TARGET) / SLOW_DECAY_PCT,
                                  math.log(DECAY_TARGET) / FAST_DECAY_PCT, D_HY, dtype=F32))[None, :]
    padw = lambda a: jnp.pad(a, ((0, FILTER_PAD - a.shape[0]), (0, FILTER_PAD - a.shape[1])))
    padv = lambda a: jnp.pad(a, ((0, FILTER_PAD - a.shape[0]),))
    w4 = jnp.pad(filt_w4[0], ((0, FILTER_PAD - filt_w4.shape[1]), (0, 0)))
    freq = jnp.pad(filt_freq[0], ((0, 0), (0, FILTER_PAD - filt_freq.shape[2])))
    ksum, kdiff = _hyena_filters(_filter_features(seq), padw(filt_w1[0]), padv(filt_b1[0]), padw(filt_w2[0]),
                                 padv(filt_b2[0]), padw(filt_w3[0]), padv(filt_b3[0]), w4, freq, deltas, 256)
    ffwd = jnp.asarray(_dft_matrices(seq)).astype(BF16)
    finv = ffwd.T
    kp, kq = _spectrum(ffwd, ksum, kdiff, 256)
    hy_out = _hyena(z_hy, conv_w[0], conv_b[0], ffwd, finv, kp, kq, hy_skip[0], 256)

    tables = _bias_tables(na_rpb[0])
    bands = jnp.stack([jnp.concatenate([tables[:, o + i] for i in range(kh)], axis=-1) for o in range(kh)], axis=1)
    band_of_row = lambda r: jnp.clip(r - kh // 2, 0, rows - kh) - r + (NA_KH - 1)
    na_out = _attention(qkv, kv_ctx, bands, band_of_row, rows, kh)

    x_mid, u_peer = _merge(hy_out, na_out, x, mod, w_out[0].astype(BF16), hy_norm_g[0], na_norm_g[0],
                           ln1_g[0], ln1_b[0], alpha, 256)

    t = bsz * seq
    u2 = u_peer.reshape(t, d)
    idx_t, gate_t = _route(u2, peer_wq[0].astype(BF16), peer_keys[0], 1024)
    w_gate = _gate_matrix(idx_t.reshape(-1, t).T, gate_t.reshape(-1, t).T, 64)
    f = _peer(u2, peer_u[0].astype(BF16), peer_v[0].astype(BF16), w_gate, 2048, 512, 512)
    out = _final_ln(x_mid.reshape(t, d), f, mod, ln2_g[0], ln2_b[0], alpha, seq, 512)
    return out.reshape(bsz, seq, d)
```

```python
import functools
import math

import numpy as np
import jax
import jax.numpy as jnp
from jax import lax
from jax.experimental import pallas as pl
from jax.experimental.pallas import tpu as pltpu

F32 = jnp.float32
BF16 = jnp.bfloat16
HIGHEST = lax.Precision.HIGHEST

GRID_W = 64
HEAD_DIM = 64
D_HY = 1024
HY_ORDER = 2
FILTER_EMB = 33
FILTER_PAD = 128
DECAY_TARGET = 1e-2
FAST_DECAY_PCT = 0.3
SLOW_DECAY_PCT = 1.5
NA_KH = 8
NA_KW = 16
PEER_HEADS = 8
PEER_NKEYS = 128
NKEYS_SHIFT = PEER_NKEYS.bit_length() - 1
assert 1 << NKEYS_SHIFT == PEER_NKEYS
PEER_TOPK = 16
LN_EPS = 1e-5
NEG_INF = -1e30

V7X_VMEM_BYTES = 64 * 1024 * 1024
V7X_LANES = 128
V7X_MXU_DEPTH = 256
NA_GROUP = V7X_MXU_DEPTH // HEAD_DIM


def _vmem_limit(estimate_bytes):
    return int(min(estimate_bytes + 12 * 1024 * 1024, V7X_VMEM_BYTES - 6 * 1024 * 1024))


def _params(semantics, estimate_bytes):
    return pltpu.CompilerParams(dimension_semantics=semantics, vmem_limit_bytes=_vmem_limit(estimate_bytes))


def _ada_kernel(cond_ref, w_ref, b_ref, o_ref):
    s = jax.nn.silu(cond_ref[...])
    o_ref[...] = jnp.dot(s, w_ref[...], precision=HIGHEST, preferred_element_type=F32) + b_ref[...]


def _ada(cond, w, b):
    n, d = cond.shape
    tn = 1024
    return pl.pallas_call(
        _ada_kernel,
        grid=(w.shape[1] // tn,),
        in_specs=[pl.BlockSpec((n, d), lambda j: (0, 0)),
                  pl.BlockSpec((d, tn), lambda j: (0, j)),
                  pl.BlockSpec((1, tn), lambda j: (0, j))],
        out_specs=pl.BlockSpec((n, tn), lambda j: (0, j)),
        out_shape=jax.ShapeDtypeStruct((n, w.shape[1]), F32),
        compiler_params=_params(("arbitrary",), 2 * d * tn * 4),
        name="ada",
    )(cond, w, b.reshape(1, -1))


def _proj_kernel(x_ref, mod_ref, w_ref, o_ref):
    u = x_ref[...] * (1.0 + mod_ref[1:2, :]) + mod_ref[0:1, :]
    o_ref[...] = jnp.dot(u.astype(BF16), w_ref[...], preferred_element_type=F32).astype(o_ref.dtype)


def _proj(x, mod, mod_row, w, out_dtype, tm):
    bsz, seq, d = x.shape
    n = w.shape[1]
    est = d * n * 2 + 2 * tm * d * 4 + 2 * tm * n * 4 + tm * n * 4
    return pl.pallas_call(
        _proj_kernel,
        grid=(bsz, seq // tm),
        in_specs=[pl.BlockSpec((None, tm, d), lambda b, i: (b, i, 0)),
                  pl.BlockSpec((None, 6, d), lambda b, i: (mod_row(b), 0, 0)),
                  pl.BlockSpec((d, n), lambda b, i: (0, 0))],
        out_specs=pl.BlockSpec((None, tm, n), lambda b, i: (b, i, 0)),
        out_shape=jax.ShapeDtypeStruct((bsz, seq, n), out_dtype),
        compiler_params=_params(("arbitrary", "arbitrary"), est),
        name="proj",
    )(x, mod, w)


def _filter_kernel(z_ref, w1_ref, b1_ref, w2_ref, b2_ref, w3_ref, b3_ref, fr_ref, w4f_ref, w4b_ref, dl_ref,
                   ks_ref, kd_ref, h_ref):
    dot = functools.partial(jnp.dot, precision=HIGHEST, preferred_element_type=F32)

    @pl.when((pl.program_id(0) == 0) & (pl.program_id(1) == 0))
    def _():
        h = jnp.sin(fr_ref[0:1, :] * (dot(z_ref[...], w1_ref[...]) + b1_ref[...]))
        h = jnp.sin(fr_ref[1:2, :] * (dot(h, w2_ref[...]) + b2_ref[...]))
        h_ref[...] = jnp.sin(fr_ref[2:3, :] * (dot(h, w3_ref[...]) + b3_ref[...]))

    h = h_ref[...]
    decay = jnp.exp(-z_ref[:, 0:1] * dl_ref[...])
    hf = dot(h, w4f_ref[...]) * decay
    hb = dot(h, w4b_ref[...]) * decay
    norm = jnp.sum(jnp.abs(hf) + jnp.abs(hb), axis=0, keepdims=True)
    hf = hf / norm
    hb = hb / norm
    row = lax.broadcasted_iota(jnp.int32, hb.shape, 0)
    hb = jnp.where(row == 0, 0.0, hb)
    ks_ref[...] = hf + hb
    kd_ref[...] = hb - hf


def _hyena_filters(zfeat, w1, b1, w2, b2, w3, b3, w4, freq, deltas, tc):
    seq = zfeat.shape[0]
    fo = w2.shape[0]
    nct = D_HY // tc
    full = lambda a: pl.BlockSpec(a.shape, lambda o, j: (0,) * a.ndim)
    args = (zfeat, w1, b1.reshape(1, -1), w2, b2.reshape(1, -1), w3, b3.reshape(1, -1), freq)
    out_spec = pl.BlockSpec((None, seq, tc), lambda o, j: (o, 0, j))
    return pl.pallas_call(
        _filter_kernel,
        grid=(HY_ORDER, nct),
        in_specs=[full(a) for a in args] + [
            pl.BlockSpec((fo, tc), lambda o, j: (0, (2 * o) * nct + j)),
            pl.BlockSpec((fo, tc), lambda o, j: (0, (2 * o + 1) * nct + j)),
            pl.BlockSpec((1, tc), lambda o, j: (0, j))],
        out_specs=[out_spec, out_spec],
        out_shape=[jax.ShapeDtypeStruct((HY_ORDER, seq, D_HY), F32)] * 2,
        scratch_shapes=[pltpu.VMEM((seq, fo), F32)],
        compiler_params=_params(("arbitrary", "arbitrary"), 12 * seq * tc * 4),
        name="hyena_filter",
    )(*args, w4, w4, deltas)


def _spectrum_kernel(ff_ref, ks_ref, kd_ref, kp_ref, kq_ref):
    seq = ks_ref.shape[0]
    ks = ks_ref[...].astype(BF16)
    kd = kd_ref[...].astype(BF16)
    p = jnp.dot(ff_ref[0:seq, :], ks, preferred_element_type=F32)
    q = jnp.dot(ff_ref[seq:2 * seq, :], kd, preferred_element_type=F32)
    nyq = jnp.dot(ff_ref[seq:seq + 16, :], ks, preferred_element_type=F32)[0:1, :]
    row = lax.broadcasted_iota(jnp.int32, p.shape, 0)
    wf = jnp.where(row == 0, 0.5 / seq, 1.0 / seq)
    kp_ref[...] = p * wf
    kq_ref[...] = jnp.where(row == 0, nyq, q) * wf


def _spectrum(ffwd, ksum, kdiff, tc):
    seq = ksum.shape[1]
    spec = pl.BlockSpec((None, seq, tc), lambda o, j: (o, 0, j))
    return pl.pallas_call(
        _spectrum_kernel,
        grid=(HY_ORDER, D_HY // tc),
        in_specs=[pl.BlockSpec(ffwd.shape, lambda o, j: (0, 0)), spec, spec],
        out_specs=[spec, spec],
        out_shape=[jax.ShapeDtypeStruct(ksum.shape, F32)] * 2,
        compiler_params=_params(("arbitrary", "arbitrary"), ffwd.size * 2 + 12 * seq * tc * 4),
        name="hyena_spectrum",
    )(ffwd, ksum, kdiff)


def _short_conv(z, w, b):
    seq = z.shape[0]
    row = lax.broadcasted_iota(jnp.int32, z.shape, 0)
    prev = jnp.where(row == 0, 0.0, pltpu.roll(z, 1, 0))
    nxt = jnp.where(row == seq - 1, 0.0, pltpu.roll(z, seq - 1, 0))
    return prev * w[0:1, :] + z * w[1:2, :] + nxt * w[2:3, :] + b


def _dft_mul(u, ff_ref, kp_ref, kq_ref, o_ref):
    seq = u.shape[0]
    a = jnp.dot(ff_ref[...], u.astype(BF16), preferred_element_type=F32)
    p, q = a[0:seq], a[seq:2 * seq]
    kp, kq = kp_ref[...], kq_ref[...]
    row0 = lax.broadcasted_iota(jnp.int32, p.shape, 0) == 0
    o_ref[0:seq, :] = jnp.where(row0, kp * p, kp * p + kq * q).astype(o_ref.dtype)
    o_ref[seq:2 * seq, :] = jnp.where(row0, kq * q, kp * q - kq * p).astype(o_ref.dtype)


def _fwd_conv_kernel(z_ref, cw_ref, cb_ref, ff_ref, kp_ref, kq_ref, o_ref):
    _dft_mul(_short_conv(z_ref[...], cw_ref[...], cb_ref[...]), ff_ref, kp_ref, kq_ref, o_ref)


def _fwd_plain_kernel(u_ref, ff_ref, kp_ref, kq_ref, o_ref):
    _dft_mul(u_ref[...], ff_ref, kp_ref, kq_ref, o_ref)


def _inv_first_kernel(y_ref, fi_ref, zv_ref, zx_ref, cwv_ref, cbv_ref, cwx_ref, cbx_ref, skip_ref, o_ref):
    v = _short_conv(zv_ref[...], cwv_ref[...], cbv_ref[...])
    x1 = _short_conv(zx_ref[...], cwx_ref[...], cbx_ref[...])
    y = jnp.dot(fi_ref[...], y_ref[...], preferred_element_type=F32)
    o_ref[...] = x1 * (y + v * skip_ref[...])


def _inv_second_kernel(y_ref, fi_ref, g_ref, zx_ref, cwx_ref, cbx_ref, skip_ref, o_ref):
    x2 = _short_conv(zx_ref[...], cwx_ref[...], cbx_ref[...])
    y = jnp.dot(fi_ref[...], y_ref[...], preferred_element_type=F32)
    o_ref[...] = x2 * (y + g_ref[...] * skip_ref[...])


def _hyena(z, conv_w, conv_b, ffwd, finv, kp, kq, skip, tc):
    bsz, seq, _ = z.shape
    nct = D_HY // tc
    sem = ("arbitrary", "arbitrary")
    zcol = lambda g: pl.BlockSpec((None, seq, tc), lambda b, j: (b, 0, g * nct + j))
    wcol = lambda g: pl.BlockSpec((3, tc), lambda b, j: (0, g * nct + j))
    bcol = lambda g: pl.BlockSpec((1, tc), lambda b, j: (0, g * nct + j))
    act = pl.BlockSpec((None, seq, tc), lambda b, j: (b, 0, j))
    spec2 = pl.BlockSpec((None, 2 * seq, tc), lambda b, j: (b, 0, j))
    kspec = lambda o: pl.BlockSpec((None, seq, tc), lambda b, j: (o, 0, j))
    sspec = lambda o: pl.BlockSpec((None, 1, tc), lambda b, j: (o, 0, j))
    mat = pl.BlockSpec(ffwd.shape, lambda b, j: (0, 0))
    mati = pl.BlockSpec(finv.shape, lambda b, j: (0, 0))
    est = ffwd.size * 2 + 14 * seq * tc * 4
    cb = conv_b.reshape(1, -1)
    skip3 = skip.reshape(HY_ORDER, 1, D_HY)
    spec_shape = jax.ShapeDtypeStruct((bsz, 2 * seq, D_HY), BF16)
    act_shape = jax.ShapeDtypeStruct((bsz, seq, D_HY), F32)

    y1 = pl.pallas_call(
        _fwd_conv_kernel, grid=(bsz, nct),
        in_specs=[zcol(0), wcol(0), bcol(0), mat, kspec(0), kspec(0)],
        out_specs=spec2, out_shape=spec_shape, compiler_params=_params(sem, est), name="hyena_fwd1",
    )(z, conv_w, cb, ffwd, kp, kq)
    g = pl.pallas_call(
        _inv_first_kernel, grid=(bsz, nct),
        in_specs=[spec2, mati, zcol(0), zcol(1), wcol(0), bcol(0), wcol(1), bcol(1), sspec(0)],
        out_specs=act, out_shape=act_shape, compiler_params=_params(sem, est), name="hyena_inv1",
    )(y1, finv, z, z, conv_w, cb, conv_w, cb, skip3)
    y2 = pl.pallas_call(
        _fwd_plain_kernel, grid=(bsz, nct),
        in_specs=[act, mat, kspec(1), kspec(1)],
        out_specs=spec2, out_shape=spec_shape, compiler_params=_params(sem, est), name="hyena_fwd2",
    )(g, ffwd, kp, kq)
    return pl.pallas_call(
        _inv_second_kernel, grid=(bsz, nct),
        in_specs=[spec2, mati, act, zcol(2), wcol(2), bcol(2), sspec(1)],
        out_specs=act, out_shape=act_shape, compiler_params=_params(sem, est), name="hyena_inv2",
    )(y2, finv, g, z, conv_w, cb, skip3)


def _bias_kernel(rpb_ref, o_ref):
    h = pl.program_id(0)
    n_r, n_c = 2 * NA_KH - 1, 2 * NA_KW - 1
    cq = lax.broadcasted_iota(jnp.int32, (GRID_W, GRID_W), 0)
    ck = lax.broadcasted_iota(jnp.int32, (GRID_W, GRID_W), 1)
    rel = ck - jnp.clip(cq - NA_KW // 2, 0, GRID_W - NA_KW)
    ok = (rel >= 0) & (rel < NA_KW)
    dcol = jnp.clip(ck - cq + NA_KW - 1, 0, n_c - 1)
    for r in range(n_r):
        acc = jnp.zeros((GRID_W, GRID_W), F32)
        for c in range(n_c):
            acc = jnp.where(dcol == c, rpb_ref[(h * n_r + r) * n_c + c], acc)
        o_ref[r] = jnp.where(ok, acc, NEG_INF)


def _bias_tables(rpb):
    heads = rpb.shape[0]
    n_r = 2 * NA_KH - 1
    return pl.pallas_call(
        _bias_kernel,
        grid=(heads,),
        in_specs=[pl.BlockSpec(memory_space=pltpu.SMEM)],
        out_specs=pl.BlockSpec((None, n_r, GRID_W, GRID_W), lambda h: (h, 0, 0, 0)),
        out_shape=jax.ShapeDtypeStruct((heads, n_r, GRID_W, GRID_W), F32),
        compiler_params=_params(("arbitrary",), 1 << 20),
        name="na_bias",
    )(rpb.reshape(-1))


def _attn_kernel(q_ref, k_ref, v_ref, kc_ref, vc_ref, bias0_ref, bias1_ref, o_ref, *, rows, kh):
    r2 = pl.program_id(1)
    nk = kh * GRID_W
    gw = NA_GROUP * HEAD_DIM
    lane = lax.broadcasted_iota(jnp.int32, (1, gw), 1)
    head_of_lane = [(lane >= e * HEAD_DIM) & (lane < (e + 1) * HEAD_DIM) for e in range(NA_GROUP)]
    nt = (((1,), (1,)), ((), ()))
    starts = [pl.multiple_of(jnp.clip(2 * r2 + i - kh // 2, 0, rows - kh) * GRID_W, GRID_W) for i in range(2)]
    bias_refs = (bias0_ref, bias1_ref)
    per_row = NA_GROUP * GRID_W
    zero = jnp.zeros((GRID_W, gw), BF16)
    for hg in range(q_ref.shape[1] // gw):
        cols = slice(hg * gw, (hg + 1) * gw)
        qp = q_ref[:, cols] * jnp.asarray(HEAD_DIM ** -0.5, BF16)
        qm = jnp.concatenate([jnp.where(head_of_lane[e], qp[i * GRID_W:(i + 1) * GRID_W], zero)
                              for i in range(2) for e in range(NA_GROUP)], axis=0)
        s_ctx = lax.dot_general(qm, kc_ref[:, cols], nt, preferred_element_type=F32)
        m_ctx = jnp.max(s_ctx, axis=-1, keepdims=True)
        s_lat, m_all = [], []
        for i in range(2):
            blk = slice(i * per_row, (i + 1) * per_row)
            bias = jnp.concatenate([bias_refs[i][hg * NA_GROUP + e] for e in range(NA_GROUP)], axis=0)
            s = lax.dot_general(qm[blk], k_ref[pl.ds(starts[i], nk), cols], nt, preferred_element_type=F32) + bias
            s_lat.append(s)
            m_all.append(jnp.maximum(jnp.max(s, axis=-1, keepdims=True), m_ctx[blk]))
        p_ctx = jnp.exp(s_ctx - jnp.concatenate(m_all, axis=0))
        o_ctx = jnp.dot(p_ctx.astype(BF16), vc_ref[:, cols], preferred_element_type=F32)
        l_ctx = jnp.sum(p_ctx, axis=-1, keepdims=True)
        for i in range(2):
            blk = slice(i * per_row, (i + 1) * per_row)
            p = jnp.exp(s_lat[i] - m_all[i])
            o = jnp.dot(p.astype(BF16), v_ref[pl.ds(starts[i], nk), cols], preferred_element_type=F32) + o_ctx[blk]
            o = o / (jnp.sum(p, axis=-1, keepdims=True) + l_ctx[blk])
            out = o[0:GRID_W]
            for e in range(1, NA_GROUP):
                out = jnp.where(head_of_lane[e], o[e * GRID_W:(e + 1) * GRID_W], out)
            o_ref[i * GRID_W:(i + 1) * GRID_W, cols] = out


def _attention(qkv, kv_ctx, bands, band_of_row, rows, kh):
    bsz, seq, d3 = qkv.shape
    dn = d3 // 3
    ctx_len = kv_ctx.shape[1]
    heads, _, _, nk = bands.shape
    assert rows % 2 == 0
    est = 2 * (2 * seq * dn * 2 + 2 * ctx_len * dn * 2 + 2 * heads * GRID_W * nk * 4 + 2 * GRID_W * dn * 6)
    band = lambda i: pl.BlockSpec((heads, None, GRID_W, nk), lambda b, r: (0, band_of_row(2 * r + i), 0, 0))
    return pl.pallas_call(
        functools.partial(_attn_kernel, rows=rows, kh=kh),
        grid=(bsz, rows // 2),
        in_specs=[pl.BlockSpec((None, 2 * GRID_W, dn), lambda b, r: (b, r, 0)),
                  pl.BlockSpec((None, seq, dn), lambda b, r: (b, 0, 1)),
                  pl.BlockSpec((None, seq, dn), lambda b, r: (b, 0, 2)),
                  pl.BlockSpec((None, ctx_len, dn), lambda b, r: (b, 0, 0)),
                  pl.BlockSpec((None, ctx_len, dn), lambda b, r: (b, 0, 1)),
                  band(0), band(1)],
        out_specs=pl.BlockSpec((None, 2 * GRID_W, dn), lambda b, r: (b, r, 0)),
        out_shape=jax.ShapeDtypeStruct((bsz, seq, dn), F32),
        compiler_params=_params(("arbitrary", "arbitrary"), est),
        name="na_attention",
    )(qkv, qkv, qkv, kv_ctx, kv_ctx, bands, bands)


def _rms(x, g):
    return x * lax.rsqrt(jnp.mean(x * x, axis=-1, keepdims=True) + LN_EPS) * g


def _layer_norm(x, g, b):
    xc = x - jnp.mean(x, axis=-1, keepdims=True)
    var = jnp.mean(xc * xc, axis=-1, keepdims=True)
    return xc * lax.rsqrt(var + LN_EPS) * g + b


def _merge_kernel(hy_ref, na_ref, x_ref, mod_ref, w_ref, ghy_ref, gna_ref, lg_ref, lb_ref, xo_ref, uo_ref, *, alpha):
    dh = hy_ref.shape[1]
    hy = _rms(hy_ref[...], ghy_ref[...]).astype(BF16)
    na = _rms(na_ref[...], gna_ref[...]).astype(BF16)
    y = (jnp.dot(hy, w_ref[0:dh, :], preferred_element_type=F32)
         + jnp.dot(na, w_ref[dh:, :], preferred_element_type=F32))
    xn = _layer_norm(alpha * x_ref[...] + mod_ref[2:3, :] * y, lg_ref[...], lb_ref[...])
    xo_ref[...] = xn
    uo_ref[...] = (xn * (1.0 + mod_ref[4:5, :]) + mod_ref[3:4, :]).astype(uo_ref.dtype)


def _merge(hy, na, x, mod, w_out, g_hy, g_na, ln_g, ln_b, alpha, tm):
    bsz, seq, d = x.shape
    dh, dn = hy.shape[2], na.shape[2]
    row = lambda a: a.reshape(1, -1)
    vec = lambda n: pl.BlockSpec((1, n), lambda b, i: (0, 0))
    tile = lambda n: pl.BlockSpec((None, tm, n), lambda b, i: (b, i, 0))
    est = d * d * 2 + 2 * tm * (dh + dn + 2 * d) * 4 + 2 * tm * d * 6 + 4 * tm * d * 4
    return pl.pallas_call(
        functools.partial(_merge_kernel, alpha=alpha),
        grid=(bsz, seq // tm),
        in_specs=[tile(dh), tile(dn), tile(d),
                  pl.BlockSpec((None, 6, d), lambda b, i: (b, 0, 0)),
                  pl.BlockSpec((d, d), lambda b, i: (0, 0)),
                  vec(dh), vec(dn), vec(d), vec(d)],
        out_specs=[tile(d), tile(d)],
        out_shape=[jax.ShapeDtypeStruct((bsz, seq, d), F32), jax.ShapeDtypeStruct((bsz, seq, d), BF16)],
        compiler_params=_params(("arbitrary", "arbitrary"), est),
        name="merge",
    )(hy, na, x, mod, w_out, row(g_hy), row(g_na), row(ln_g), row(ln_b))


def _first(a, b):
    return (a[0] > b[0]) | ((a[0] == b[0]) & (a[1] < b[1]))


def _order_pair(xs, i, j):
    a, b = xs[i], xs[j]
    f = _first(a, b)
    xs[i] = (jnp.maximum(a[0], b[0]),) + tuple(jnp.where(f, p, r) for p, r in zip(a[1:], b[1:]))
    xs[j] = (jnp.minimum(a[0], b[0]),) + tuple(jnp.where(f, r, p) for p, r in zip(a[1:], b[1:]))


def _bitonic_merge(xs):
    n = len(xs)
    j = n // 2
    while j >= 1:
        for i in range(n):
            if i & j == 0:
                _order_pair(xs, i, i | j)
        j //= 2
    return xs


def _sort(xs):
    n = len(xs)
    p = 1
    while p < n:
        k = p
        while k >= 1:
            for j in range(k % p, n - k, 2 * k):
                for i in range(min(k, n - j - k)):
                    if (i + j) // (2 * p) == (i + j + k) // (2 * p):
                        _order_pair(xs, i + j, i + j + k)
            k //= 2
        p *= 2
    return xs


def _leading_half(a, b):
    out = []
    for x, y in zip(a, reversed(b)):
        f = _first(x, y)
        out.append((jnp.maximum(x[0], y[0]),) + tuple(jnp.where(f, p, r) for p, r in zip(x[1:], y[1:])))
    return out


def _sorted_top(items, k):
    runs = [_sort(items[i:i + k]) for i in range(0, len(items), k)]
    while len(runs) > 1:
        runs = [_bitonic_merge(_leading_half(runs[i], runs[i + 1])) for i in range(0, len(runs), 2)]
    return runs[0]


def _pitch(rows):
    return rows + 8


def _route_kernel(u_ref, wq_ref, keys_ref, idx_ref, gate_ref, s_ref):
    nkeys, half = keys_ref.shape[1], keys_ref.shape[2]
    k = PEER_TOPK
    groups = u_ref.shape[0] // V7X_LANES
    pitch = _pitch(nkeys)
    tile = (groups, V7X_LANES)
    q = jnp.dot(u_ref[...], wq_ref[...], preferred_element_type=F32)
    nt = (((1,), (1,)), ((), ()))
    def split(v):
        hi = v.astype(BF16)
        return hi, (v - hi.astype(F32)).astype(BF16)

    for p in range(2):
        k_hi, k_lo = split(keys_ref[p])
        q_hi, q_lo = split(q[:, p * half:(p + 1) * half])
        dot = lambda a, b: lax.dot_general(a, b, nt, preferred_element_type=F32)
        s = dot(k_hi, q_hi) + (dot(k_hi, q_lo) + dot(k_lo, q_hi))
        for g in range(groups):
            s_ref[p, g * pitch:g * pitch + nkeys, :] = s[:, g * V7X_LANES:(g + 1) * V7X_LANES]

    def scores(p):
        return [(s_ref[p, pl.ds(n, groups, stride=pitch), :], jnp.full(tile, float(n), F32)) for n in range(nkeys)]

    top_a = _sorted_top(scores(0), k)
    top_b = _sorted_top(scores(1), k)

    def cand(i, j):
        return (top_a[i][0] + top_b[j][0], jnp.full(tile, float(i * k + j), F32),
                top_a[i][1] * float(nkeys) + top_b[j][1])

    pad = (jnp.full(tile, -jnp.inf, F32), jnp.full(tile, float(k * k), F32), jnp.zeros(tile, F32))
    run = lambda i: [cand(i, j) for j in range(k // (i + 1))]
    tail = [cand(i, 0) for i in range(k // 2, k)]
    assert k == 16
    r0 = run(0)
    r1 = _bitonic_merge(run(1) + tail[::-1])
    r2 = _sort(run(2) + run(3) + run(4) + run(5) + run(6))
    r3 = run(7) + [pad] * (k - 2)
    best = _leading_half(_bitonic_merge(_leading_half(r0, r1)), _bitonic_merge(_leading_half(r2, r3)))

    m = functools.reduce(jnp.maximum, [c[0] for c in best])
    e = [jnp.exp(c[0] - m) for c in best]
    inv = 1.0 / functools.reduce(jnp.add, e)
    for n in range(k):
        idx_ref[n] = best[n][2].astype(jnp.int32)
        gate_ref[n] = e[n] * inv


def _route(u, wq, keys, tm):
    t, d = u.shape
    heads, _, nkeys, half = keys.shape
    groups = tm // V7X_LANES
    est = 2 * (tm * d * 2 + d * 2 * half * 2 + 2 * nkeys * half * 4) + 8 * nkeys * tm * 4
    out_spec = pl.BlockSpec((PEER_TOPK, groups, V7X_LANES), lambda i, h: (h, i, 0))
    return pl.pallas_call(
        _route_kernel,
        grid=(t // tm, heads),
        in_specs=[pl.BlockSpec((tm, d), lambda i, h: (i, 0)),
                  pl.BlockSpec((d, 2 * half), lambda i, h: (0, h)),
                  pl.BlockSpec((None, 2, nkeys, half), lambda i, h: (h, 0, 0, 0))],
        out_specs=[out_spec, out_spec],
        out_shape=[jax.ShapeDtypeStruct((heads * PEER_TOPK, t // V7X_LANES, V7X_LANES), jnp.int32),
                   jax.ShapeDtypeStruct((heads * PEER_TOPK, t // V7X_LANES, V7X_LANES), F32)],
        scratch_shapes=[pltpu.VMEM((2, groups * _pitch(nkeys), V7X_LANES), F32)],
        compiler_params=_params(("arbitrary", "arbitrary"), est),
        name="peer_route",
    )(u, wq, keys)


def _gate_matrix_kernel(idx_ref, gate_ref, o_ref, w0_ref, w1_ref):
    step = pl.program_id(0)

    @pl.when(step == 0)
    def _():
        w1_ref[...] = jnp.zeros_like(w1_ref)

    @pl.when(step % 2 == 0)
    def _():
        _gate_matrix_step(idx_ref, gate_ref, o_ref, w0_ref, w1_ref)

    @pl.when(step % 2 == 1)
    def _():
        _gate_matrix_step(idx_ref, gate_ref, o_ref, w1_ref, w0_ref)


def _gate_matrix_step(idx_ref, gate_ref, o_ref, cur_ref, prev_ref):
    tw, nsel = idx_ref.shape
    pitch = _pitch(PEER_NKEYS)
    key = lax.broadcasted_iota(jnp.int32, (PEER_NKEYS, nsel), 0)
    nt = (((1,), (1,)), ((), ()))
    zero = jnp.zeros((PEER_NKEYS, nsel), BF16)
    n_pairs = tw // 2
    planes_per_pair = PEER_NKEYS // n_pairs
    for p in range(n_pairs):
        a_hot, b_hot = [], []
        for t in (2 * p, 2 * p + 1):
            idx = idx_ref[t:t + 1, :]
            gate = gate_ref[t:t + 1, :]
            a_hot.append(jnp.where((idx >> NKEYS_SHIFT) == key, gate, 0.0).astype(BF16))
            b_hot.append(jnp.where((idx & (PEER_NKEYS - 1)) == key, 1.0, 0.0).astype(BF16))
        lhs = jnp.concatenate(a_hot, axis=1)
        rhs = jnp.concatenate([jnp.concatenate([b_hot[0], zero], axis=1),
                               jnp.concatenate([zero, b_hot[1]], axis=1)], axis=0)
        planes = lax.dot_general(lhs, rhs, nt, preferred_element_type=F32)
        for k in range(2):
            row = (2 * p + k) * pitch
            cur_ref[row:row + PEER_NKEYS, :] = planes[:, k * PEER_NKEYS:(k + 1) * PEER_NKEYS]
        for a in range(p * planes_per_pair, (p + 1) * planes_per_pair):
            o_ref[:, a * PEER_NKEYS:(a + 1) * PEER_NKEYS] = (
                prev_ref[pl.ds(a, tw, stride=pitch), :].astype(o_ref.dtype))


def _gate_matrix(idx, gate, tw):
    t, nsel = idx.shape
    n_exp = PEER_NKEYS * PEER_NKEYS
    n_blocks = t // tw
    assert PEER_NKEYS % (tw // 2) == 0
    est = 2 * (2 * tw * nsel * 4 + tw * n_exp * 2) + 2 * tw * _pitch(PEER_NKEYS) * PEER_NKEYS * 4
    block_in = pl.BlockSpec((tw, nsel), lambda i: (jnp.minimum(i, n_blocks - 1), 0))
    return pl.pallas_call(
        _gate_matrix_kernel,
        grid=(n_blocks + 1,),
        in_specs=[block_in, block_in],
        out_specs=pl.BlockSpec((tw, n_exp), lambda i: (jnp.maximum(i - 1, 0), 0)),
        out_shape=jax.ShapeDtypeStruct((t, n_exp), BF16),
        scratch_shapes=[pltpu.VMEM((tw * _pitch(PEER_NKEYS), PEER_NKEYS), F32)] * 2,
        compiler_params=_params(("arbitrary",), est),
        name="peer_gate_matrix",
    )(idx, gate)


def _peer_kernel(u_ref, eu_ref, ev_ref, w_ref, o_ref, *, n_chunk):
    @pl.when(pl.program_id(1) == 0)
    def _():
        o_ref[...] = jnp.zeros_like(o_ref)

    z = lax.dot_general(u_ref[...], eu_ref[...], (((1,), (1,)), ((), ())), preferred_element_type=F32)
    act = 0.5 * z * (1.0 + lax.erf(z * (2.0 ** -0.5)))
    h = (w_ref[...].astype(F32) * act).astype(BF16)
    for c in range(0, o_ref.shape[1], n_chunk):
        o_ref[:, c:c + n_chunk] += jnp.dot(h, ev_ref[:, c:c + n_chunk], preferred_element_type=F32)


def _final_ln_kernel(x_ref, f_ref, mod_ref, lg_ref, lb_ref, o_ref, *, alpha):
    r = alpha * x_ref[...] + mod_ref[5:6, :] * f_ref[...]
    o_ref[...] = _layer_norm(r, lg_ref[...], lb_ref[...])


def _final_ln(x, f, mod, ln_g, ln_b, alpha, seq, tm):
    t, d = x.shape
    tile = pl.BlockSpec((tm, d), lambda i: (i, 0))
    vec = pl.BlockSpec((1, d), lambda i: (0, 0))
    return pl.pallas_call(
        functools.partial(_final_ln_kernel, alpha=alpha),
        grid=(t // tm,),
        in_specs=[tile, tile, pl.BlockSpec((None, 6, d), lambda i: ((i * tm) // seq, 0, 0)), vec, vec],
        out_specs=tile,
        out_shape=jax.ShapeDtypeStruct((t, d), F32),
        compiler_params=_params(("arbitrary",), 8 * tm * d * 4),
        name="final_ln",
    )(x, f, mod, ln_g.reshape(1, -1), ln_b.reshape(1, -1))


def _peer(u, eu, ev, w, tm, te, n_chunk):
    t, d = u.shape
    n_exp = ev.shape[0]
    once = pl.Buffered(1)
    est = tm * d * 2 + tm * d * 4 + 2 * (2 * te * d * 2 + tm * te * 2) + 4 * tm * te * 4 + tm * n_chunk * 4
    return pl.pallas_call(
        functools.partial(_peer_kernel, n_chunk=n_chunk),
        grid=(t // tm, n_exp // te),
        in_specs=[pl.BlockSpec((tm, d), lambda i, j: (i, 0), pipeline_mode=once),
                  pl.BlockSpec((te, d), lambda i, j: (j, 0)),
                  pl.BlockSpec((te, d), lambda i, j: (j, 0)),
                  pl.BlockSpec((tm, te), lambda i, j: (i, j))],
        out_specs=pl.BlockSpec((tm, d), lambda i, j: (i, 0), pipeline_mode=once),
        out_shape=jax.ShapeDtypeStruct((t, d), F32),
        compiler_params=_params(("arbitrary", "arbitrary"), est),
        name="peer_experts",
    )(u, eu, ev, w)


def _filter_features(seq):
    t = jnp.linspace(0.0, 1.0, seq, dtype=F32)[:, None]
    bands = (FILTER_EMB - 1) // 2
    w = 2.0 * math.pi * jnp.arange(seq, dtype=F32)[:, None] / seq
    f = jnp.linspace(1e-4, bands - 1, bands, dtype=F32)[None, :]
    z = jnp.concatenate([t, jnp.cos(f * w), -jnp.sin(f * w)], axis=-1)
    return jnp.pad(z, ((0, 0), (0, FILTER_PAD - FILTER_EMB)))


@functools.lru_cache(maxsize=None)
def _dft_matrices(seq):
    n2 = 2 * seq
    f = np.arange(seq, dtype=np.int64)[:, None]
    n = np.arange(seq, dtype=np.int64)[None, :]
    ang = ((f * n) % n2).astype(np.float64) * (2.0 * math.pi / n2)
    sin = np.where(f == 0, (1 - 2 * (n % 2)).astype(np.float64), np.sin(ang))
    return np.concatenate([np.cos(ang), sin], axis=0).astype(np.float32)


def kernel(x, c, ctx, c_ctx, w_ada, b_ada, w_in, conv_w, conv_b, filt_w1, filt_b1, filt_w2, filt_b2,
           filt_w3, filt_b3, filt_w4, filt_freq, hy_skip, na_rpb, hy_norm_g, na_norm_g, w_out,
           ln1_g, ln1_b, peer_wq, peer_keys, peer_u, peer_v, ln2_g, ln2_b):
    depth = w_ada.shape[0]
    assert depth == 1, "single-layer block: the context stream is never updated"
    bsz, seq, d = x.shape
    rows = seq // GRID_W
    kh = min(NA_KH, rows)
    assert kh == NA_KH and seq % GRID_W == 0
    alpha = (2.0 * depth) ** 0.25
    d_na = d - D_HY
    off_q = 3 * D_HY
    off_k = off_q + d_na

    n_cond = 8
    cond = jnp.concatenate([c, c_ctx[None], jnp.zeros((n_cond - bsz - 1, d), F32)], axis=0)
    mod = _ada(cond, w_ada[0], b_ada[0]).reshape(n_cond, 6, d)

    w_cols = lambda lo, hi: w_in[0][:, lo:hi].astype(BF16)
    z_hy = _proj(x, mod, lambda b: b, w_cols(0, off_q), F32, 512)
    qkv = _proj(x, mod, lambda b: b, w_cols(off_q, None), BF16, 512)
    kv_ctx = _proj(ctx, mod, lambda b: bsz, w_cols(off_k, None), BF16, 256)

    deltas = jnp.abs(jnp.linspace(math.log(DECAY_TARGET) / SLOW_DECAY_PCT,
                                  math.log(DECAY_TARGET) / FAST_DECAY_PCT, D_HY, dtype=F32))[None, :]
    padw = lambda a: jnp.pad(a, ((0, FILTER_PAD - a.shape[0]), (0, FILTER_PAD - a.shape[1])))
    padv = lambda a: jnp.pad(a, ((0, FILTER_PAD - a.shape[0]),))
    w4 = jnp.pad(filt_w4[0], ((0, FILTER_PAD - filt_w4.shape[1]), (0, 0)))
    freq = jnp.pad(filt_freq[0], ((0, 0), (0, FILTER_PAD - filt_freq.shape[2])))
    ksum, kdiff = _hyena_filters(_filter_features(seq), padw(filt_w1[0]), padv(filt_b1[0]), padw(filt_w2[0]),
                                 padv(filt_b2[0]), padw(filt_w3[0]), padv(filt_b3[0]), w4, freq, deltas, 256)
    ffwd = jnp.asarray(_dft_matrices(seq)).astype(BF16)
    finv = ffwd.T
    kp, kq = _spectrum(ffwd, ksum, kdiff, 256)
    hy_out = _hyena(z_hy, conv_w[0], conv_b[0], ffwd, finv, kp, kq, hy_skip[0], 256)

    tables = _bias_tables(na_rpb[0])
    bands = jnp.stack([jnp.concatenate([tables[:, o + i] for i in range(kh)], axis=-1) for o in range(kh)], axis=1)
    band_of_row = lambda r: jnp.clip(r - kh // 2, 0, rows - kh) - r + (NA_KH - 1)
    na_out = _attention(qkv, kv_ctx, bands, band_of_row, rows, kh)

    x_mid, u_peer = _merge(hy_out, na_out, x, mod, w_out[0].astype(BF16), hy_norm_g[0], na_norm_g[0],
                           ln1_g[0], ln1_b[0], alpha, 256)

    t = bsz * seq
    u2 = u_peer.reshape(t, d)
    idx_t, gate_t = _route(u2, peer_wq[0].astype(BF16), peer_keys[0], 1024)
    w_gate = _gate_matrix(idx_t.reshape(-1, t).T, gate_t.reshape(-1, t).T, 64)
    f = _peer(u2, peer_u[0].astype(BF16), peer_v[0].astype(BF16), w_gate, 2048, 512, 512)
    out = _final_ln(x_mid.reshape(t, d), f, mod, ln2_g[0], ln2_b[0], alpha, seq, 512)
    return out.reshape(bsz, seq, d)
```

```python
import functools
import math

import numpy as np
import jax
import jax.numpy as jnp
from jax import lax
from jax.experimental import pallas as pl
from jax.experimental.pallas import tpu as pltpu

F32 = jnp.float32
BF16 = jnp.bfloat16
HIGHEST = lax.Precision.HIGHEST

GRID_W = 64
HEAD_DIM = 64
D_HY = 1024
HY_ORDER = 2
FILTER_EMB = 33
FILTER_PAD = 128
DECAY_TARGET = 1e-2
FAST_DECAY_PCT = 0.3
SLOW_DECAY_PCT = 1.5
NA_KH = 8
NA_KW = 16
PEER_HEADS = 8
PEER_NKEYS = 128
NKEYS_SHIFT = PEER_NKEYS.bit_length() - 1
assert 1 << NKEYS_SHIFT == PEER_NKEYS
PEER_TOPK = 16
LN_EPS = 1e-5
NEG_INF = -1e30

V7X_VMEM_BYTES = 64 * 1024 * 1024
V7X_LANES = 128
V7X_MXU_DEPTH = 256
NA_GROUP = V7X_MXU_DEPTH // HEAD_DIM


def _vmem_limit(estimate_bytes):
    return int(min(estimate_bytes + 12 * 1024 * 1024, V7X_VMEM_BYTES - 6 * 1024 * 1024))


def _params(semantics, estimate_bytes):
    return pltpu.CompilerParams(dimension_semantics=semantics, vmem_limit_bytes=_vmem_limit(estimate_bytes))


def _ada_kernel(cond_ref, w_ref, b_ref, o_ref):
    s = jax.nn.silu(cond_ref[...])
    o_ref[...] = jnp.dot(s, w_ref[...], precision=HIGHEST, preferred_element_type=F32) + b_ref[...]


def _ada(cond, w, b):
    n, d = cond.shape
    tn = 1024
    return pl.pallas_call(
        _ada_kernel,
        grid=(w.shape[1] // tn,),
        in_specs=[pl.BlockSpec((n, d), lambda j: (0, 0)),
                  pl.BlockSpec((d, tn), lambda j: (0, j)),
                  pl.BlockSpec((1, tn), lambda j: (0, j))],
        out_specs=pl.BlockSpec((n, tn), lambda j: (0, j)),
        out_shape=jax.ShapeDtypeStruct((n, w.shape[1]), F32),
        compiler_params=_params(("arbitrary",), 2 * d * tn * 4),
        name="ada",
    )(cond, w, b.reshape(1, -1))


def _proj_kernel(x_ref, mod_ref, w_ref, o_ref):
    u = x_ref[...] * (1.0 + mod_ref[1:2, :]) + mod_ref[0:1, :]
    o_ref[...] = jnp.dot(u.astype(BF16), w_ref[...], preferred_element_type=F32).astype(o_ref.dtype)


def _proj(x, mod, mod_row, w, out_dtype, tm):
    bsz, seq, d = x.shape
    n = w.shape[1]
    est = d * n * 2 + 2 * tm * d * 4 + 2 * tm * n * 4 + tm * n * 4
    return pl.pallas_call(
        _proj_kernel,
        grid=(bsz, seq // tm),
        in_specs=[pl.BlockSpec((None, tm, d), lambda b, i: (b, i, 0)),
                  pl.BlockSpec((None, 6, d), lambda b, i: (mod_row(b), 0, 0)),
                  pl.BlockSpec((d, n), lambda b, i: (0, 0))],
        out_specs=pl.BlockSpec((None, tm, n), lambda b, i: (b, i, 0)),
        out_shape=jax.ShapeDtypeStruct((bsz, seq, n), out_dtype),
        compiler_params=_params(("arbitrary", "arbitrary"), est),
        name="proj",
    )(x, mod, w)


def _filter_kernel(z_ref, w1_ref, b1_ref, w2_ref, b2_ref, w3_ref, b3_ref, fr_ref, w4f_ref, w4b_ref, dl_ref,
                   ks_ref, kd_ref, h_ref):
    dot = functools.partial(jnp.dot, precision=HIGHEST, preferred_element_type=F32)

    @pl.when((pl.program_id(0) == 0) & (pl.program_id(1) == 0))
    def _():
        h = jnp.sin(fr_ref[0:1, :] * (dot(z_ref[...], w1_ref[...]) + b1_ref[...]))
        h = jnp.sin(fr_ref[1:2, :] * (dot(h, w2_ref[...]) + b2_ref[...]))
        h_ref[...] = jnp.sin(fr_ref[2:3, :] * (dot(h, w3_ref[...]) + b3_ref[...]))

    h = h_ref[...]
    decay = jnp.exp(-z_ref[:, 0:1] * dl_ref[...])
    hf = dot(h, w4f_ref[...]) * decay
    hb = dot(h, w4b_ref[...]) * decay
    norm = jnp.sum(jnp.abs(hf) + jnp.abs(hb), axis=0, keepdims=True)
    hf = hf / norm
    hb = hb / norm
    row = lax.broadcasted_iota(jnp.int32, hb.shape, 0)
    hb = jnp.where(row == 0, 0.0, hb)
    ks_ref[...] = hf + hb
    kd_ref[...] = hb - hf


def _hyena_filters(zfeat, w1, b1, w2, b2, w3, b3, w4, freq, deltas, tc):
    seq = zfeat.shape[0]
    fo = w2.shape[0]
    nct = D_HY // tc
    full = lambda a: pl.BlockSpec(a.shape, lambda o, j: (0,) * a.ndim)
    args = (zfeat, w1, b1.reshape(1, -1), w2, b2.reshape(1, -1), w3, b3.reshape(1, -1), freq)
    out_spec = pl.BlockSpec((None, seq, tc), lambda o, j: (o, 0, j))
    return pl.pallas_call(
        _filter_kernel,
        grid=(HY_ORDER, nct),
        in_specs=[full(a) for a in args] + [
            pl.BlockSpec((fo, tc), lambda o, j: (0, (2 * o) * nct + j)),
            pl.BlockSpec((fo, tc), lambda o, j: (0, (2 * o + 1) * nct + j)),
            pl.BlockSpec((1, tc), lambda o, j: (0, j))],
        out_specs=[out_spec, out_spec],
        out_shape=[jax.ShapeDtypeStruct((HY_ORDER, seq, D_HY), F32)] * 2,
        scratch_shapes=[pltpu.VMEM((seq, fo), F32)],
        compiler_params=_params(("arbitrary", "arbitrary"), 12 * seq * tc * 4),
        name="hyena_filter",
    )(*args, w4, w4, deltas)


def _spectrum_kernel(ff_ref, ks_ref, kd_ref, kp_ref, kq_ref):
    seq = ks_ref.shape[0]
    ks = ks_ref[...].astype(BF16)
    kd = kd_ref[...].astype(BF16)
    p = jnp.dot(ff_ref[0:seq, :], ks, preferred_element_type=F32)
    q = jnp.dot(ff_ref[seq:2 * seq, :], kd, preferred_element_type=F32)
    nyq = jnp.dot(ff_ref[seq:seq + 16, :], ks, preferred_element_type=F32)[0:1, :]
    row = lax.broadcasted_iota(jnp.int32, p.shape, 0)
    wf = jnp.where(row == 0, 0.5 / seq, 1.0 / seq)
    kp_ref[...] = p * wf
    kq_ref[...] = jnp.where(row == 0, nyq, q) * wf


def _spectrum(ffwd, ksum, kdiff, tc):
    seq = ksum.shape[1]
    spec = pl.BlockSpec((None, seq, tc), lambda o, j: (o, 0, j))
    return pl.pallas_call(
        _spectrum_kernel,
        grid=(HY_ORDER, D_HY // tc),
        in_specs=[pl.BlockSpec(ffwd.shape, lambda o, j: (0, 0)), spec, spec],
        out_specs=[spec, spec],
        out_shape=[jax.ShapeDtypeStruct(ksum.shape, F32)] * 2,
        compiler_params=_params(("arbitrary", "arbitrary"), ffwd.size * 2 + 12 * seq * tc * 4),
        name="hyena_spectrum",
    )(ffwd, ksum, kdiff)


def _short_conv(z, w, b):
    seq = z.shape[0]
    row = lax.broadcasted_iota(jnp.int32, z.shape, 0)
    prev = jnp.where(row == 0, 0.0, pltpu.roll(z, 1, 0))
    nxt = jnp.where(row == seq - 1, 0.0, pltpu.roll(z, seq - 1, 0))
    return prev * w[0:1, :] + z * w[1:2, :] + nxt * w[2:3, :] + b


def _dft_mul(u, ff_ref, kp_ref, kq_ref, o_ref):
    seq = u.shape[0]
    a = jnp.dot(ff_ref[...], u.astype(BF16), preferred_element_type=F32)
    p, q = a[0:seq], a[seq:2 * seq]
    kp, kq = kp_ref[...], kq_ref[...]
    row0 = lax.broadcasted_iota(jnp.int32, p.shape, 0) == 0
    o_ref[0:seq, :] = jnp.where(row0, kp * p, kp * p + kq * q).astype(o_ref.dtype)
    o_ref[seq:2 * seq, :] = jnp.where(row0, kq * q, kp * q - kq * p).astype(o_ref.dtype)


def _fwd_conv_kernel(z_ref, cw_ref, cb_ref, ff_ref, kp_ref, kq_ref, o_ref):
    _dft_mul(_short_conv(z_ref[...], cw_ref[...], cb_ref[...]), ff_ref, kp_ref, kq_ref, o_ref)


def _fwd_plain_kernel(u_ref, ff_ref, kp_ref, kq_ref, o_ref):
    _dft_mul(u_ref[...], ff_ref, kp_ref, kq_ref, o_ref)


def _inv_first_kernel(y_ref, fi_ref, zv_ref, zx_ref, cwv_ref, cbv_ref, cwx_ref, cbx_ref, skip_ref, o_ref):
    v = _short_conv(zv_ref[...], cwv_ref[...], cbv_ref[...])
    x1 = _short_conv(zx_ref[...], cwx_ref[...], cbx_ref[...])
    y = jnp.dot(fi_ref[...], y_ref[...], preferred_element_type=F32)
    o_ref[...] = x1 * (y + v * skip_ref[...])


def _inv_second_kernel(y_ref, fi_ref, g_ref, zx_ref, cwx_ref, cbx_ref, skip_ref, o_ref):
    x2 = _short_conv(zx_ref[...], cwx_ref[...], cbx_ref[...])
    y = jnp.dot(fi_ref[...], y_ref[...], preferred_element_type=F32)
    o_ref[...] = x2 * (y + g_ref[...] * skip_ref[...])


def _hyena(z, conv_w, conv_b, ffwd, finv, kp, kq, skip, tc):
    bsz, seq, _ = z.shape
    nct = D_HY // tc
    sem = ("arbitrary", "arbitrary")
    zcol = lambda g: pl.BlockSpec((None, seq, tc), lambda b, j: (b, 0, g * nct + j))
    wcol = lambda g: pl.BlockSpec((3, tc), lambda b, j: (0, g * nct + j))
    bcol = lambda g: pl.BlockSpec((1, tc), lambda b, j: (0, g * nct + j))
    act = pl.BlockSpec((None, seq, tc), lambda b, j: (b, 0, j))
    spec2 = pl.BlockSpec((None, 2 * seq, tc), lambda b, j: (b, 0, j))
    kspec = lambda o: pl.BlockSpec((None, seq, tc), lambda b, j: (o, 0, j))
    sspec = lambda o: pl.BlockSpec((None, 1, tc), lambda b, j: (o, 0, j))
    mat = pl.BlockSpec(ffwd.shape, lambda b, j: (0, 0))
    mati = pl.BlockSpec(finv.shape, lambda b, j: (0, 0))
    est = ffwd.size * 2 + 14 * seq * tc * 4
    cb = conv_b.reshape(1, -1)
    skip3 = skip.reshape(HY_ORDER, 1, D_HY)
    spec_shape = jax.ShapeDtypeStruct((bsz, 2 * seq, D_HY), BF16)
    act_shape = jax.ShapeDtypeStruct((bsz, seq, D_HY), F32)

    y1 = pl.pallas_call(
        _fwd_conv_kernel, grid=(bsz, nct),
        in_specs=[zcol(0), wcol(0), bcol(0), mat, kspec(0), kspec(0)],
        out_specs=spec2, out_shape=spec_shape, compiler_params=_params(sem, est), name="hyena_fwd1",
    )(z, conv_w, cb, ffwd, kp, kq)
    g = pl.pallas_call(
        _inv_first_kernel, grid=(bsz, nct),
        in_specs=[spec2, mati, zcol(0), zcol(1), wcol(0), bcol(0), wcol(1), bcol(1), sspec(0)],
        out_specs=act, out_shape=act_shape, compiler_params=_params(sem, est), name="hyena_inv1",
    )(y1, finv, z, z, conv_w, cb, conv_w, cb, skip3)
    y2 = pl.pallas_call(
        _fwd_plain_kernel, grid=(bsz, nct),
        in_specs=[act, mat, kspec(1), kspec(1)],
        out_specs=spec2, out_shape=spec_shape, compiler_params=_params(sem, est), name="hyena_fwd2",
    )(g, ffwd, kp, kq)
    return pl.pallas_call(
        _inv_second_kernel, grid=(bsz, nct),
        in_specs=[spec2, mati, act, zcol(2), wcol(2), bcol(2), sspec(1)],
        out_specs=act, out_shape=act_shape, compiler_params=_params(sem, est), name="hyena_inv2",
    )(y2, finv, g, z, conv_w, cb, skip3)


def _bias_kernel(rpb_ref, o_ref):
    h = pl.program_id(0)
    n_r, n_c = 2 * NA_KH - 1, 2 * NA_KW - 1
    cq = lax.broadcasted_iota(jnp.int32, (GRID_W, GRID_W), 0)
    ck = lax.broadcasted_iota(jnp.int32, (GRID_W, GRID_W), 1)
    rel = ck - jnp.clip(cq - NA_KW // 2, 0, GRID_W - NA_KW)
    ok = (rel >= 0) & (rel < NA_KW)
    dcol = jnp.clip(ck - cq + NA_KW - 1, 0, n_c - 1)
    for r in range(n_r):
        acc = jnp.zeros((GRID_W, GRID_W), F32)
        for c in range(n_c):
            acc = jnp.where(dcol == c, rpb_ref[(h * n_r + r) * n_c + c], acc)
        o_ref[r] = jnp.where(ok, acc, NEG_INF)


def _bias_tables(rpb):
    heads = rpb.shape[0]
    n_r = 2 * NA_KH - 1
    return pl.pallas_call(
        _bias_kernel,
        grid=(heads,),
        in_specs=[pl.BlockSpec(memory_space=pltpu.SMEM)],
        out_specs=pl.BlockSpec((None, n_r, GRID_W, GRID_W), lambda h: (h, 0, 0, 0)),
        out_shape=jax.ShapeDtypeStruct((heads, n_r, GRID_W, GRID_W), F32),
        compiler_params=_params(("arbitrary",), 1 << 20),
        name="na_bias",
    )(rpb.reshape(-1))


def _attn_kernel(q_ref, k_ref, v_ref, kc_ref, vc_ref, bias0_ref, bias1_ref, o_ref, *, rows, kh):
    r2 = pl.program_id(1)
    nk = kh * GRID_W
    gw = NA_GROUP * HEAD_DIM
    lane = lax.broadcasted_iota(jnp.int32, (1, gw), 1)
    head_of_lane = [(lane >= e * HEAD_DIM) & (lane < (e + 1) * HEAD_DIM) for e in range(NA_GROUP)]
    nt = (((1,), (1,)), ((), ()))
    starts = [pl.multiple_of(jnp.clip(2 * r2 + i - kh // 2, 0, rows - kh) * GRID_W, GRID_W) for i in range(2)]
    bias_refs = (bias0_ref, bias1_ref)
    per_row = NA_GROUP * GRID_W
    zero = jnp.zeros((GRID_W, gw), BF16)
    for hg in range(q_ref.shape[1] // gw):
        cols = slice(hg * gw, (hg + 1) * gw)
        qp = q_ref[:, cols] * jnp.asarray(HEAD_DIM ** -0.5, BF16)
        qm = jnp.concatenate([jnp.where(head_of_lane[e], qp[i * GRID_W:(i + 1) * GRID_W], zero)
                              for i in range(2) for e in range(NA_GROUP)], axis=0)
        s_ctx = lax.dot_general(qm, kc_ref[:, cols], nt, preferred_element_type=F32)
        m_ctx = jnp.max(s_ctx, axis=-1, keepdims=True)
        s_lat, m_all = [], []
        for i in range(2):
            blk = slice(i * per_row, (i + 1) * per_row)
            bias = jnp.concatenate([bias_refs[i][hg * NA_GROUP + e] for e in range(NA_GROUP)], axis=0)
            s = lax.dot_general(qm[blk], k_ref[pl.ds(starts[i], nk), cols], nt, preferred_element_type=F32) + bias
            s_lat.append(s)
            m_all.append(jnp.maximum(jnp.max(s, axis=-1, keepdims=True), m_ctx[blk]))
        p_ctx = jnp.exp(s_ctx - jnp.concatenate(m_all, axis=0))
        o_ctx = jnp.dot(p_ctx.astype(BF16), vc_ref[:, cols], preferred_element_type=F32)
        l_ctx = jnp.sum(p_ctx, axis=-1, keepdims=True)
        for i in range(2):
            blk = slice(i * per_row, (i + 1) * per_row)
            p = jnp.exp(s_lat[i] - m_all[i])
            o = jnp.dot(p.astype(BF16), v_ref[pl.ds(starts[i], nk), cols], preferred_element_type=F32) + o_ctx[blk]
            o = o / (jnp.sum(p, axis=-1, keepdims=True) + l_ctx[blk])
            out = o[0:GRID_W]
            for e in range(1, NA_GROUP):
                out = jnp.where(head_of_lane[e], o[e * GRID_W:(e + 1) * GRID_W], out)
            o_ref[i * GRID_W:(i + 1) * GRID_W, cols] = out


def _attention(qkv, kv_ctx, bands, band_of_row, rows, kh):
    bsz, seq, d3 = qkv.shape
    dn = d3 // 3
    ctx_len = kv_ctx.shape[1]
    heads, _, _, nk = bands.shape
    assert rows % 2 == 0
    est = 2 * (2 * seq * dn * 2 + 2 * ctx_len * dn * 2 + 2 * heads * GRID_W * nk * 4 + 2 * GRID_W * dn * 6)
    band = lambda i: pl.BlockSpec((heads, None, GRID_W, nk), lambda b, r: (0, band_of_row(2 * r + i), 0, 0))
    return pl.pallas_call(
        functools.partial(_attn_kernel, rows=rows, kh=kh),
        grid=(bsz, rows // 2),
        in_specs=[pl.BlockSpec((None, 2 * GRID_W, dn), lambda b, r: (b, r, 0)),
                  pl.BlockSpec((None, seq, dn), lambda b, r: (b, 0, 1)),
                  pl.BlockSpec((None, seq, dn), lambda b, r: (b, 0, 2)),
                  pl.BlockSpec((None, ctx_len, dn), lambda b, r: (b, 0, 0)),
                  pl.BlockSpec((None, ctx_len, dn), lambda b, r: (b, 0, 1)),
                  band(0), band(1)],
        out_specs=pl.BlockSpec((None, 2 * GRID_W, dn), lambda b, r: (b, r, 0)),
        out_shape=jax.ShapeDtypeStruct((bsz, seq, dn), F32),
        compiler_params=_params(("arbitrary", "arbitrary"), est),
        name="na_attention",
    )(qkv, qkv, qkv, kv_ctx, kv_ctx, bands, bands)


def _rms(x, g):
    return x * lax.rsqrt(jnp.mean(x * x, axis=-1, keepdims=True) + LN_EPS) * g


def _layer_norm(x, g, b):
    xc = x - jnp.mean(x, axis=-1, keepdims=True)
    var = jnp.mean(xc * xc, axis=-1, keepdims=True)
    return xc * lax.rsqrt(var + LN_EPS) * g + b


def _merge_kernel(hy_ref, na_ref, x_ref, mod_ref, w_ref, ghy_ref, gna_ref, lg_ref, lb_ref, xo_ref, uo_ref, *, alpha):
    dh = hy_ref.shape[1]
    hy = _rms(hy_ref[...], ghy_ref[...]).astype(BF16)
    na = _rms(na_ref[...], gna_ref[...]).astype(BF16)
    y = (jnp.dot(hy, w_ref[0:dh, :], preferred_element_type=F32)
         + jnp.dot(na, w_ref[dh:, :], preferred_element_type=F32))
    xn = _layer_norm(alpha * x_ref[...] + mod_ref[2:3, :] * y, lg_ref[...], lb_ref[...])
    xo_ref[...] = xn
    uo_ref[...] = (xn * (1.0 + mod_ref[4:5, :]) + mod_ref[3:4, :]).astype(uo_ref.dtype)


def _merge(hy, na, x, mod, w_out, g_hy, g_na, ln_g, ln_b, alpha, tm):
    bsz, seq, d = x.shape
    dh, dn = hy.shape[2], na.shape[2]
    row = lambda a: a.reshape(1, -1)
    vec = lambda n: pl.BlockSpec((1, n), lambda b, i: (0, 0))
    tile = lambda n: pl.BlockSpec((None, tm, n), lambda b, i: (b, i, 0))
    est = d * d * 2 + 2 * tm * (dh + dn + 2 * d) * 4 + 2 * tm * d * 6 + 4 * tm * d * 4
    return pl.pallas_call(
        functools.partial(_merge_kernel, alpha=alpha),
        grid=(bsz, seq // tm),
        in_specs=[tile(dh), tile(dn), tile(d),
                  pl.BlockSpec((None, 6, d), lambda b, i: (b, 0, 0)),
                  pl.BlockSpec((d, d), lambda b, i: (0, 0)),
                  vec(dh), vec(dn), vec(d), vec(d)],
        out_specs=[tile(d), tile(d)],
        out_shape=[jax.ShapeDtypeStruct((bsz, seq, d), F32), jax.ShapeDtypeStruct((bsz, seq, d), BF16)],
        compiler_params=_params(("arbitrary", "arbitrary"), est),
        name="merge",
    )(hy, na, x, mod, w_out, row(g_hy), row(g_na), row(ln_g), row(ln_b))


def _first(a, b):
    return (a[0] > b[0]) | ((a[0] == b[0]) & (a[1] < b[1]))


def _order_pair(xs, i, j):
    a, b = xs[i], xs[j]
    f = _first(a, b)
    xs[i] = (jnp.maximum(a[0], b[0]),) + tuple(jnp.where(f, p, r) for p, r in zip(a[1:], b[1:]))
    xs[j] = (jnp.minimum(a[0], b[0]),) + tuple(jnp.where(f, r, p) for p, r in zip(a[1:], b[1:]))


def _bitonic_merge(xs):
    n = len(xs)
    j = n // 2
    while j >= 1:
        for i in range(n):
            if i & j == 0:
                _order_pair(xs, i, i | j)
        j //= 2
    return xs


def _sort(xs):
    n = len(xs)
    p = 1
    while p < n:
        k = p
        while k >= 1:
            for j in range(k % p, n - k, 2 * k):
                for i in range(min(k, n - j - k)):
                    if (i + j) // (2 * p) == (i + j + k) // (2 * p):
                        _order_pair(xs, i + j, i + j + k)
            k //= 2
        p *= 2
    return xs


def _leading_half(a, b):
    out = []
    for x, y in zip(a, reversed(b)):
        f = _first(x, y)
        out.append((jnp.maximum(x[0], y[0]),) + tuple(jnp.where(f, p, r) for p, r in zip(x[1:], y[1:])))
    return out


def _sorted_top(items, k):
    runs = [_sort(items[i:i + k]) for i in range(0, len(items), k)]
    while len(runs) > 1:
        runs = [_bitonic_merge(_leading_half(runs[i], runs[i + 1])) for i in range(0, len(runs), 2)]
    return runs[0]


def _pitch(rows):
    return rows + 8


def _route_kernel(u_ref, wq_ref, keys_ref, eu_ref, ev_ref, idx_ref, gate_ref, eub_ref, evb_ref, s_ref):
    eub_ref[...] = eu_ref[...].astype(eub_ref.dtype)
    evb_ref[...] = ev_ref[...].astype(evb_ref.dtype)
    nkeys, half = keys_ref.shape[1], keys_ref.shape[2]
    k = PEER_TOPK
    groups = u_ref.shape[0] // V7X_LANES
    pitch = _pitch(nkeys)
    tile = (groups, V7X_LANES)
    q = jnp.dot(u_ref[...], wq_ref[...], preferred_element_type=F32)
    nt = (((1,), (1,)), ((), ()))
    def split(v):
        hi = v.astype(BF16)
        return hi, (v - hi.astype(F32)).astype(BF16)

    for p in range(2):
        k_hi, k_lo = split(keys_ref[p])
        q_hi, q_lo = split(q[:, p * half:(p + 1) * half])
        dot = lambda a, b: lax.dot_general(a, b, nt, preferred_element_type=F32)
        s = dot(k_hi, q_hi) + (dot(k_hi, q_lo) + dot(k_lo, q_hi))
        for g in range(groups):
            s_ref[p, g * pitch:g * pitch + nkeys, :] = s[:, g * V7X_LANES:(g + 1) * V7X_LANES]

    def scores(p):
        return [(s_ref[p, pl.ds(n, groups, stride=pitch), :], jnp.full(tile, float(n), F32)) for n in range(nkeys)]

    top_a = _sorted_top(scores(0), k)
    top_b = _sorted_top(scores(1), k)

    def cand(i, j):
        return (top_a[i][0] + top_b[j][0], jnp.full(tile, float(i * k + j), F32),
                top_a[i][1] * float(nkeys) + top_b[j][1])

    pad = (jnp.full(tile, -jnp.inf, F32), jnp.full(tile, float(k * k), F32), jnp.zeros(tile, F32))
    run = lambda i: [cand(i, j) for j in range(k // (i + 1))]
    tail = [cand(i, 0) for i in range(k // 2, k)]
    assert k == 16
    r0 = run(0)
    r1 = _bitonic_merge(run(1) + tail[::-1])
    r2 = _sort(run(2) + run(3) + run(4) + run(5) + run(6))
    r3 = run(7) + [pad] * (k - 2)
    best = _leading_half(_bitonic_merge(_leading_half(r0, r1)), _bitonic_merge(_leading_half(r2, r3)))

    m = functools.reduce(jnp.maximum, [c[0] for c in best])
    e = [jnp.exp(c[0] - m) for c in best]
    inv = 1.0 / functools.reduce(jnp.add, e)
    for n in range(k):
        idx_ref[n] = best[n][2].astype(jnp.int32)
        gate_ref[n] = e[n] * inv


def _route(u, wq, keys, eu, ev, tm):
    t, d = u.shape
    heads, _, nkeys, half = keys.shape
    groups = tm // V7X_LANES
    n_steps = (t // tm) * heads
    n_exp, de = eu.shape
    slab = n_exp // n_steps
    assert slab * n_steps == n_exp and slab % 16 == 0
    est = (2 * (tm * d * 2 + d * 2 * half * 2 + 2 * nkeys * half * 4) + 8 * nkeys * tm * 4
           + 2 * 2 * slab * de * 6)
    out_spec = pl.BlockSpec((PEER_TOPK, groups, V7X_LANES), lambda i, h: (h, i, 0))
    slab_spec = pl.BlockSpec((slab, de), lambda i, h: (i * heads + h, 0))
    return pl.pallas_call(
        _route_kernel,
        grid=(t // tm, heads),
        in_specs=[pl.BlockSpec((tm, d), lambda i, h: (i, 0)),
                  pl.BlockSpec((d, 2 * half), lambda i, h: (0, h)),
                  pl.BlockSpec((None, 2, nkeys, half), lambda i, h: (h, 0, 0, 0)),
                  slab_spec, slab_spec],
        out_specs=[out_spec, out_spec, slab_spec, slab_spec],
        out_shape=[jax.ShapeDtypeStruct((heads * PEER_TOPK, t // V7X_LANES, V7X_LANES), jnp.int32),
                   jax.ShapeDtypeStruct((heads * PEER_TOPK, t // V7X_LANES, V7X_LANES), F32),
                   jax.ShapeDtypeStruct(eu.shape, BF16), jax.ShapeDtypeStruct(ev.shape, BF16)],
        scratch_shapes=[pltpu.VMEM((2, groups * _pitch(nkeys), V7X_LANES), F32)],
        compiler_params=_params(("arbitrary", "arbitrary"), est),
        name="peer_route",
    )(u, wq, keys, eu, ev)


def _gate_matrix_kernel(idx_ref, gate_ref, o_ref, w0_ref, w1_ref):
    step = pl.program_id(0)

    @pl.when(step == 0)
    def _():
        w1_ref[...] = jnp.zeros_like(w1_ref)

    @pl.when(step % 2 == 0)
    def _():
        _gate_matrix_step(idx_ref, gate_ref, o_ref, w0_ref, w1_ref)

    @pl.when(step % 2 == 1)
    def _():
        _gate_matrix_step(idx_ref, gate_ref, o_ref, w1_ref, w0_ref)


def _gate_matrix_step(idx_ref, gate_ref, o_ref, cur_ref, prev_ref):
    tw, nsel = idx_ref.shape
    pitch = _pitch(PEER_NKEYS)
    key = lax.broadcasted_iota(jnp.int32, (PEER_NKEYS, nsel), 0)
    nt = (((1,), (1,)), ((), ()))
    zero = jnp.zeros((PEER_NKEYS, nsel), BF16)
    n_pairs = tw // 2
    planes_per_pair = PEER_NKEYS // n_pairs
    for p in range(n_pairs):
        a_hot, b_hot = [], []
        for t in (2 * p, 2 * p + 1):
            idx = idx_ref[t:t + 1, :]
            gate = gate_ref[t:t + 1, :]
            a_hot.append(jnp.where((idx >> NKEYS_SHIFT) == key, gate, 0.0).astype(BF16))
            b_hot.append(jnp.where((idx & (PEER_NKEYS - 1)) == key, 1.0, 0.0).astype(BF16))
        lhs = jnp.concatenate(a_hot, axis=1)
        rhs = jnp.concatenate([jnp.concatenate([b_hot[0], zero], axis=1),
                               jnp.concatenate([zero, b_hot[1]], axis=1)], axis=0)
        planes = lax.dot_general(lhs, rhs, nt, preferred_element_type=F32)
        for k in range(2):
            row = (2 * p + k) * pitch
            cur_ref[row:row + PEER_NKEYS, :] = planes[:, k * PEER_NKEYS:(k + 1) * PEER_NKEYS]
        for a in range(p * planes_per_pair, (p + 1) * planes_per_pair):
            o_ref[:, a * PEER_NKEYS:(a + 1) * PEER_NKEYS] = (
                prev_ref[pl.ds(a, tw, stride=pitch), :].astype(o_ref.dtype))


def _gate_matrix(idx, gate, tw):
    t, nsel = idx.shape
    n_exp = PEER_NKEYS * PEER_NKEYS
    n_blocks = t // tw
    assert PEER_NKEYS % (tw // 2) == 0
    est = 2 * (2 * tw * nsel * 4 + tw * n_exp * 2) + 2 * tw * _pitch(PEER_NKEYS) * PEER_NKEYS * 4
    block_in = pl.BlockSpec((tw, nsel), lambda i: (jnp.minimum(i, n_blocks - 1), 0))
    return pl.pallas_call(
        _gate_matrix_kernel,
        grid=(n_blocks + 1,),
        in_specs=[block_in, block_in],
        out_specs=pl.BlockSpec((tw, n_exp), lambda i: (jnp.maximum(i - 1, 0), 0)),
        out_shape=jax.ShapeDtypeStruct((t, n_exp), BF16),
        scratch_shapes=[pltpu.VMEM((tw * _pitch(PEER_NKEYS), PEER_NKEYS), F32)] * 2,
        compiler_params=_params(("arbitrary",), est),
        name="peer_gate_matrix",
    )(idx, gate)


def _peer_kernel(u_ref, eu_ref, ev_ref, w_ref, o_ref, *, n_chunk):
    @pl.when(pl.program_id(1) == 0)
    def _():
        o_ref[...] = jnp.zeros_like(o_ref)

    z = lax.dot_general(u_ref[...], eu_ref[...], (((1,), (1,)), ((), ())), preferred_element_type=F32)
    act = 0.5 * z * (1.0 + lax.erf(z * (2.0 ** -0.5)))
    h = (w_ref[...].astype(F32) * act).astype(BF16)
    for c in range(0, o_ref.shape[1], n_chunk):
        o_ref[:, c:c + n_chunk] += jnp.dot(h, ev_ref[:, c:c + n_chunk], preferred_element_type=F32)


def _final_ln_kernel(x_ref, f_ref, mod_ref, lg_ref, lb_ref, o_ref, *, alpha):
    r = alpha * x_ref[...] + mod_ref[5:6, :] * f_ref[...]
    o_ref[...] = _layer_norm(r, lg_ref[...], lb_ref[...])


def _final_ln(x, f, mod, ln_g, ln_b, alpha, seq, tm):
    t, d = x.shape
    tile = pl.BlockSpec((tm, d), lambda i: (i, 0))
    vec = pl.BlockSpec((1, d), lambda i: (0, 0))
    return pl.pallas_call(
        functools.partial(_final_ln_kernel, alpha=alpha),
        grid=(t // tm,),
        in_specs=[tile, tile, pl.BlockSpec((None, 6, d), lambda i: ((i * tm) // seq, 0, 0)), vec, vec],
        out_specs=tile,
        out_shape=jax.ShapeDtypeStruct((t, d), F32),
        compiler_params=_params(("arbitrary",), 8 * tm * d * 4),
        name="final_ln",
    )(x, f, mod, ln_g.reshape(1, -1), ln_b.reshape(1, -1))


def _peer(u, eu, ev, w, tm, te, n_chunk):
    t, d = u.shape
    n_exp = ev.shape[0]
    once = pl.Buffered(1)
    est = tm * d * 2 + tm * d * 4 + 2 * (2 * te * d * 2 + tm * te * 2) + 4 * tm * te * 4 + tm * n_chunk * 4
    return pl.pallas_call(
        functools.partial(_peer_kernel, n_chunk=n_chunk),
        grid=(t // tm, n_exp // te),
        in_specs=[pl.BlockSpec((tm, d), lambda i, j: (i, 0), pipeline_mode=once),
                  pl.BlockSpec((te, d), lambda i, j: (j, 0)),
                  pl.BlockSpec((te, d), lambda i, j: (j, 0)),
                  pl.BlockSpec((tm, te), lambda i, j: (i, j))],
        out_specs=pl.BlockSpec((tm, d), lambda i, j: (i, 0), pipeline_mode=once),
        out_shape=jax.ShapeDtypeStruct((t, d), F32),
        compiler_params=_params(("arbitrary", "arbitrary"), est),
        name="peer_experts",
    )(u, eu, ev, w)


def _filter_features(seq):
    t = jnp.linspace(0.0, 1.0, seq, dtype=F32)[:, None]
    bands = (FILTER_EMB - 1) // 2
    w = 2.0 * math.pi * jnp.arange(seq, dtype=F32)[:, None] / seq
    f = jnp.linspace(1e-4, bands - 1, bands, dtype=F32)[None, :]
    z = jnp.concatenate([t, jnp.cos(f * w), -jnp.sin(f * w)], axis=-1)
    return jnp.pad(z, ((0, 0), (0, FILTER_PAD - FILTER_EMB)))


@functools.lru_cache(maxsize=None)
def _dft_matrices(seq):
    n2 = 2 * seq
    f = np.arange(seq, dtype=np.int64)[:, None]
    n = np.arange(seq, dtype=np.int64)[None, :]
    ang = ((f * n) % n2).astype(np.float64) * (2.0 * math.pi / n2)
    sin = np.where(f == 0, (1 - 2 * (n % 2)).astype(np.float64), np.sin(ang))
    return np.concatenate([np.cos(ang), sin], axis=0).astype(np.float32)


def kernel(x, c, ctx, c_ctx, w_ada, b_ada, w_in, conv_w, conv_b, filt_w1, filt_b1, filt_w2, filt_b2,
           filt_w3, filt_b3, filt_w4, filt_freq, hy_skip, na_rpb, hy_norm_g, na_norm_g, w_out,
           ln1_g, ln1_b, peer_wq, peer_keys, peer_u, peer_v, ln2_g, ln2_b):
    depth = w_ada.shape[0]
    assert depth == 1, "single-layer block: the context stream is never updated"
    bsz, seq, d = x.shape
    rows = seq // GRID_W
    kh = min(NA_KH, rows)
    assert kh == NA_KH and seq % GRID_W == 0
    alpha = (2.0 * depth) ** 0.25
    d_na = d - D_HY
    off_q = 3 * D_HY
    off_k = off_q + d_na

    n_cond = 8
    cond = jnp.concatenate([c, c_ctx[None], jnp.zeros((n_cond - bsz - 1, d), F32)], axis=0)
    mod = _ada(cond, w_ada[0], b_ada[0]).reshape(n_cond, 6, d)

    w_cols = lambda lo, hi: w_in[0][:, lo:hi].astype(BF16)
    z_hy = _proj(x, mod, lambda b: b, w_cols(0, off_q), F32, 512)
    qkv = _proj(x, mod, lambda b: b, w_cols(off_q, None), BF16, 512)
    kv_ctx = _proj(ctx, mod, lambda b: bsz, w_cols(off_k, None), BF16, 256)

    deltas = jnp.abs(jnp.linspace(math.log(DECAY_TARGET) / SLOW_DECAY_PCT,
                                  math.log(DECAY_TARGET) / FAST_DECAY_PCT, D_HY, dtype=F32))[None, :]
    padw = lambda a: jnp.pad(a, ((0, FILTER_PAD - a.shape[0]), (0, FILTER_PAD - a.shape[1])))
    padv = lambda a: jnp.pad(a, ((0, FILTER_PAD - a.shape[0]),))
    w4 = jnp.pad(filt_w4[0], ((0, FILTER_PAD - filt_w4.shape[1]), (0, 0)))
    freq = jnp.pad(filt_freq[0], ((0, 0), (0, FILTER_PAD - filt_freq.shape[2])))
    ksum, kdiff = _hyena_filters(_filter_features(seq), padw(filt_w1[0]), padv(filt_b1[0]), padw(filt_w2[0]),
                                 padv(filt_b2[0]), padw(filt_w3[0]), padv(filt_b3[0]), w4, freq, deltas, 256)
    ffwd = jnp.asarray(_dft_matrices(seq)).astype(BF16)
    finv = ffwd.T
    kp, kq = _spectrum(ffwd, ksum, kdiff, 256)
    hy_out = _hyena(z_hy, conv_w[0], conv_b[0], ffwd, finv, kp, kq, hy_skip[0], 256)

    tables = _bias_tables(na_rpb[0])
    bands = jnp.stack([jnp.concatenate([tables[:, o + i] for i in range(kh)], axis=-1) for o in range(kh)], axis=1)
    band_of_row = lambda r: jnp.clip(r - kh // 2, 0, rows - kh) - r + (NA_KH - 1)
    na_out = _attention(qkv, kv_ctx, bands, band_of_row, rows, kh)

    x_mid, u_peer = _merge(hy_out, na_out, x, mod, w_out[0].astype(BF16), hy_norm_g[0], na_norm_g[0],
                           ln1_g[0], ln1_b[0], alpha, 256)

    t = bsz * seq
    u2 = u_peer.reshape(t, d)
    idx_t, gate_t, eu_b, ev_b = _route(u2, peer_wq[0].astype(BF16), peer_keys[0], peer_u[0], peer_v[0], 1024)
    w_gate = _gate_matrix(idx_t.reshape(-1, t).T, gate_t.reshape(-1, t).T, 64)
    f = _peer(u2, eu_b, ev_b, w_gate, 2048, 512, 512)
    out = _final_ln(x_mid.reshape(t, d), f, mod, ln2_g[0], ln2_b[0], alpha, seq, 512)
    return out.reshape(bsz, seq, d)
```

```python
import functools
import math

import numpy as np
import jax
import jax.numpy as jnp
from jax import lax
from jax.experimental import pallas as pl
from jax.experimental.pallas import tpu as pltpu

F32 = jnp.float32
BF16 = jnp.bfloat16
HIGHEST = lax.Precision.HIGHEST

GRID_W = 64
HEAD_DIM = 64
D_HY = 1024
HY_ORDER = 2
FILTER_EMB = 33
FILTER_PAD = 128
DECAY_TARGET = 1e-2
FAST_DECAY_PCT = 0.3
SLOW_DECAY_PCT = 1.5
NA_KH = 8
NA_KW = 16
PEER_HEADS = 8
PEER_NKEYS = 128
NKEYS_SHIFT = PEER_NKEYS.bit_length() - 1
assert 1 << NKEYS_SHIFT == PEER_NKEYS
PEER_TOPK = 16
LN_EPS = 1e-5
NEG_INF = -1e30

V7X_VMEM_BYTES = 64 * 1024 * 1024
V7X_LANES = 128
V7X_MXU_DEPTH = 256
NA_GROUP = V7X_MXU_DEPTH // HEAD_DIM


def _vmem_limit(estimate_bytes):
    return int(min(estimate_bytes + 12 * 1024 * 1024, V7X_VMEM_BYTES - 6 * 1024 * 1024))


def _params(semantics, estimate_bytes):
    return pltpu.CompilerParams(dimension_semantics=semantics, vmem_limit_bytes=_vmem_limit(estimate_bytes))


def _ada_kernel(cond_ref, w_ref, b_ref, o_ref):
    s = jax.nn.silu(cond_ref[...])
    o_ref[...] = jnp.dot(s, w_ref[...], precision=HIGHEST, preferred_element_type=F32) + b_ref[...]


def _ada(cond, w, b):
    n, d = cond.shape
    tn = 2048
    return pl.pallas_call(
        _ada_kernel,
        grid=(w.shape[1] // tn,),
        in_specs=[pl.BlockSpec((n, d), lambda j: (0, 0)),
                  pl.BlockSpec((d, tn), lambda j: (0, j)),
                  pl.BlockSpec((1, tn), lambda j: (0, j))],
        out_specs=pl.BlockSpec((n, tn), lambda j: (0, j)),
        out_shape=jax.ShapeDtypeStruct((n, w.shape[1]), F32),
        compiler_params=_params(("arbitrary",), 2 * d * tn * 4),
        name="ada",
    )(cond, w, b.reshape(1, -1))


def _proj_kernel(x_ref, mod_ref, w_ref, o_ref):
    u = x_ref[...] * (1.0 + mod_ref[1:2, :]) + mod_ref[0:1, :]
    o_ref[...] = jnp.dot(u.astype(BF16), w_ref[...], preferred_element_type=F32).astype(o_ref.dtype)


def _proj(x, mod, mod_row, w, out_dtype, tm):
    bsz, seq, d = x.shape
    n = w.shape[1]
    est = d * n * 2 + 2 * tm * d * 4 + 2 * tm * n * 4 + tm * n * 4
    return pl.pallas_call(
        _proj_kernel,
        grid=(bsz, seq // tm),
        in_specs=[pl.BlockSpec((None, tm, d), lambda b, i: (b, i, 0)),
                  pl.BlockSpec((None, 6, d), lambda b, i: (mod_row(b), 0, 0)),
                  pl.BlockSpec((d, n), lambda b, i: (0, 0))],
        out_specs=pl.BlockSpec((None, tm, n), lambda b, i: (b, i, 0)),
        out_shape=jax.ShapeDtypeStruct((bsz, seq, n), out_dtype),
        compiler_params=_params(("arbitrary", "arbitrary"), est),
        name="proj",
    )(x, mod, w)


def _filter_kernel(z_ref, w1_ref, b1_ref, w2_ref, b2_ref, w3_ref, b3_ref, fr_ref, w4f_ref, w4b_ref, dl_ref,
                   ks_ref, kd_ref, h_ref):
    dot = functools.partial(jnp.dot, precision=HIGHEST, preferred_element_type=F32)

    @pl.when((pl.program_id(0) == 0) & (pl.program_id(1) == 0))
    def _():
        h = jnp.sin(fr_ref[0:1, :] * (dot(z_ref[...], w1_ref[...]) + b1_ref[...]))
        h = jnp.sin(fr_ref[1:2, :] * (dot(h, w2_ref[...]) + b2_ref[...]))
        h_ref[...] = jnp.sin(fr_ref[2:3, :] * (dot(h, w3_ref[...]) + b3_ref[...]))

    h = h_ref[...]
    decay = jnp.exp(-z_ref[:, 0:1] * dl_ref[...])
    hf = dot(h, w4f_ref[...]) * decay
    hb = dot(h, w4b_ref[...]) * decay
    norm = jnp.sum(jnp.abs(hf) + jnp.abs(hb), axis=0, keepdims=True)
    hf = hf / norm
    hb = hb / norm
    row = lax.broadcasted_iota(jnp.int32, hb.shape, 0)
    hb = jnp.where(row == 0, 0.0, hb)
    ks_ref[...] = hf + hb
    kd_ref[...] = hb - hf


def _hyena_filters(zfeat, w1, b1, w2, b2, w3, b3, w4, freq, deltas, tc):
    seq = zfeat.shape[0]
    fo = w2.shape[0]
    nct = D_HY // tc
    full = lambda a: pl.BlockSpec(a.shape, lambda o, j: (0,) * a.ndim)
    args = (zfeat, w1, b1.reshape(1, -1), w2, b2.reshape(1, -1), w3, b3.reshape(1, -1), freq)
    out_spec = pl.BlockSpec((None, seq, tc), lambda o, j: (o, 0, j))
    return pl.pallas_call(
        _filter_kernel,
        grid=(HY_ORDER, nct),
        in_specs=[full(a) for a in args] + [
            pl.BlockSpec((fo, tc), lambda o, j: (0, (2 * o) * nct + j)),
            pl.BlockSpec((fo, tc), lambda o, j: (0, (2 * o + 1) * nct + j)),
            pl.BlockSpec((1, tc), lambda o, j: (0, j))],
        out_specs=[out_spec, out_spec],
        out_shape=[jax.ShapeDtypeStruct((HY_ORDER, seq, D_HY), F32)] * 2,
        scratch_shapes=[pltpu.VMEM((seq, fo), F32)],
        compiler_params=_params(("arbitrary", "arbitrary"), 12 * seq * tc * 4),
        name="hyena_filter",
    )(*args, w4, w4, deltas)


def _spectrum_kernel(ff_ref, ks_ref, kd_ref, kp_ref, kq_ref):
    seq = ks_ref.shape[0]
    ks = ks_ref[...].astype(BF16)
    kd = kd_ref[...].astype(BF16)
    p = jnp.dot(ff_ref[0:seq, :], ks, preferred_element_type=F32)
    q = jnp.dot(ff_ref[seq:2 * seq, :], kd, preferred_element_type=F32)
    nyq = jnp.dot(ff_ref[seq:seq + 16, :], ks, preferred_element_type=F32)[0:1, :]
    row = lax.broadcasted_iota(jnp.int32, p.shape, 0)
    wf = jnp.where(row == 0, 0.5 / seq, 1.0 / seq)
    kp_ref[...] = p * wf
    kq_ref[...] = jnp.where(row == 0, nyq, q) * wf


def _spectrum(ffwd, ksum, kdiff, tc):
    seq = ksum.shape[1]
    spec = pl.BlockSpec((None, seq, tc), lambda o, j: (o, 0, j))
    return pl.pallas_call(
        _spectrum_kernel,
        grid=(HY_ORDER, D_HY // tc),
        in_specs=[pl.BlockSpec(ffwd.shape, lambda o, j: (0, 0)), spec, spec],
        out_specs=[spec, spec],
        out_shape=[jax.ShapeDtypeStruct(ksum.shape, F32)] * 2,
        compiler_params=_params(("arbitrary", "arbitrary"), ffwd.size * 2 + 12 * seq * tc * 4),
        name="hyena_spectrum",
    )(ffwd, ksum, kdiff)


def _short_conv(z, w, b):
    seq = z.shape[0]
    row = lax.broadcasted_iota(jnp.int32, z.shape, 0)
    prev = jnp.where(row == 0, 0.0, pltpu.roll(z, 1, 0))
    nxt = jnp.where(row == seq - 1, 0.0, pltpu.roll(z, seq - 1, 0))
    return prev * w[0:1, :] + z * w[1:2, :] + nxt * w[2:3, :] + b


def _dft_mul(u, ff_ref, kp_ref, kq_ref, o_ref):
    seq = u.shape[0]
    a = jnp.dot(ff_ref[...], u.astype(BF16), preferred_element_type=F32)
    p, q = a[0:seq], a[seq:2 * seq]
    kp, kq = kp_ref[...], kq_ref[...]
    row0 = lax.broadcasted_iota(jnp.int32, p.shape, 0) == 0
    o_ref[0:seq, :] = jnp.where(row0, kp * p, kp * p + kq * q).astype(o_ref.dtype)
    o_ref[seq:2 * seq, :] = jnp.where(row0, kq * q, kp * q - kq * p).astype(o_ref.dtype)


def _fwd_conv_kernel(z_ref, cw_ref, cb_ref, ff_ref, kp_ref, kq_ref, o_ref):
    _dft_mul(_short_conv(z_ref[...], cw_ref[...], cb_ref[...]), ff_ref, kp_ref, kq_ref, o_ref)


def _fwd_plain_kernel(u_ref, ff_ref, kp_ref, kq_ref, o_ref):
    _dft_mul(u_ref[...], ff_ref, kp_ref, kq_ref, o_ref)


def _inv_first_kernel(y_ref, fi_ref, zv_ref, zx_ref, cwv_ref, cbv_ref, cwx_ref, cbx_ref, skip_ref, o_ref):
    v = _short_conv(zv_ref[...], cwv_ref[...], cbv_ref[...])
    x1 = _short_conv(zx_ref[...], cwx_ref[...], cbx_ref[...])
    y = jnp.dot(fi_ref[...], y_ref[...], preferred_element_type=F32)
    o_ref[...] = x1 * (y + v * skip_ref[...])


def _inv_second_kernel(y_ref, fi_ref, g_ref, zx_ref, cwx_ref, cbx_ref, skip_ref, o_ref):
    x2 = _short_conv(zx_ref[...], cwx_ref[...], cbx_ref[...])
    y = jnp.dot(fi_ref[...], y_ref[...], preferred_element_type=F32)
    o_ref[...] = x2 * (y + g_ref[...] * skip_ref[...])


def _hyena(z, conv_w, conv_b, ffwd, finv, kp, kq, skip, tc):
    bsz, seq, _ = z.shape
    nct = D_HY // tc
    sem = ("arbitrary", "arbitrary")
    zcol = lambda g: pl.BlockSpec((None, seq, tc), lambda b, j: (b, 0, g * nct + j))
    wcol = lambda g: pl.BlockSpec((3, tc), lambda b, j: (0, g * nct + j))
    bcol = lambda g: pl.BlockSpec((1, tc), lambda b, j: (0, g * nct + j))
    act = pl.BlockSpec((None, seq, tc), lambda b, j: (b, 0, j))
    spec2 = pl.BlockSpec((None, 2 * seq, tc), lambda b, j: (b, 0, j))
    kspec = lambda o: pl.BlockSpec((None, seq, tc), lambda b, j: (o, 0, j))
    sspec = lambda o: pl.BlockSpec((None, 1, tc), lambda b, j: (o, 0, j))
    mat = pl.BlockSpec(ffwd.shape, lambda b, j: (0, 0))
    mati = pl.BlockSpec(finv.shape, lambda b, j: (0, 0))
    est = ffwd.size * 2 + 14 * seq * tc * 4
    cb = conv_b.reshape(1, -1)
    skip3 = skip.reshape(HY_ORDER, 1, D_HY)
    spec_shape = jax.ShapeDtypeStruct((bsz, 2 * seq, D_HY), BF16)
    act_shape = jax.ShapeDtypeStruct((bsz, seq, D_HY), F32)

    y1 = pl.pallas_call(
        _fwd_conv_kernel, grid=(bsz, nct),
        in_specs=[zcol(0), wcol(0), bcol(0), mat, kspec(0), kspec(0)],
        out_specs=spec2, out_shape=spec_shape, compiler_params=_params(sem, est), name="hyena_fwd1",
    )(z, conv_w, cb, ffwd, kp, kq)
    g = pl.pallas_call(
        _inv_first_kernel, grid=(bsz, nct),
        in_specs=[spec2, mati, zcol(0), zcol(1), wcol(0), bcol(0), wcol(1), bcol(1), sspec(0)],
        out_specs=act, out_shape=act_shape, compiler_params=_params(sem, est), name="hyena_inv1",
    )(y1, finv, z, z, conv_w, cb, conv_w, cb, skip3)
    y2 = pl.pallas_call(
        _fwd_plain_kernel, grid=(bsz, nct),
        in_specs=[act, mat, kspec(1), kspec(1)],
        out_specs=spec2, out_shape=spec_shape, compiler_params=_params(sem, est), name="hyena_fwd2",
    )(g, ffwd, kp, kq)
    return pl.pallas_call(
        _inv_second_kernel, grid=(bsz, nct),
        in_specs=[spec2, mati, act, zcol(2), wcol(2), bcol(2), sspec(1)],
        out_specs=act, out_shape=act_shape, compiler_params=_params(sem, est), name="hyena_inv2",
    )(y2, finv, g, z, conv_w, cb, skip3)


def _bias_kernel(rpb_ref, o_ref, *, kh):
    h = pl.program_id(0)
    n_r, n_c = 2 * NA_KH - 1, 2 * NA_KW - 1
    cq = lax.broadcasted_iota(jnp.int32, (GRID_W, GRID_W), 0)
    ck = lax.broadcasted_iota(jnp.int32, (GRID_W, GRID_W), 1)
    rel = ck - jnp.clip(cq - NA_KW // 2, 0, GRID_W - NA_KW)
    ok = (rel >= 0) & (rel < NA_KW)
    dcol = jnp.clip(ck - cq + NA_KW - 1, 0, n_c - 1)
    tables = []
    for r in range(n_r):
        acc = jnp.zeros((GRID_W, GRID_W), F32)
        for c in range(n_c):
            acc = jnp.where(dcol == c, rpb_ref[(h * n_r + r) * n_c + c], acc)
        tables.append(jnp.where(ok, acc, NEG_INF))
    for off in range(o_ref.shape[0]):
        for i in range(kh):
            o_ref[off, :, i * GRID_W:(i + 1) * GRID_W] = tables[off + i]


def _bias_bands(rpb, kh):
    heads = rpb.shape[0]
    n_off = 2 * NA_KH - kh
    return pl.pallas_call(
        functools.partial(_bias_kernel, kh=kh),
        grid=(heads,),
        in_specs=[pl.BlockSpec(memory_space=pltpu.SMEM)],
        out_specs=pl.BlockSpec((None, n_off, GRID_W, kh * GRID_W), lambda h: (h, 0, 0, 0)),
        out_shape=jax.ShapeDtypeStruct((heads, n_off, GRID_W, kh * GRID_W), F32),
        compiler_params=_params(("arbitrary",), 4 << 20),
        name="na_bias",
    )(rpb.reshape(-1))


def _attn_kernel(q_ref, k_ref, v_ref, kc_ref, vc_ref, bias0_ref, bias1_ref, *rest, rows, kh, n_riders):
    o_ref = rest[n_riders]
    for src_ref, dst_ref in zip(rest[:n_riders], rest[n_riders + 1:]):
        dst_ref[...] = src_ref[...].astype(dst_ref.dtype)
    r2 = pl.program_id(1)
    nk = kh * GRID_W
    gw = NA_GROUP * HEAD_DIM
    lane = lax.broadcasted_iota(jnp.int32, (1, gw), 1)
    head_of_lane = [(lane >= e * HEAD_DIM) & (lane < (e + 1) * HEAD_DIM) for e in range(NA_GROUP)]
    nt = (((1,), (1,)), ((), ()))
    starts = [pl.multiple_of(jnp.clip(2 * r2 + i - kh // 2, 0, rows - kh) * GRID_W, GRID_W) for i in range(2)]
    bias_refs = (bias0_ref, bias1_ref)
    per_row = NA_GROUP * GRID_W
    zero = jnp.zeros((GRID_W, gw), BF16)
    for hg in range(q_ref.shape[1] // gw):
        cols = slice(hg * gw, (hg + 1) * gw)
        qp = q_ref[:, cols] * jnp.asarray(HEAD_DIM ** -0.5, BF16)
        qm = jnp.concatenate([jnp.where(head_of_lane[e], qp[i * GRID_W:(i + 1) * GRID_W], zero)
                              for i in range(2) for e in range(NA_GROUP)], axis=0)
        s_ctx = lax.dot_general(qm, kc_ref[:, cols], nt, preferred_element_type=F32)
        m_ctx = jnp.max(s_ctx, axis=-1, keepdims=True)
        s_lat, m_all = [], []
        for i in range(2):
            blk = slice(i * per_row, (i + 1) * per_row)
            bias = jnp.concatenate([bias_refs[i][hg * NA_GROUP + e] for e in range(NA_GROUP)], axis=0)
            s = lax.dot_general(qm[blk], k_ref[pl.ds(starts[i], nk), cols], nt, preferred_element_type=F32) + bias
            s_lat.append(s)
            m_all.append(jnp.maximum(jnp.max(s, axis=-1, keepdims=True), m_ctx[blk]))
        p_ctx = jnp.exp(s_ctx - jnp.concatenate(m_all, axis=0))
        o_ctx = jnp.dot(p_ctx.astype(BF16), vc_ref[:, cols], preferred_element_type=F32)
        l_ctx = jnp.sum(p_ctx, axis=-1, keepdims=True)
        for i in range(2):
            blk = slice(i * per_row, (i + 1) * per_row)
            p = jnp.exp(s_lat[i] - m_all[i])
            o = jnp.dot(p.astype(BF16), v_ref[pl.ds(starts[i], nk), cols], preferred_element_type=F32) + o_ctx[blk]
            o = o / (jnp.sum(p, axis=-1, keepdims=True) + l_ctx[blk])
            out = o[0:GRID_W]
            for e in range(1, NA_GROUP):
                out = jnp.where(head_of_lane[e], o[e * GRID_W:(e + 1) * GRID_W], out)
            o_ref[i * GRID_W:(i + 1) * GRID_W, cols] = out


def _attention(qkv, kv_ctx, bands, band_of_row, rows, kh, riders):
    bsz, seq, d3 = qkv.shape
    dn = d3 // 3
    ctx_len = kv_ctx.shape[1]
    heads, _, _, nk = bands.shape
    assert rows % 2 == 0
    n_steps = bsz * (rows // 2)
    slab_specs = []
    for w in riders:
        slab = w.shape[0] // n_steps
        assert slab * n_steps == w.shape[0] and slab % 16 == 0
        slab_specs.append(pl.BlockSpec((slab, w.shape[1]), lambda b, r: (b * (rows // 2) + r, 0)))
    est = (2 * (2 * seq * dn * 2 + 2 * ctx_len * dn * 2 + 2 * heads * GRID_W * nk * 4 + 2 * GRID_W * dn * 6)
           + sum(2 * (w.size // n_steps) * 6 for w in riders))
    band = lambda i: pl.BlockSpec((heads, None, GRID_W, nk), lambda b, r: (0, band_of_row(2 * r + i), 0, 0))
    return pl.pallas_call(
        functools.partial(_attn_kernel, rows=rows, kh=kh, n_riders=len(riders)),
        grid=(bsz, rows // 2),
        in_specs=[pl.BlockSpec((None, 2 * GRID_W, dn), lambda b, r: (b, r, 0)),
                  pl.BlockSpec((None, seq, dn), lambda b, r: (b, 0, 1)),
                  pl.BlockSpec((None, seq, dn), lambda b, r: (b, 0, 2)),
                  pl.BlockSpec((None, ctx_len, dn), lambda b, r: (b, 0, 0)),
                  pl.BlockSpec((None, ctx_len, dn), lambda b, r: (b, 0, 1)),
                  band(0), band(1)] + slab_specs,
        out_specs=[pl.BlockSpec((None, 2 * GRID_W, dn), lambda b, r: (b, r, 0))] + slab_specs,
        out_shape=[jax.ShapeDtypeStruct((bsz, seq, dn), F32)]
        + [jax.ShapeDtypeStruct(w.shape, BF16) for w in riders],
        compiler_params=_params(("arbitrary", "arbitrary"), est),
        name="na_attention",
    )(qkv, qkv, qkv, kv_ctx, kv_ctx, bands, bands, *riders)


def _rms(x, g):
    return x * lax.rsqrt(jnp.mean(x * x, axis=-1, keepdims=True) + LN_EPS) * g


def _layer_norm(x, g, b):
    xc = x - jnp.mean(x, axis=-1, keepdims=True)
    var = jnp.mean(xc * xc, axis=-1, keepdims=True)
    return xc * lax.rsqrt(var + LN_EPS) * g + b


def _merge_kernel(hy_ref, na_ref, x_ref, mod_ref, w_ref, ghy_ref, gna_ref, lg_ref, lb_ref, xo_ref, uo_ref, *, alpha):
    dh = hy_ref.shape[1]
    hy = _rms(hy_ref[...], ghy_ref[...]).astype(BF16)
    na = _rms(na_ref[...], gna_ref[...]).astype(BF16)
    y = (jnp.dot(hy, w_ref[0:dh, :], preferred_element_type=F32)
         + jnp.dot(na, w_ref[dh:, :], preferred_element_type=F32))
    xn = _layer_norm(alpha * x_ref[...] + mod_ref[2:3, :] * y, lg_ref[...], lb_ref[...])
    xo_ref[...] = xn
    uo_ref[...] = (xn * (1.0 + mod_ref[4:5, :]) + mod_ref[3:4, :]).astype(uo_ref.dtype)


def _merge(hy, na, x, mod, w_out, g_hy, g_na, ln_g, ln_b, alpha, tm):
    bsz, seq, d = x.shape
    dh, dn = hy.shape[2], na.shape[2]
    row = lambda a: a.reshape(1, -1)
    vec = lambda n: pl.BlockSpec((1, n), lambda b, i: (0, 0))
    tile = lambda n: pl.BlockSpec((None, tm, n), lambda b, i: (b, i, 0))
    est = d * d * 2 + 2 * tm * (dh + dn + 2 * d) * 4 + 2 * tm * d * 6 + 4 * tm * d * 4
    return pl.pallas_call(
        functools.partial(_merge_kernel, alpha=alpha),
        grid=(bsz, seq // tm),
        in_specs=[tile(dh), tile(dn), tile(d),
                  pl.BlockSpec((None, 6, d), lambda b, i: (b, 0, 0)),
                  pl.BlockSpec((d, d), lambda b, i: (0, 0)),
                  vec(dh), vec(dn), vec(d), vec(d)],
        out_specs=[tile(d), tile(d)],
        out_shape=[jax.ShapeDtypeStruct((bsz, seq, d), F32), jax.ShapeDtypeStruct((bsz, seq, d), BF16)],
        compiler_params=_params(("arbitrary", "arbitrary"), est),
        name="merge",
    )(hy, na, x, mod, w_out, row(g_hy), row(g_na), row(ln_g), row(ln_b))


def _first(a, b):
    return (a[0] > b[0]) | ((a[0] == b[0]) & (a[1] < b[1]))


def _order_pair(xs, i, j):
    a, b = xs[i], xs[j]
    f = _first(a, b)
    xs[i] = (jnp.maximum(a[0], b[0]),) + tuple(jnp.where(f, p, r) for p, r in zip(a[1:], b[1:]))
    xs[j] = (jnp.minimum(a[0], b[0]),) + tuple(jnp.where(f, r, p) for p, r in zip(a[1:], b[1:]))


def _bitonic_merge(xs):
    n = len(xs)
    j = n // 2
    while j >= 1:
        for i in range(n):
            if i & j == 0:
                _order_pair(xs, i, i | j)
        j //= 2
    return xs


def _sort(xs):
    n = len(xs)
    p = 1
    while p < n:
        k = p
        while k >= 1:
            for j in range(k % p, n - k, 2 * k):
                for i in range(min(k, n - j - k)):
                    if (i + j) // (2 * p) == (i + j + k) // (2 * p):
                        _order_pair(xs, i + j, i + j + k)
            k //= 2
        p *= 2
    return xs


def _leading_half(a, b):
    out = []
    for x, y in zip(a, reversed(b)):
        f = _first(x, y)
        out.append((jnp.maximum(x[0], y[0]),) + tuple(jnp.where(f, p, r) for p, r in zip(x[1:], y[1:])))
    return out


def _sorted_top(items, k):
    runs = [_sort(items[i:i + k]) for i in range(0, len(items), k)]
    while len(runs) > 1:
        runs = [_bitonic_merge(_leading_half(runs[i], runs[i + 1])) for i in range(0, len(runs), 2)]
    return runs[0]


def _pitch(rows):
    return rows + 8


def _route_kernel(u_ref, wq_ref, keys_ref, eu_ref, ev_ref, idx_ref, gate_ref, eub_ref, evb_ref, s_ref):
    eub_ref[...] = eu_ref[...].astype(eub_ref.dtype)
    evb_ref[...] = ev_ref[...].astype(evb_ref.dtype)
    nkeys, half = keys_ref.shape[1], keys_ref.shape[2]
    k = PEER_TOPK
    groups = u_ref.shape[0] // V7X_LANES
    pitch = _pitch(nkeys)
    tile = (groups, V7X_LANES)
    q = jnp.dot(u_ref[...], wq_ref[...], preferred_element_type=F32)
    nt = (((1,), (1,)), ((), ()))

    def split(v):
        hi = v.astype(BF16)
        return hi, (v - hi.astype(F32)).astype(BF16)

    for p in range(2):
        k_hi, k_lo = split(keys_ref[p])
        q_hi, q_lo = split(q[:, p * half:(p + 1) * half])
        dot = lambda a, b: lax.dot_general(a, b, nt, preferred_element_type=F32)
        s = dot(k_hi, q_hi) + (dot(k_hi, q_lo) + dot(k_lo, q_hi))
        for g in range(groups):
            s_ref[p, g * pitch:g * pitch + nkeys, :] = s[:, g * V7X_LANES:(g + 1) * V7X_LANES]

    def scores(p):
        return [(s_ref[p, pl.ds(n, groups, stride=pitch), :], jnp.full(tile, float(n), F32)) for n in range(nkeys)]

    top_a = _sorted_top(scores(0), k)
    top_b = _sorted_top(scores(1), k)

    def cand(i, j):
        return (top_a[i][0] + top_b[j][0], jnp.full(tile, float(i * k + j), F32),
                top_a[i][1] * float(nkeys) + top_b[j][1])

    pad = (jnp.full(tile, -jnp.inf, F32), jnp.full(tile, float(k * k), F32), jnp.zeros(tile, F32))
    run = lambda i: [cand(i, j) for j in range(k // (i + 1))]
    tail = [cand(i, 0) for i in range(k // 2, k)]
    assert k == 16
    r0 = run(0)
    r1 = _bitonic_merge(run(1) + tail[::-1])
    r2 = _sort(run(2) + run(3) + run(4) + run(5) + run(6))
    r3 = run(7) + [pad] * (k - 2)
    best = _leading_half(_bitonic_merge(_leading_half(r0, r1)), _bitonic_merge(_leading_half(r2, r3)))

    m = functools.reduce(jnp.maximum, [c[0] for c in best])
    e = [jnp.exp(c[0] - m) for c in best]
    inv = 1.0 / functools.reduce(jnp.add, e)
    for n in range(k):
        idx_ref[n] = best[n][2].astype(jnp.int32)
        gate_ref[n] = e[n] * inv


def _route(u, wq, keys, eu, ev, tm):
    t, d = u.shape
    heads, _, nkeys, half = keys.shape
    groups = tm // V7X_LANES
    n_steps = (t // tm) * heads
    n_exp, de = eu.shape
    slab = n_exp // n_steps
    assert slab * n_steps == n_exp and slab % 16 == 0
    est = (2 * (tm * d * 2 + d * 2 * half * 2 + 2 * nkeys * half * 4) + 8 * nkeys * tm * 4
           + 2 * 2 * slab * de * 6)
    out_spec = pl.BlockSpec((PEER_TOPK, groups, V7X_LANES), lambda i, h: (h, i, 0))
    slab_spec = pl.BlockSpec((slab, de), lambda i, h: (i * heads + h, 0))
    return pl.pallas_call(
        _route_kernel,
        grid=(t // tm, heads),
        in_specs=[pl.BlockSpec((tm, d), lambda i, h: (i, 0)),
                  pl.BlockSpec((d, 2 * half), lambda i, h: (0, h)),
                  pl.BlockSpec((None, 2, nkeys, half), lambda i, h: (h, 0, 0, 0)),
                  slab_spec, slab_spec],
        out_specs=[out_spec, out_spec, slab_spec, slab_spec],
        out_shape=[jax.ShapeDtypeStruct((heads * PEER_TOPK, t // V7X_LANES, V7X_LANES), jnp.int32),
                   jax.ShapeDtypeStruct((heads * PEER_TOPK, t // V7X_LANES, V7X_LANES), F32),
                   jax.ShapeDtypeStruct(eu.shape, BF16), jax.ShapeDtypeStruct(ev.shape, BF16)],
        scratch_shapes=[pltpu.VMEM((2, groups * _pitch(nkeys), V7X_LANES), F32)],
        compiler_params=_params(("arbitrary", "arbitrary"), est),
        name="peer_route",
    )(u, wq, keys, eu, ev)


def _gate_matrix_kernel(idx_ref, gate_ref, o_ref, w0_ref, w1_ref):
    step = pl.program_id(0)

    @pl.when(step == 0)
    def _():
        w1_ref[...] = jnp.zeros_like(w1_ref)

    @pl.when(step % 2 == 0)
    def _():
        _gate_matrix_step(idx_ref, gate_ref, o_ref, w0_ref, w1_ref)

    @pl.when(step % 2 == 1)
    def _():
        _gate_matrix_step(idx_ref, gate_ref, o_ref, w1_ref, w0_ref)


def _gate_matrix_step(idx_ref, gate_ref, o_ref, cur_ref, prev_ref):
    tw, nsel = idx_ref.shape
    pitch = _pitch(PEER_NKEYS)
    key = lax.broadcasted_iota(jnp.int32, (PEER_NKEYS, nsel), 0)
    nt = (((1,), (1,)), ((), ()))
    zero = jnp.zeros((PEER_NKEYS, nsel), BF16)
    n_pairs = tw // 2
    planes_per_pair = PEER_NKEYS // n_pairs
    for p in range(n_pairs):
        a_hot, b_hot = [], []
        for t in (2 * p, 2 * p + 1):
            idx = idx_ref[t:t + 1, :]
            gate = gate_ref[t:t + 1, :]
            a_hot.append(jnp.where((idx >> NKEYS_SHIFT) == key, gate, 0.0).astype(BF16))
            b_hot.append(jnp.where((idx & (PEER_NKEYS - 1)) == key, 1.0, 0.0).astype(BF16))
        lhs = jnp.concatenate(a_hot, axis=1)
        rhs = jnp.concatenate([jnp.concatenate([b_hot[0], zero], axis=1),
                               jnp.concatenate([zero, b_hot[1]], axis=1)], axis=0)
        planes = lax.dot_general(lhs, rhs, nt, preferred_element_type=F32)
        for k in range(2):
            row = (2 * p + k) * pitch
            cur_ref[row:row + PEER_NKEYS, :] = planes[:, k * PEER_NKEYS:(k + 1) * PEER_NKEYS]
        for a in range(p * planes_per_pair, (p + 1) * planes_per_pair):
            o_ref[:, a * PEER_NKEYS:(a + 1) * PEER_NKEYS] = (
                prev_ref[pl.ds(a, tw, stride=pitch), :].astype(o_ref.dtype))


def _gate_matrix(idx, gate, tw):
    t, nsel = idx.shape
    n_exp = PEER_NKEYS * PEER_NKEYS
    n_blocks = t // tw
    assert PEER_NKEYS % (tw // 2) == 0
    est = 2 * (2 * tw * nsel * 4 + tw * n_exp * 2) + 2 * tw * _pitch(PEER_NKEYS) * PEER_NKEYS * 4
    block_in = pl.BlockSpec((tw, nsel), lambda i: (jnp.minimum(i, n_blocks - 1), 0))
    return pl.pallas_call(
        _gate_matrix_kernel,
        grid=(n_blocks + 1,),
        in_specs=[block_in, block_in],
        out_specs=pl.BlockSpec((tw, n_exp), lambda i: (jnp.maximum(i - 1, 0), 0)),
        out_shape=jax.ShapeDtypeStruct((t, n_exp), BF16),
        scratch_shapes=[pltpu.VMEM((tw * _pitch(PEER_NKEYS), PEER_NKEYS), F32)] * 2,
        compiler_params=_params(("arbitrary",), est),
        name="peer_gate_matrix",
    )(idx, gate)


def _peer_kernel(u_ref, eu_ref, ev_ref, w_ref, o_ref, *, n_chunk):
    @pl.when(pl.program_id(1) == 0)
    def _():
        o_ref[...] = jnp.zeros_like(o_ref)

    z = lax.dot_general(u_ref[...], eu_ref[...], (((1,), (1,)), ((), ())), preferred_element_type=F32)
    act = 0.5 * z * (1.0 + lax.erf(z * (2.0 ** -0.5)))
    h = (w_ref[...].astype(F32) * act).astype(BF16)
    for c in range(0, o_ref.shape[1], n_chunk):
        o_ref[:, c:c + n_chunk] += jnp.dot(h, ev_ref[:, c:c + n_chunk], preferred_element_type=F32)


def _final_ln_kernel(x_ref, f_ref, mod_ref, lg_ref, lb_ref, o_ref, *, alpha):
    r = alpha * x_ref[...] + mod_ref[5:6, :] * f_ref[...]
    o_ref[...] = _layer_norm(r, lg_ref[...], lb_ref[...])


def _final_ln(x, f, mod, ln_g, ln_b, alpha, seq, tm):
    t, d = x.shape
    tile = pl.BlockSpec((tm, d), lambda i: (i, 0))
    vec = pl.BlockSpec((1, d), lambda i: (0, 0))
    return pl.pallas_call(
        functools.partial(_final_ln_kernel, alpha=alpha),
        grid=(t // tm,),
        in_specs=[tile, tile, pl.BlockSpec((None, 6, d), lambda i: ((i * tm) // seq, 0, 0)), vec, vec],
        out_specs=tile,
        out_shape=jax.ShapeDtypeStruct((t, d), F32),
        compiler_params=_params(("arbitrary",), 8 * tm * d * 4),
        name="final_ln",
    )(x, f, mod, ln_g.reshape(1, -1), ln_b.reshape(1, -1))


def _peer(u, eu, ev, w, tm, te, n_chunk):
    t, d = u.shape
    n_exp = ev.shape[0]
    once = pl.Buffered(1)
    est = tm * d * 2 + tm * d * 4 + 2 * (2 * te * d * 2 + tm * te * 2) + 4 * tm * te * 4 + tm * n_chunk * 4
    return pl.pallas_call(
        functools.partial(_peer_kernel, n_chunk=n_chunk),
        grid=(t // tm, n_exp // te),
        in_specs=[pl.BlockSpec((tm, d), lambda i, j: (i, 0), pipeline_mode=once),
                  pl.BlockSpec((te, d), lambda i, j: (j, 0)),
                  pl.BlockSpec((te, d), lambda i, j: (j, 0)),
                  pl.BlockSpec((tm, te), lambda i, j: (i, j))],
        out_specs=pl.BlockSpec((tm, d), lambda i, j: (i, 0), pipeline_mode=once),
        out_shape=jax.ShapeDtypeStruct((t, d), F32),
        compiler_params=_params(("arbitrary", "arbitrary"), est),
        name="peer_experts",
    )(u, eu, ev, w)


def _filter_features(seq):
    t = jnp.linspace(0.0, 1.0, seq, dtype=F32)[:, None]
    bands = (FILTER_EMB - 1) // 2
    w = 2.0 * math.pi * jnp.arange(seq, dtype=F32)[:, None] / seq
    f = jnp.linspace(1e-4, bands - 1, bands, dtype=F32)[None, :]
    z = jnp.concatenate([t, jnp.cos(f * w), -jnp.sin(f * w)], axis=-1)
    return jnp.pad(z, ((0, 0), (0, FILTER_PAD - FILTER_EMB)))


@functools.lru_cache(maxsize=None)
def _dft_matrices(seq):
    n2 = 2 * seq
    f = np.arange(seq, dtype=np.int64)[:, None]
    n = np.arange(seq, dtype=np.int64)[None, :]
    ang = ((f * n) % n2).astype(np.float64) * (2.0 * math.pi / n2)
    sin = np.where(f == 0, (1 - 2 * (n % 2)).astype(np.float64), np.sin(ang))
    return np.concatenate([np.cos(ang), sin], axis=0).astype(np.float32)


def kernel(x, c, ctx, c_ctx, w_ada, b_ada, w_in, conv_w, conv_b, filt_w1, filt_b1, filt_w2, filt_b2,
           filt_w3, filt_b3, filt_w4, filt_freq, hy_skip, na_rpb, hy_norm_g, na_norm_g, w_out,
           ln1_g, ln1_b, peer_wq, peer_keys, peer_u, peer_v, ln2_g, ln2_b):
    depth = w_ada.shape[0]
    assert depth == 1, "single-layer block: the context stream is never updated"
    bsz, seq, d = x.shape
    rows = seq // GRID_W
    kh = min(NA_KH, rows)
    assert kh == NA_KH and seq % GRID_W == 0
    alpha = (2.0 * depth) ** 0.25
    d_na = d - D_HY
    off_q = 3 * D_HY
    off_k = off_q + d_na

    n_cond = 8
    cond = jnp.concatenate([c, c_ctx[None], jnp.zeros((n_cond - bsz - 1, d), F32)], axis=0)
    mod = _ada(cond, w_ada[0], b_ada[0]).reshape(n_cond, 6, d)

    w_cols = lambda lo, hi: w_in[0][:, lo:hi].astype(BF16)
    z_hy = _proj(x, mod, lambda b: b, w_cols(0, off_q), F32, 512)
    qkv = _proj(x, mod, lambda b: b, w_cols(off_q, None), BF16, 512)
    kv_ctx = _proj(ctx, mod, lambda b: bsz, w_cols(off_k, None), BF16, 256)

    deltas = jnp.abs(jnp.linspace(math.log(DECAY_TARGET) / SLOW_DECAY_PCT,
                                  math.log(DECAY_TARGET) / FAST_DECAY_PCT, D_HY, dtype=F32))[None, :]
    padw = lambda a: jnp.pad(a, ((0, FILTER_PAD - a.shape[0]), (0, FILTER_PAD - a.shape[1])))
    padv = lambda a: jnp.pad(a, ((0, FILTER_PAD - a.shape[0]),))
    w4 = jnp.pad(filt_w4[0], ((0, FILTER_PAD - filt_w4.shape[1]), (0, 0)))
    freq = jnp.pad(filt_freq[0], ((0, 0), (0, FILTER_PAD - filt_freq.shape[2])))
    ksum, kdiff = _hyena_filters(_filter_features(seq), padw(filt_w1[0]), padv(filt_b1[0]), padw(filt_w2[0]),
                                 padv(filt_b2[0]), padw(filt_w3[0]), padv(filt_b3[0]), w4, freq, deltas, 256)
    ffwd = jnp.asarray(_dft_matrices(seq)).astype(BF16)
    finv = ffwd.T
    kp, kq = _spectrum(ffwd, ksum, kdiff, 256)
    hy_out = _hyena(z_hy, conv_w[0], conv_b[0], ffwd, finv, kp, kq, hy_skip[0], 256)

    bands = _bias_bands(na_rpb[0], kh)
    band_of_row = lambda r: jnp.clip(r - kh // 2, 0, rows - kh) - r + (NA_KH - 1)
    na_out, w_out_b, wq_b = _attention(qkv, kv_ctx, bands, band_of_row, rows, kh, [w_out[0], peer_wq[0]])

    x_mid, u_peer = _merge(hy_out, na_out, x, mod, w_out_b, hy_norm_g[0], na_norm_g[0],
                           ln1_g[0], ln1_b[0], alpha, 256)

    t = bsz * seq
    u2 = u_peer.reshape(t, d)
    idx_t, gate_t, eu_b, ev_b = _route(u2, wq_b, peer_keys[0], peer_u[0], peer_v[0], 1024)
    w_gate = _gate_matrix(idx_t.reshape(-1, t).T, gate_t.reshape(-1, t).T, 64)
    f = _peer(u2, eu_b, ev_b, w_gate, 2048, 512, 512)
    out = _final_ln(x_mid.reshape(t, d), f, mod, ln2_g[0], ln2_b[0], alpha, seq, 512)
    return out.reshape(bsz, seq, d)
```

```python
import functools
import math

import numpy as np
import jax
import jax.numpy as jnp
from jax import lax
from jax.experimental import pallas as pl
from jax.experimental.pallas import tpu as pltpu

F32 = jnp.float32
BF16 = jnp.bfloat16
HIGHEST = lax.Precision.HIGHEST

GRID_W = 64
HEAD_DIM = 64
D_HY = 1024
HY_ORDER = 2
FILTER_EMB = 33
FILTER_PAD = 128
DECAY_TARGET = 1e-2
FAST_DECAY_PCT = 0.3
SLOW_DECAY_PCT = 1.5
NA_KH = 8
NA_KW = 16
PEER_HEADS = 8
PEER_NKEYS = 128
NKEYS_SHIFT = PEER_NKEYS.bit_length() - 1
assert 1 << NKEYS_SHIFT == PEER_NKEYS
PEER_TOPK = 16
LN_EPS = 1e-5
NEG_INF = -1e30

V7X_VMEM_BYTES = 64 * 1024 * 1024
V7X_LANES = 128
V7X_MXU_DEPTH = 256
NA_GROUP = V7X_MXU_DEPTH // HEAD_DIM


MIB = 1024 * 1024
VMEM_TEMPORARIES = 12 * MIB
VMEM_RESERVED = 6 * MIB

ADA_TN = 2048
PROJ_TM = 512
CTX_TM = 256
HYENA_TC = 256
MERGE_TM = 256
ROUTE_TM = 1024
GATE_TW = 64
PEER_TM, PEER_TE, PEER_CHUNK = 2048, 512, 512
FINAL_TM = 512


def _vmem_limit(estimate_bytes):
    return int(min(estimate_bytes + VMEM_TEMPORARIES, V7X_VMEM_BYTES - VMEM_RESERVED))


def _params(semantics, estimate_bytes):
    return pltpu.CompilerParams(dimension_semantics=semantics, vmem_limit_bytes=_vmem_limit(estimate_bytes))


def _ada_kernel(cond_ref, w_ref, b_ref, o_ref):
    s = jax.nn.silu(cond_ref[...])
    o_ref[...] = jnp.dot(s, w_ref[...], precision=HIGHEST, preferred_element_type=F32) + b_ref[...]


def _ada(cond, w, b):
    n, d = cond.shape
    tn = ADA_TN
    return pl.pallas_call(
        _ada_kernel,
        grid=(w.shape[1] // tn,),
        in_specs=[pl.BlockSpec((n, d), lambda j: (0, 0)),
                  pl.BlockSpec((d, tn), lambda j: (0, j)),
                  pl.BlockSpec((1, tn), lambda j: (0, j))],
        out_specs=pl.BlockSpec((n, tn), lambda j: (0, j)),
        out_shape=jax.ShapeDtypeStruct((n, w.shape[1]), F32),
        compiler_params=_params(("arbitrary",), 2 * d * tn * 4),
        name="ada",
    )(cond, w, b.reshape(1, -1))


def _rider_specs(riders, n_steps, step_of):
    specs = []
    for w in riders:
        slab = w.shape[0] // n_steps
        assert slab * n_steps == w.shape[0] and slab % 16 == 0
        specs.append(pl.BlockSpec((slab, w.shape[1]), lambda *g: (step_of(*g), 0)))
    return specs


def _rider_bytes(riders, n_steps):
    return sum(2 * (w.size // n_steps) * 6 for w in riders)


def _convert_riders(src_refs, dst_refs):
    for src_ref, dst_ref in zip(src_refs, dst_refs):
        dst_ref[...] = src_ref[...].astype(dst_ref.dtype)


def _proj_kernel(x_ref, mod_ref, w_ref, *rest, n_riders):
    o_ref = rest[n_riders]
    _convert_riders(rest[:n_riders], rest[n_riders + 1:])
    u = x_ref[...] * (1.0 + mod_ref[1:2, :]) + mod_ref[0:1, :]
    o_ref[...] = jnp.dot(u.astype(BF16), w_ref[...], preferred_element_type=F32).astype(o_ref.dtype)


def _proj(x, mod, mod_row, w, out_dtype, tm, riders=()):
    bsz, seq, d = x.shape
    n = w.shape[1]
    n_i = seq // tm
    slabs = _rider_specs(riders, bsz * n_i, lambda b, i: b * n_i + i)
    est = d * n * 2 + 2 * tm * d * 4 + 2 * tm * n * 4 + tm * n * 4 + _rider_bytes(riders, bsz * n_i)
    return pl.pallas_call(
        functools.partial(_proj_kernel, n_riders=len(riders)),
        grid=(bsz, n_i),
        in_specs=[pl.BlockSpec((None, tm, d), lambda b, i: (b, i, 0)),
                  pl.BlockSpec((None, 6, d), lambda b, i: (mod_row(b), 0, 0)),
                  pl.BlockSpec((d, n), lambda b, i: (0, 0))] + slabs,
        out_specs=[pl.BlockSpec((None, tm, n), lambda b, i: (b, i, 0))] + slabs,
        out_shape=[jax.ShapeDtypeStruct((bsz, seq, n), out_dtype)]
        + [jax.ShapeDtypeStruct(r.shape, BF16) for r in riders],
        compiler_params=_params(("arbitrary", "arbitrary"), est),
        name="proj",
    )(x, mod, w, *riders)


def _filter_kernel(z_ref, w1_ref, b1_ref, w2_ref, b2_ref, w3_ref, b3_ref, fr_ref, w4f_ref, w4b_ref, dl_ref,
                   ks_ref, kd_ref, h_ref):
    dot = functools.partial(jnp.dot, precision=HIGHEST, preferred_element_type=F32)

    @pl.when((pl.program_id(0) == 0) & (pl.program_id(1) == 0))
    def _():
        h = jnp.sin(fr_ref[0:1, :] * (dot(z_ref[...], w1_ref[...]) + b1_ref[...]))
        h = jnp.sin(fr_ref[1:2, :] * (dot(h, w2_ref[...]) + b2_ref[...]))
        h_ref[...] = jnp.sin(fr_ref[2:3, :] * (dot(h, w3_ref[...]) + b3_ref[...]))

    h = h_ref[...]
    decay = jnp.exp(-z_ref[:, 0:1] * dl_ref[...])
    hf = dot(h, w4f_ref[...]) * decay
    hb = dot(h, w4b_ref[...]) * decay
    norm = jnp.sum(jnp.abs(hf) + jnp.abs(hb), axis=0, keepdims=True)
    hf = hf / norm
    hb = hb / norm
    row = lax.broadcasted_iota(jnp.int32, hb.shape, 0)
    hb = jnp.where(row == 0, 0.0, hb)
    ks_ref[...] = hf + hb
    kd_ref[...] = hb - hf


def _hyena_filters(zfeat, w1, b1, w2, b2, w3, b3, w4, freq, deltas, tc):
    seq = zfeat.shape[0]
    fo = w2.shape[0]
    nct = D_HY // tc
    full = lambda a: pl.BlockSpec(a.shape, lambda o, j: (0,) * a.ndim)
    args = (zfeat, w1, b1.reshape(1, -1), w2, b2.reshape(1, -1), w3, b3.reshape(1, -1), freq)
    out_spec = pl.BlockSpec((None, seq, tc), lambda o, j: (o, 0, j))
    return pl.pallas_call(
        _filter_kernel,
        grid=(HY_ORDER, nct),
        in_specs=[full(a) for a in args] + [
            pl.BlockSpec((fo, tc), lambda o, j: (0, (2 * o) * nct + j)),
            pl.BlockSpec((fo, tc), lambda o, j: (0, (2 * o + 1) * nct + j)),
            pl.BlockSpec((1, tc), lambda o, j: (0, j))],
        out_specs=[out_spec, out_spec],
        out_shape=[jax.ShapeDtypeStruct((HY_ORDER, seq, D_HY), F32)] * 2,
        scratch_shapes=[pltpu.VMEM((seq, fo), F32)],
        compiler_params=_params(("arbitrary", "arbitrary"), 12 * seq * tc * 4),
        name="hyena_filter",
    )(*args, w4, w4, deltas)


def _spectrum_kernel(ff_ref, ks_ref, kd_ref, kp_ref, kq_ref):
    seq = ks_ref.shape[0]
    ks = ks_ref[...].astype(BF16)
    kd = kd_ref[...].astype(BF16)
    p = jnp.dot(ff_ref[0:seq, :], ks, preferred_element_type=F32)
    q = jnp.dot(ff_ref[seq:2 * seq, :], kd, preferred_element_type=F32)
    nyq = jnp.dot(ff_ref[seq:seq + 16, :], ks, preferred_element_type=F32)[0:1, :]
    row = lax.broadcasted_iota(jnp.int32, p.shape, 0)
    wf = jnp.where(row == 0, 0.5 / seq, 1.0 / seq)
    kp_ref[...] = p * wf
    kq_ref[...] = jnp.where(row == 0, nyq, q) * wf


def _spectrum(ffwd, ksum, kdiff, tc):
    seq = ksum.shape[1]
    spec = pl.BlockSpec((None, seq, tc), lambda o, j: (o, 0, j))
    return pl.pallas_call(
        _spectrum_kernel,
        grid=(HY_ORDER, D_HY // tc),
        in_specs=[pl.BlockSpec(ffwd.shape, lambda o, j: (0, 0)), spec, spec],
        out_specs=[spec, spec],
        out_shape=[jax.ShapeDtypeStruct(ksum.shape, F32)] * 2,
        compiler_params=_params(("arbitrary", "arbitrary"), ffwd.size * 2 + 12 * seq * tc * 4),
        name="hyena_spectrum",
    )(ffwd, ksum, kdiff)


def _short_conv(z, w, b):
    seq = z.shape[0]
    row = lax.broadcasted_iota(jnp.int32, z.shape, 0)
    prev = jnp.where(row == 0, 0.0, pltpu.roll(z, 1, 0))
    nxt = jnp.where(row == seq - 1, 0.0, pltpu.roll(z, seq - 1, 0))
    return prev * w[0:1, :] + z * w[1:2, :] + nxt * w[2:3, :] + b


def _dft_mul(u, ff_ref, kp_ref, kq_ref, o_ref):
    seq = u.shape[0]
    a = jnp.dot(ff_ref[...], u.astype(BF16), preferred_element_type=F32)
    p, q = a[0:seq], a[seq:2 * seq]
    kp, kq = kp_ref[...], kq_ref[...]
    row0 = lax.broadcasted_iota(jnp.int32, p.shape, 0) == 0
    o_ref[0:seq, :] = jnp.where(row0, kp * p, kp * p + kq * q).astype(o_ref.dtype)
    o_ref[seq:2 * seq, :] = jnp.where(row0, kq * q, kp * q - kq * p).astype(o_ref.dtype)


def _fwd_conv_kernel(z_ref, cw_ref, cb_ref, ff_ref, kp_ref, kq_ref, o_ref):
    _dft_mul(_short_conv(z_ref[...], cw_ref[...], cb_ref[...]), ff_ref, kp_ref, kq_ref, o_ref)


def _fwd_plain_kernel(u_ref, ff_ref, kp_ref, kq_ref, o_ref):
    _dft_mul(u_ref[...], ff_ref, kp_ref, kq_ref, o_ref)


def _inv_first_kernel(y_ref, fi_ref, zv_ref, zx_ref, cwv_ref, cbv_ref, cwx_ref, cbx_ref, skip_ref, o_ref):
    v = _short_conv(zv_ref[...], cwv_ref[...], cbv_ref[...])
    x1 = _short_conv(zx_ref[...], cwx_ref[...], cbx_ref[...])
    y = jnp.dot(fi_ref[...], y_ref[...], preferred_element_type=F32)
    o_ref[...] = x1 * (y + v * skip_ref[...])


def _inv_second_kernel(y_ref, fi_ref, g_ref, zx_ref, cwx_ref, cbx_ref, skip_ref, o_ref):
    x2 = _short_conv(zx_ref[...], cwx_ref[...], cbx_ref[...])
    y = jnp.dot(fi_ref[...], y_ref[...], preferred_element_type=F32)
    o_ref[...] = x2 * (y + g_ref[...] * skip_ref[...])


def _hyena(z, conv_w, conv_b, ffwd, finv, kp, kq, skip, tc):
    bsz, seq, _ = z.shape
    nct = D_HY // tc
    sem = ("arbitrary", "arbitrary")
    zcol = lambda g: pl.BlockSpec((None, seq, tc), lambda b, j: (b, 0, g * nct + j))
    wcol = lambda g: pl.BlockSpec((3, tc), lambda b, j: (0, g * nct + j))
    bcol = lambda g: pl.BlockSpec((1, tc), lambda b, j: (0, g * nct + j))
    act = pl.BlockSpec((None, seq, tc), lambda b, j: (b, 0, j))
    spec2 = pl.BlockSpec((None, 2 * seq, tc), lambda b, j: (b, 0, j))
    kspec = lambda o: pl.BlockSpec((None, seq, tc), lambda b, j: (o, 0, j))
    sspec = lambda o: pl.BlockSpec((None, 1, tc), lambda b, j: (o, 0, j))
    mat = pl.BlockSpec(ffwd.shape, lambda b, j: (0, 0))
    mati = pl.BlockSpec(finv.shape, lambda b, j: (0, 0))
    est = ffwd.size * 2 + 14 * seq * tc * 4
    cb = conv_b.reshape(1, -1)
    skip3 = skip.reshape(HY_ORDER, 1, D_HY)
    spec_shape = jax.ShapeDtypeStruct((bsz, 2 * seq, D_HY), BF16)
    act_shape = jax.ShapeDtypeStruct((bsz, seq, D_HY), F32)

    y1 = pl.pallas_call(
        _fwd_conv_kernel, grid=(bsz, nct),
        in_specs=[zcol(0), wcol(0), bcol(0), mat, kspec(0), kspec(0)],
        out_specs=spec2, out_shape=spec_shape, compiler_params=_params(sem, est), name="hyena_fwd1",
    )(z, conv_w, cb, ffwd, kp, kq)
    g = pl.pallas_call(
        _inv_first_kernel, grid=(bsz, nct),
        in_specs=[spec2, mati, zcol(0), zcol(1), wcol(0), bcol(0), wcol(1), bcol(1), sspec(0)],
        out_specs=act, out_shape=act_shape, compiler_params=_params(sem, est), name="hyena_inv1",
    )(y1, finv, z, z, conv_w, cb, conv_w, cb, skip3)
    y2 = pl.pallas_call(
        _fwd_plain_kernel, grid=(bsz, nct),
        in_specs=[act, mat, kspec(1), kspec(1)],
        out_specs=spec2, out_shape=spec_shape, compiler_params=_params(sem, est), name="hyena_fwd2",
    )(g, ffwd, kp, kq)
    return pl.pallas_call(
        _inv_second_kernel, grid=(bsz, nct),
        in_specs=[spec2, mati, act, zcol(2), wcol(2), bcol(2), sspec(1)],
        out_specs=act, out_shape=act_shape, compiler_params=_params(sem, est), name="hyena_inv2",
    )(y2, finv, g, z, conv_w, cb, skip3)


def _bias_kernel(rpb_ref, o_ref, *, kh):
    h = pl.program_id(0)
    n_r, n_c = 2 * NA_KH - 1, 2 * NA_KW - 1
    cq = lax.broadcasted_iota(jnp.int32, (GRID_W, GRID_W), 0)
    ck = lax.broadcasted_iota(jnp.int32, (GRID_W, GRID_W), 1)
    rel = ck - jnp.clip(cq - NA_KW // 2, 0, GRID_W - NA_KW)
    ok = (rel >= 0) & (rel < NA_KW)
    dcol = jnp.clip(ck - cq + NA_KW - 1, 0, n_c - 1)
    tables = []
    for r in range(n_r):
        acc = jnp.zeros((GRID_W, GRID_W), F32)
        for c in range(n_c):
            acc = jnp.where(dcol == c, rpb_ref[(h * n_r + r) * n_c + c], acc)
        tables.append(jnp.where(ok, acc, NEG_INF))
    for off in range(o_ref.shape[0]):
        for i in range(kh):
            o_ref[off, :, i * GRID_W:(i + 1) * GRID_W] = tables[off + i]


def _bias_bands(rpb, kh):
    heads = rpb.shape[0]
    n_off = 2 * NA_KH - kh
    return pl.pallas_call(
        functools.partial(_bias_kernel, kh=kh),
        grid=(heads,),
        in_specs=[pl.BlockSpec(memory_space=pltpu.SMEM)],
        out_specs=pl.BlockSpec((None, n_off, GRID_W, kh * GRID_W), lambda h: (h, 0, 0, 0)),
        out_shape=jax.ShapeDtypeStruct((heads, n_off, GRID_W, kh * GRID_W), F32),
        compiler_params=_params(("arbitrary",), 4 << 20),
        name="na_bias",
    )(rpb.reshape(-1))


def _attn_kernel(q_ref, k_ref, v_ref, kc_ref, vc_ref, bias0_ref, bias1_ref, *rest, rows, kh, n_riders):
    o_ref = rest[n_riders]
    _convert_riders(rest[:n_riders], rest[n_riders + 1:])
    r2 = pl.program_id(1)
    nk = kh * GRID_W
    gw = NA_GROUP * HEAD_DIM
    lane = lax.broadcasted_iota(jnp.int32, (1, gw), 1)
    head_of_lane = [(lane >= e * HEAD_DIM) & (lane < (e + 1) * HEAD_DIM) for e in range(NA_GROUP)]
    nt = (((1,), (1,)), ((), ()))
    starts = [pl.multiple_of(jnp.clip(2 * r2 + i - kh // 2, 0, rows - kh) * GRID_W, GRID_W) for i in range(2)]
    bias_refs = (bias0_ref, bias1_ref)
    per_row = NA_GROUP * GRID_W
    zero = jnp.zeros((GRID_W, gw), BF16)
    for hg in range(q_ref.shape[1] // gw):
        cols = slice(hg * gw, (hg + 1) * gw)
        qp = q_ref[:, cols] * jnp.asarray(HEAD_DIM ** -0.5, BF16)
        qm = jnp.concatenate([jnp.where(head_of_lane[e], qp[i * GRID_W:(i + 1) * GRID_W], zero)
                              for i in range(2) for e in range(NA_GROUP)], axis=0)
        s_ctx = lax.dot_general(qm, kc_ref[:, cols], nt, preferred_element_type=F32)
        m_ctx = jnp.max(s_ctx, axis=-1, keepdims=True)
        s_lat, m_all = [], []
        for i in range(2):
            blk = slice(i * per_row, (i + 1) * per_row)
            bias = jnp.concatenate([bias_refs[i][hg * NA_GROUP + e] for e in range(NA_GROUP)], axis=0)
            s = lax.dot_general(qm[blk], k_ref[pl.ds(starts[i], nk), cols], nt, preferred_element_type=F32) + bias
            s_lat.append(s)
            m_all.append(jnp.maximum(jnp.max(s, axis=-1, keepdims=True), m_ctx[blk]))
        p_ctx = jnp.exp(s_ctx - jnp.concatenate(m_all, axis=0))
        o_ctx = jnp.dot(p_ctx.astype(BF16), vc_ref[:, cols], preferred_element_type=F32)
        l_ctx = jnp.sum(p_ctx, axis=-1, keepdims=True)
        for i in range(2):
            blk = slice(i * per_row, (i + 1) * per_row)
            p = jnp.exp(s_lat[i] - m_all[i])
            o = jnp.dot(p.astype(BF16), v_ref[pl.ds(starts[i], nk), cols], preferred_element_type=F32) + o_ctx[blk]
            o = o / (jnp.sum(p, axis=-1, keepdims=True) + l_ctx[blk])
            out = o[0:GRID_W]
            for e in range(1, NA_GROUP):
                out = jnp.where(head_of_lane[e], o[e * GRID_W:(e + 1) * GRID_W], out)
            o_ref[i * GRID_W:(i + 1) * GRID_W, cols] = out


def _attention(qkv, kv_ctx, bands, band_of_row, rows, kh, riders):
    bsz, seq, d3 = qkv.shape
    dn = d3 // 3
    ctx_len = kv_ctx.shape[1]
    heads, _, _, nk = bands.shape
    assert rows % 2 == 0
    n_steps = bsz * (rows // 2)
    slab_specs = _rider_specs(riders, n_steps, lambda b, r: b * (rows // 2) + r)
    est = (2 * (2 * seq * dn * 2 + 2 * ctx_len * dn * 2 + 2 * heads * GRID_W * nk * 4 + 2 * GRID_W * dn * 6)
           + _rider_bytes(riders, n_steps))
    band = lambda i: pl.BlockSpec((heads, None, GRID_W, nk), lambda b, r: (0, band_of_row(2 * r + i), 0, 0))
    return pl.pallas_call(
        functools.partial(_attn_kernel, rows=rows, kh=kh, n_riders=len(riders)),
        grid=(bsz, rows // 2),
        in_specs=[pl.BlockSpec((None, 2 * GRID_W, dn), lambda b, r: (b, r, 0)),
                  pl.BlockSpec((None, seq, dn), lambda b, r: (b, 0, 1)),
                  pl.BlockSpec((None, seq, dn), lambda b, r: (b, 0, 2)),
                  pl.BlockSpec((None, ctx_len, dn), lambda b, r: (b, 0, 0)),
                  pl.BlockSpec((None, ctx_len, dn), lambda b, r: (b, 0, 1)),
                  band(0), band(1)] + slab_specs,
        out_specs=[pl.BlockSpec((None, 2 * GRID_W, dn), lambda b, r: (b, r, 0))] + slab_specs,
        out_shape=[jax.ShapeDtypeStruct((bsz, seq, dn), F32)]
        + [jax.ShapeDtypeStruct(w.shape, BF16) for w in riders],
        compiler_params=_params(("arbitrary", "arbitrary"), est),
        name="na_attention",
    )(qkv, qkv, qkv, kv_ctx, kv_ctx, bands, bands, *riders)


def _rms(x, g):
    return x * lax.rsqrt(jnp.mean(x * x, axis=-1, keepdims=True) + LN_EPS) * g


def _layer_norm(x, g, b):
    xc = x - jnp.mean(x, axis=-1, keepdims=True)
    var = jnp.mean(xc * xc, axis=-1, keepdims=True)
    return xc * lax.rsqrt(var + LN_EPS) * g + b


def _merge_kernel(hy_ref, na_ref, x_ref, mod_ref, w_ref, ghy_ref, gna_ref, lg_ref, lb_ref, xo_ref, uo_ref, *, alpha):
    dh = hy_ref.shape[1]
    hy = _rms(hy_ref[...], ghy_ref[...]).astype(BF16)
    na = _rms(na_ref[...], gna_ref[...]).astype(BF16)
    y = (jnp.dot(hy, w_ref[0:dh, :], preferred_element_type=F32)
         + jnp.dot(na, w_ref[dh:, :], preferred_element_type=F32))
    xn = _layer_norm(alpha * x_ref[...] + mod_ref[2:3, :] * y, lg_ref[...], lb_ref[...])
    xo_ref[...] = xn
    uo_ref[...] = (xn * (1.0 + mod_ref[4:5, :]) + mod_ref[3:4, :]).astype(uo_ref.dtype)


def _merge(hy, na, x, mod, w_out, g_hy, g_na, ln_g, ln_b, alpha, tm):
    bsz, seq, d = x.shape
    dh, dn = hy.shape[2], na.shape[2]
    row = lambda a: a.reshape(1, -1)
    vec = lambda n: pl.BlockSpec((1, n), lambda b, i: (0, 0))
    tile = lambda n: pl.BlockSpec((None, tm, n), lambda b, i: (b, i, 0))
    est = d * d * 2 + 2 * tm * (dh + dn + 2 * d) * 4 + 2 * tm * d * 6 + 4 * tm * d * 4
    return pl.pallas_call(
        functools.partial(_merge_kernel, alpha=alpha),
        grid=(bsz, seq // tm),
        in_specs=[tile(dh), tile(dn), tile(d),
                  pl.BlockSpec((None, 6, d), lambda b, i: (b, 0, 0)),
                  pl.BlockSpec((d, d), lambda b, i: (0, 0)),
                  vec(dh), vec(dn), vec(d), vec(d)],
        out_specs=[tile(d), tile(d)],
        out_shape=[jax.ShapeDtypeStruct((bsz, seq, d), F32), jax.ShapeDtypeStruct((bsz, seq, d), BF16)],
        compiler_params=_params(("arbitrary", "arbitrary"), est),
        name="merge",
    )(hy, na, x, mod, w_out, row(g_hy), row(g_na), row(ln_g), row(ln_b))


def _first(a, b):
    return (a[0] > b[0]) | ((a[0] == b[0]) & (a[1] < b[1]))


def _order_pair(xs, i, j):
    a, b = xs[i], xs[j]
    f = _first(a, b)
    xs[i] = (jnp.maximum(a[0], b[0]),) + tuple(jnp.where(f, p, r) for p, r in zip(a[1:], b[1:]))
    xs[j] = (jnp.minimum(a[0], b[0]),) + tuple(jnp.where(f, r, p) for p, r in zip(a[1:], b[1:]))


def _bitonic_merge(xs):
    n = len(xs)
    j = n // 2
    while j >= 1:
        for i in range(n):
            if i & j == 0:
                _order_pair(xs, i, i | j)
        j //= 2
    return xs


def _sort(xs):
    n = len(xs)
    p = 1
    while p < n:
        k = p
        while k >= 1:
            for j in range(k % p, n - k, 2 * k):
                for i in range(min(k, n - j - k)):
                    if (i + j) // (2 * p) == (i + j + k) // (2 * p):
                        _order_pair(xs, i + j, i + j + k)
            k //= 2
        p *= 2
    return xs


def _leading_half(a, b):
    out = []
    for x, y in zip(a, reversed(b)):
        f = _first(x, y)
        out.append((jnp.maximum(x[0], y[0]),) + tuple(jnp.where(f, p, r) for p, r in zip(x[1:], y[1:])))
    return out


def _sorted_top(items, k):
    runs = [_sort(items[i:i + k]) for i in range(0, len(items), k)]
    while len(runs) > 1:
        runs = [_bitonic_merge(_leading_half(runs[i], runs[i + 1])) for i in range(0, len(runs), 2)]
    return runs[0]


def _pitch(rows):
    return rows + 8


def _route_kernel(u_ref, wq_ref, keys_ref, eu_ref, ev_ref, idx_ref, gate_ref, eub_ref, evb_ref, s_ref):
    eub_ref[...] = eu_ref[...].astype(eub_ref.dtype)
    evb_ref[...] = ev_ref[...].astype(evb_ref.dtype)
    nkeys, half = keys_ref.shape[1], keys_ref.shape[2]
    k = PEER_TOPK
    groups = u_ref.shape[0] // V7X_LANES
    pitch = _pitch(nkeys)
    tile = (groups, V7X_LANES)
    q = jnp.dot(u_ref[...], wq_ref[...], preferred_element_type=F32)
    nt = (((1,), (1,)), ((), ()))

    def split(v):
        hi = v.astype(BF16)
        return hi, (v - hi.astype(F32)).astype(BF16)

    for p in range(2):
        k_hi, k_lo = split(keys_ref[p])
        q_hi, q_lo = split(q[:, p * half:(p + 1) * half])
        dot = lambda a, b: lax.dot_general(a, b, nt, preferred_element_type=F32)
        s = dot(k_hi, q_hi) + (dot(k_hi, q_lo) + dot(k_lo, q_hi))
        for g in range(groups):
            s_ref[p, g * pitch:g * pitch + nkeys, :] = s[:, g * V7X_LANES:(g + 1) * V7X_LANES]

    def scores(p):
        return [(s_ref[p, pl.ds(n, groups, stride=pitch), :], jnp.full(tile, float(n), F32)) for n in range(nkeys)]

    top_a = _sorted_top(scores(0), k)
    top_b = _sorted_top(scores(1), k)

    def cand(i, j):
        return (top_a[i][0] + top_b[j][0], jnp.full(tile, float(i * k + j), F32),
                top_a[i][1] * float(nkeys) + top_b[j][1])

    pad = (jnp.full(tile, -jnp.inf, F32), jnp.full(tile, float(k * k), F32), jnp.zeros(tile, F32))
    run = lambda i: [cand(i, j) for j in range(k // (i + 1))]
    tail = [cand(i, 0) for i in range(k // 2, k)]
    assert k == 16
    r0 = run(0)
    r1 = _bitonic_merge(run(1) + tail[::-1])
    r2 = _sort(run(2) + run(3) + run(4) + run(5) + run(6))
    r3 = run(7) + [pad] * (k - 2)
    best = _leading_half(_bitonic_merge(_leading_half(r0, r1)), _bitonic_merge(_leading_half(r2, r3)))

    m = functools.reduce(jnp.maximum, [c[0] for c in best])
    e = [jnp.exp(c[0] - m) for c in best]
    inv = 1.0 / functools.reduce(jnp.add, e)
    for n in range(k):
        idx_ref[n] = best[n][2].astype(jnp.int32)
        gate_ref[n] = e[n] * inv


def _route(u, wq, keys, eu, ev, tm):
    t, d = u.shape
    heads, _, nkeys, half = keys.shape
    groups = tm // V7X_LANES
    n_steps = (t // tm) * heads
    n_exp, de = eu.shape
    slab = n_exp // n_steps
    assert slab * n_steps == n_exp and slab % 16 == 0
    est = (2 * (tm * d * 2 + d * 2 * half * 2 + 2 * nkeys * half * 4) + 8 * nkeys * tm * 4
           + 2 * 2 * slab * de * 6)
    out_spec = pl.BlockSpec((PEER_TOPK, groups, V7X_LANES), lambda i, h: (h, i, 0))
    slab_spec = pl.BlockSpec((slab, de), lambda i, h: (i * heads + h, 0))
    return pl.pallas_call(
        _route_kernel,
        grid=(t // tm, heads),
        in_specs=[pl.BlockSpec((tm, d), lambda i, h: (i, 0)),
                  pl.BlockSpec((d, 2 * half), lambda i, h: (0, h)),
                  pl.BlockSpec((None, 2, nkeys, half), lambda i, h: (h, 0, 0, 0)),
                  slab_spec, slab_spec],
        out_specs=[out_spec, out_spec, slab_spec, slab_spec],
        out_shape=[jax.ShapeDtypeStruct((heads * PEER_TOPK, t // V7X_LANES, V7X_LANES), jnp.int32),
                   jax.ShapeDtypeStruct((heads * PEER_TOPK, t // V7X_LANES, V7X_LANES), F32),
                   jax.ShapeDtypeStruct(eu.shape, BF16), jax.ShapeDtypeStruct(ev.shape, BF16)],
        scratch_shapes=[pltpu.VMEM((2, groups * _pitch(nkeys), V7X_LANES), F32)],
        compiler_params=_params(("arbitrary", "arbitrary"), est),
        name="peer_route",
    )(u, wq, keys, eu, ev)


def _gate_matrix_kernel(idx_ref, gate_ref, o_ref, w0_ref, w1_ref):
    step = pl.program_id(0)

    @pl.when(step == 0)
    def _():
        w1_ref[...] = jnp.zeros_like(w1_ref)

    @pl.when(step % 2 == 0)
    def _():
        _gate_matrix_step(idx_ref, gate_ref, o_ref, w0_ref, w1_ref)

    @pl.when(step % 2 == 1)
    def _():
        _gate_matrix_step(idx_ref, gate_ref, o_ref, w1_ref, w0_ref)


def _gate_matrix_step(idx_ref, gate_ref, o_ref, cur_ref, prev_ref):
    tw, nsel = idx_ref.shape
    pitch = _pitch(PEER_NKEYS)
    key = lax.broadcasted_iota(jnp.int32, (PEER_NKEYS, nsel), 0)
    nt = (((1,), (1,)), ((), ()))
    zero = jnp.zeros((PEER_NKEYS, nsel), BF16)
    n_pairs = tw // 2
    planes_per_pair = PEER_NKEYS // n_pairs
    for p in range(n_pairs):
        a_hot, b_hot = [], []
        for t in (2 * p, 2 * p + 1):
            idx = idx_ref[t:t + 1, :]
            gate = gate_ref[t:t + 1, :]
            a_hot.append(jnp.where((idx >> NKEYS_SHIFT) == key, gate, 0.0).astype(BF16))
            b_hot.append(jnp.where((idx & (PEER_NKEYS - 1)) == key, 1.0, 0.0).astype(BF16))
        lhs = jnp.concatenate(a_hot, axis=1)
        rhs = jnp.concatenate([jnp.concatenate([b_hot[0], zero], axis=1),
                               jnp.concatenate([zero, b_hot[1]], axis=1)], axis=0)
        planes = lax.dot_general(lhs, rhs, nt, preferred_element_type=F32)
        for k in range(2):
            row = (2 * p + k) * pitch
            cur_ref[row:row + PEER_NKEYS, :] = planes[:, k * PEER_NKEYS:(k + 1) * PEER_NKEYS]
        for a in range(p * planes_per_pair, (p + 1) * planes_per_pair):
            o_ref[:, a * PEER_NKEYS:(a + 1) * PEER_NKEYS] = (
                prev_ref[pl.ds(a, tw, stride=pitch), :].astype(o_ref.dtype))


def _gate_matrix(idx, gate, tw):
    t, nsel = idx.shape
    n_exp = PEER_NKEYS * PEER_NKEYS
    n_blocks = t // tw
    assert PEER_NKEYS % (tw // 2) == 0
    est = 2 * (2 * tw * nsel * 4 + tw * n_exp * 2) + 2 * tw * _pitch(PEER_NKEYS) * PEER_NKEYS * 4
    block_in = pl.BlockSpec((tw, nsel), lambda i: (jnp.minimum(i, n_blocks - 1), 0))
    return pl.pallas_call(
        _gate_matrix_kernel,
        grid=(n_blocks + 1,),
        in_specs=[block_in, block_in],
        out_specs=pl.BlockSpec((tw, n_exp), lambda i: (jnp.maximum(i - 1, 0), 0)),
        out_shape=jax.ShapeDtypeStruct((t, n_exp), BF16),
        scratch_shapes=[pltpu.VMEM((tw * _pitch(PEER_NKEYS), PEER_NKEYS), F32)] * 2,
        compiler_params=_params(("arbitrary",), est),
        name="peer_gate_matrix",
    )(idx, gate)


def _peer_kernel(u_ref, eu_ref, ev_ref, w_ref, o_ref, *, n_chunk):
    @pl.when(pl.program_id(1) == 0)
    def _():
        o_ref[...] = jnp.zeros_like(o_ref)

    z = lax.dot_general(u_ref[...], eu_ref[...], (((1,), (1,)), ((), ())), preferred_element_type=F32)
    act = 0.5 * z * (1.0 + lax.erf(z * (2.0 ** -0.5)))
    h = (w_ref[...].astype(F32) * act).astype(BF16)
    for c in range(0, o_ref.shape[1], n_chunk):
        o_ref[:, c:c + n_chunk] += jnp.dot(h, ev_ref[:, c:c + n_chunk], preferred_element_type=F32)


def _final_ln_kernel(x_ref, f_ref, mod_ref, lg_ref, lb_ref, o_ref, *, alpha):
    r = alpha * x_ref[...] + mod_ref[5:6, :] * f_ref[...]
    o_ref[...] = _layer_norm(r, lg_ref[...], lb_ref[...])


def _final_ln(x, f, mod, ln_g, ln_b, alpha, seq, tm):
    t, d = x.shape
    tile = pl.BlockSpec((tm, d), lambda i: (i, 0))
    vec = pl.BlockSpec((1, d), lambda i: (0, 0))
    return pl.pallas_call(
        functools.partial(_final_ln_kernel, alpha=alpha),
        grid=(t // tm,),
        in_specs=[tile, tile, pl.BlockSpec((None, 6, d), lambda i: ((i * tm) // seq, 0, 0)), vec, vec],
        out_specs=tile,
        out_shape=jax.ShapeDtypeStruct((t, d), F32),
        compiler_params=_params(("arbitrary",), 8 * tm * d * 4),
        name="final_ln",
    )(x, f, mod, ln_g.reshape(1, -1), ln_b.reshape(1, -1))


def _peer(u, eu, ev, w, tm, te, n_chunk):
    t, d = u.shape
    n_exp = ev.shape[0]
    once = pl.Buffered(1)
    est = tm * d * 2 + tm * d * 4 + 2 * (2 * te * d * 2 + tm * te * 2) + 4 * tm * te * 4 + tm * n_chunk * 4
    return pl.pallas_call(
        functools.partial(_peer_kernel, n_chunk=n_chunk),
        grid=(t // tm, n_exp // te),
        in_specs=[pl.BlockSpec((tm, d), lambda i, j: (i, 0), pipeline_mode=once),
                  pl.BlockSpec((te, d), lambda i, j: (j, 0)),
                  pl.BlockSpec((te, d), lambda i, j: (j, 0)),
                  pl.BlockSpec((tm, te), lambda i, j: (i, j))],
        out_specs=pl.BlockSpec((tm, d), lambda i, j: (i, 0), pipeline_mode=once),
        out_shape=jax.ShapeDtypeStruct((t, d), F32),
        compiler_params=_params(("arbitrary", "arbitrary"), est),
        name="peer_experts",
    )(u, eu, ev, w)


def _filter_features(seq):
    t = jnp.linspace(0.0, 1.0, seq, dtype=F32)[:, None]
    bands = (FILTER_EMB - 1) // 2
    w = 2.0 * math.pi * jnp.arange(seq, dtype=F32)[:, None] / seq
    f = jnp.linspace(1e-4, bands - 1, bands, dtype=F32)[None, :]
    z = jnp.concatenate([t, jnp.cos(f * w), -jnp.sin(f * w)], axis=-1)
    return jnp.pad(z, ((0, 0), (0, FILTER_PAD - FILTER_EMB)))


@functools.lru_cache(maxsize=None)
def _dft_matrices(seq):
    n2 = 2 * seq
    f = np.arange(seq, dtype=np.int64)[:, None]
    n = np.arange(seq, dtype=np.int64)[None, :]
    ang = ((f * n) % n2).astype(np.float64) * (2.0 * math.pi / n2)
    sin = np.where(f == 0, (1 - 2 * (n % 2)).astype(np.float64), np.sin(ang))
    return np.concatenate([np.cos(ang), sin], axis=0).astype(np.float32)


def kernel(x, c, ctx, c_ctx, w_ada, b_ada, w_in, conv_w, conv_b, filt_w1, filt_b1, filt_w2, filt_b2,
           filt_w3, filt_b3, filt_w4, filt_freq, hy_skip, na_rpb, hy_norm_g, na_norm_g, w_out,
           ln1_g, ln1_b, peer_wq, peer_keys, peer_u, peer_v, ln2_g, ln2_b):
    depth = w_ada.shape[0]
    assert depth == 1, "single-layer block: the context stream is never updated"
    bsz, seq, d = x.shape
    rows = seq // GRID_W
    kh = min(NA_KH, rows)
    assert kh == NA_KH and seq % GRID_W == 0
    alpha = (2.0 * depth) ** 0.25
    d_na = d - D_HY
    off_q = 3 * D_HY
    off_k = off_q + d_na

    n_cond = 8
    cond = jnp.concatenate([c, c_ctx[None], jnp.zeros((n_cond - bsz - 1, d), F32)], axis=0)
    mod = _ada(cond, w_ada[0], b_ada[0]).reshape(n_cond, 6, d)

    w_cols = lambda lo, hi: w_in[0][:, lo:hi].astype(BF16)
    (z_hy,) = _proj(x, mod, lambda b: b, w_cols(0, off_q), F32, PROJ_TM)
    qkv, ffwd = _proj(x, mod, lambda b: b, w_cols(off_q, None), BF16, PROJ_TM, [_dft_matrices(seq)])
    (kv_ctx,) = _proj(ctx, mod, lambda b: bsz, w_cols(off_k, None), BF16, CTX_TM)

    deltas = jnp.abs(jnp.linspace(math.log(DECAY_TARGET) / SLOW_DECAY_PCT,
                                  math.log(DECAY_TARGET) / FAST_DECAY_PCT, D_HY, dtype=F32))[None, :]
    padw = lambda a: jnp.pad(a, ((0, FILTER_PAD - a.shape[0]), (0, FILTER_PAD - a.shape[1])))
    padv = lambda a: jnp.pad(a, ((0, FILTER_PAD - a.shape[0]),))
    w4 = jnp.pad(filt_w4[0], ((0, FILTER_PAD - filt_w4.shape[1]), (0, 0)))
    freq = jnp.pad(filt_freq[0], ((0, 0), (0, FILTER_PAD - filt_freq.shape[2])))
    ksum, kdiff = _hyena_filters(_filter_features(seq), padw(filt_w1[0]), padv(filt_b1[0]), padw(filt_w2[0]),
                                 padv(filt_b2[0]), padw(filt_w3[0]), padv(filt_b3[0]), w4, freq, deltas, HYENA_TC)
    finv = ffwd.T
    kp, kq = _spectrum(ffwd, ksum, kdiff, HYENA_TC)
    hy_out = _hyena(z_hy, conv_w[0], conv_b[0], ffwd, finv, kp, kq, hy_skip[0], HYENA_TC)

    bands = _bias_bands(na_rpb[0], kh)
    band_of_row = lambda r: jnp.clip(r - kh // 2, 0, rows - kh) - r + (NA_KH - 1)
    na_out, w_out_b, wq_b = _attention(qkv, kv_ctx, bands, band_of_row, rows, kh, [w_out[0], peer_wq[0]])

    x_mid, u_peer = _merge(hy_out, na_out, x, mod, w_out_b, hy_norm_g[0], na_norm_g[0],
                           ln1_g[0], ln1_b[0], alpha, MERGE_TM)

    t = bsz * seq
    u2 = u_peer.reshape(t, d)
    idx_t, gate_t, eu_b, ev_b = _route(u2, wq_b, peer_keys[0], peer_u[0], peer_v[0], ROUTE_TM)
    w_gate = _gate_matrix(idx_t.reshape(-1, t).T, gate_t.reshape(-1, t).T, GATE_TW)
    f = _peer(u2, eu_b, ev_b, w_gate, PEER_TM, PEER_TE, PEER_CHUNK)
    out = _final_ln(x_mid.reshape(t, d), f, mod, ln2_g[0], ln2_b[0], alpha, seq, FINAL_TM)
    return out.reshape(bsz, seq, d)
```

```python
import functools
import math

import numpy as np
import jax
import jax.numpy as jnp
from jax import lax
from jax.experimental import pallas as pl
from jax.experimental.pallas import tpu as pltpu

F32 = jnp.float32
BF16 = jnp.bfloat16
HIGHEST = lax.Precision.HIGHEST

GRID_W = 64
HEAD_DIM = 64
D_HY = 1024
HY_ORDER = 2
FILTER_EMB = 33
FILTER_PAD = 128
DECAY_TARGET = 1e-2
FAST_DECAY_PCT = 0.3
SLOW_DECAY_PCT = 1.5
NA_KH = 8
NA_KW = 16
PEER_HEADS = 8
PEER_NKEYS = 128
NKEYS_SHIFT = PEER_NKEYS.bit_length() - 1
assert 1 << NKEYS_SHIFT == PEER_NKEYS
PEER_TOPK = 16
LN_EPS = 1e-5
NEG_INF = -1e30

V7X_VMEM_BYTES = 64 * 1024 * 1024
V7X_LANES = 128
V7X_MXU_DEPTH = 256
NA_GROUP = V7X_MXU_DEPTH // HEAD_DIM


MIB = 1024 * 1024
VMEM_TEMPORARIES = 12 * MIB
VMEM_RESERVED = 6 * MIB

ADA_TN = 2048
PROJ_TM = 512
CTX_TM = 256
HYENA_TC = 256
MERGE_TM = 256
ROUTE_TM = 1024
GATE_TW = 64
PEER_TM, PEER_TE, PEER_CHUNK = 2048, 512, 512
FINAL_TM = 512


def _vmem_limit(estimate_bytes):
    return int(min(estimate_bytes + VMEM_TEMPORARIES, V7X_VMEM_BYTES - VMEM_RESERVED))


def _params(semantics, estimate_bytes):
    return pltpu.CompilerParams(dimension_semantics=semantics, vmem_limit_bytes=_vmem_limit(estimate_bytes))


def _ada_kernel(cond_ref, w_ref, b_ref, o_ref):
    s = jax.nn.silu(cond_ref[...])
    o_ref[...] = jnp.dot(s, w_ref[...], precision=HIGHEST, preferred_element_type=F32) + b_ref[...]


def _ada(cond, w, b):
    n, d = cond.shape
    tn = ADA_TN
    return pl.pallas_call(
        _ada_kernel,
        grid=(w.shape[1] // tn,),
        in_specs=[pl.BlockSpec((n, d), lambda j: (0, 0)),
                  pl.BlockSpec((d, tn), lambda j: (0, j)),
                  pl.BlockSpec((1, tn), lambda j: (0, j))],
        out_specs=pl.BlockSpec((n, tn), lambda j: (0, j)),
        out_shape=jax.ShapeDtypeStruct((n, w.shape[1]), F32),
        compiler_params=_params(("arbitrary",), 2 * d * tn * 4),
        name="ada",
    )(cond, w, b.reshape(1, -1))


def _rider_specs(riders, n_steps, step_of):
    specs = []
    for w in riders:
        slab = w.shape[0] // n_steps
        assert slab * n_steps == w.shape[0] and slab % 16 == 0
        specs.append(pl.BlockSpec((slab, w.shape[1]), lambda *g: (step_of(*g), 0)))
    return specs


def _rider_bytes(riders, n_steps):
    return sum(2 * (w.size // n_steps) * 6 for w in riders)


def _convert_riders(src_refs, dst_refs):
    for src_ref, dst_ref in zip(src_refs, dst_refs):
        dst_ref[...] = src_ref[...].astype(dst_ref.dtype)


def _proj_kernel(x_ref, mod_ref, w_ref, *rest, n_riders):
    o_ref = rest[n_riders]
    _convert_riders(rest[:n_riders], rest[n_riders + 1:])
    u = x_ref[...] * (1.0 + mod_ref[1:2, :]) + mod_ref[0:1, :]
    o_ref[...] = jnp.dot(u.astype(BF16), w_ref[...], preferred_element_type=F32).astype(o_ref.dtype)


def _proj(x, mod, mod_row, w, n, col_block, out_dtype, tm, riders=()):
    bsz, seq, d = x.shape
    n_i = seq // tm
    slabs = _rider_specs(riders, bsz * n_i, lambda b, i: b * n_i + i)
    est = d * n * 2 + 2 * tm * d * 4 + 2 * tm * n * 4 + tm * n * 4 + _rider_bytes(riders, bsz * n_i)
    return pl.pallas_call(
        functools.partial(_proj_kernel, n_riders=len(riders)),
        grid=(bsz, n_i),
        in_specs=[pl.BlockSpec((None, tm, d), lambda b, i: (b, i, 0)),
                  pl.BlockSpec((None, 6, d), lambda b, i: (mod_row(b), 0, 0)),
                  pl.BlockSpec((d, n), lambda b, i: (0, col_block), pipeline_mode=pl.Buffered(1))] + slabs,
        out_specs=[pl.BlockSpec((None, tm, n), lambda b, i: (b, i, 0))] + slabs,
        out_shape=[jax.ShapeDtypeStruct((bsz, seq, n), out_dtype)]
        + [jax.ShapeDtypeStruct(r.shape, BF16) for r in riders],
        compiler_params=_params(("arbitrary", "arbitrary"), est),
        name="proj",
    )(x, mod, w, *riders)


def _filter_kernel(z_ref, w1_ref, b1_ref, w2_ref, b2_ref, w3_ref, b3_ref, fr_ref, w4f_ref, w4b_ref, dl_ref,
                   ks_ref, kd_ref, h_ref):
    dot = functools.partial(jnp.dot, precision=HIGHEST, preferred_element_type=F32)

    @pl.when((pl.program_id(0) == 0) & (pl.program_id(1) == 0))
    def _():
        h = jnp.sin(fr_ref[0:1, :] * (dot(z_ref[...], w1_ref[...]) + b1_ref[...]))
        h = jnp.sin(fr_ref[1:2, :] * (dot(h, w2_ref[...]) + b2_ref[...]))
        h_ref[...] = jnp.sin(fr_ref[2:3, :] * (dot(h, w3_ref[...]) + b3_ref[...]))

    h = h_ref[...]
    decay = jnp.exp(-z_ref[:, 0:1] * dl_ref[...])
    hf = dot(h, w4f_ref[...]) * decay
    hb = dot(h, w4b_ref[...]) * decay
    norm = jnp.sum(jnp.abs(hf) + jnp.abs(hb), axis=0, keepdims=True)
    hf = hf / norm
    hb = hb / norm
    row = lax.broadcasted_iota(jnp.int32, hb.shape, 0)
    hb = jnp.where(row == 0, 0.0, hb)
    ks_ref[...] = hf + hb
    kd_ref[...] = hb - hf


def _hyena_filters(zfeat, w1, b1, w2, b2, w3, b3, w4, freq, deltas, tc):
    seq = zfeat.shape[0]
    fo = w2.shape[0]
    nct = D_HY // tc
    full = lambda a: pl.BlockSpec(a.shape, lambda o, j: (0,) * a.ndim)
    args = (zfeat, w1, b1.reshape(1, -1), w2, b2.reshape(1, -1), w3, b3.reshape(1, -1), freq)
    out_spec = pl.BlockSpec((None, seq, tc), lambda o, j: (o, 0, j))
    return pl.pallas_call(
        _filter_kernel,
        grid=(HY_ORDER, nct),
        in_specs=[full(a) for a in args] + [
            pl.BlockSpec((fo, tc), lambda o, j: (0, (2 * o) * nct + j)),
            pl.BlockSpec((fo, tc), lambda o, j: (0, (2 * o + 1) * nct + j)),
            pl.BlockSpec((1, tc), lambda o, j: (0, j))],
        out_specs=[out_spec, out_spec],
        out_shape=[jax.ShapeDtypeStruct((HY_ORDER, seq, D_HY), F32)] * 2,
        scratch_shapes=[pltpu.VMEM((seq, fo), F32)],
        compiler_params=_params(("arbitrary", "arbitrary"), 12 * seq * tc * 4),
        name="hyena_filter",
    )(*args, w4, w4, deltas)


def _spectrum_kernel(ff_ref, ks_ref, kd_ref, kp_ref, kq_ref):
    seq = ks_ref.shape[0]
    ks = ks_ref[...].astype(BF16)
    kd = kd_ref[...].astype(BF16)
    p = jnp.dot(ff_ref[0:seq, :], ks, preferred_element_type=F32)
    q = jnp.dot(ff_ref[seq:2 * seq, :], kd, preferred_element_type=F32)
    nyq = jnp.dot(ff_ref[seq:seq + 16, :], ks, preferred_element_type=F32)[0:1, :]
    row = lax.broadcasted_iota(jnp.int32, p.shape, 0)
    wf = jnp.where(row == 0, 0.5 / seq, 1.0 / seq)
    kp_ref[...] = p * wf
    kq_ref[...] = jnp.where(row == 0, nyq, q) * wf


def _spectrum(ffwd, ksum, kdiff, tc):
    seq = ksum.shape[1]
    spec = pl.BlockSpec((None, seq, tc), lambda o, j: (o, 0, j))
    return pl.pallas_call(
        _spectrum_kernel,
        grid=(HY_ORDER, D_HY // tc),
        in_specs=[pl.BlockSpec(ffwd.shape, lambda o, j: (0, 0)), spec, spec],
        out_specs=[spec, spec],
        out_shape=[jax.ShapeDtypeStruct(ksum.shape, F32)] * 2,
        compiler_params=_params(("arbitrary", "arbitrary"), ffwd.size * 2 + 12 * seq * tc * 4),
        name="hyena_spectrum",
    )(ffwd, ksum, kdiff)


def _short_conv(z, w, b):
    seq = z.shape[0]
    row = lax.broadcasted_iota(jnp.int32, z.shape, 0)
    prev = jnp.where(row == 0, 0.0, pltpu.roll(z, 1, 0))
    nxt = jnp.where(row == seq - 1, 0.0, pltpu.roll(z, seq - 1, 0))
    return prev * w[0:1, :] + z * w[1:2, :] + nxt * w[2:3, :] + b


def _dft_mul(u, ff_ref, kp_ref, kq_ref, o_ref):
    seq = u.shape[0]
    a = jnp.dot(ff_ref[...], u.astype(BF16), preferred_element_type=F32)
    p, q = a[0:seq], a[seq:2 * seq]
    kp, kq = kp_ref[...], kq_ref[...]
    row0 = lax.broadcasted_iota(jnp.int32, p.shape, 0) == 0
    o_ref[0:seq, :] = jnp.where(row0, kp * p, kp * p + kq * q).astype(o_ref.dtype)
    o_ref[seq:2 * seq, :] = jnp.where(row0, kq * q, kp * q - kq * p).astype(o_ref.dtype)


def _fwd_conv_kernel(z_ref, cw_ref, cb_ref, ff_ref, kp_ref, kq_ref, o_ref):
    _dft_mul(_short_conv(z_ref[...], cw_ref[...], cb_ref[...]), ff_ref, kp_ref, kq_ref, o_ref)


def _fwd_plain_kernel(u_ref, ff_ref, kp_ref, kq_ref, o_ref):
    _dft_mul(u_ref[...], ff_ref, kp_ref, kq_ref, o_ref)


def _inv_first_kernel(y_ref, fi_ref, zv_ref, zx_ref, cwv_ref, cbv_ref, cwx_ref, cbx_ref, skip_ref, o_ref):
    v = _short_conv(zv_ref[...], cwv_ref[...], cbv_ref[...])
    x1 = _short_conv(zx_ref[...], cwx_ref[...], cbx_ref[...])
    y = jnp.dot(fi_ref[...], y_ref[...], preferred_element_type=F32)
    o_ref[...] = x1 * (y + v * skip_ref[...])


def _inv_second_kernel(y_ref, fi_ref, g_ref, zx_ref, cwx_ref, cbx_ref, skip_ref, o_ref):
    x2 = _short_conv(zx_ref[...], cwx_ref[...], cbx_ref[...])
    y = jnp.dot(fi_ref[...], y_ref[...], preferred_element_type=F32)
    o_ref[...] = x2 * (y + g_ref[...] * skip_ref[...])


def _hyena(z, conv_w, conv_b, ffwd, finv, kp, kq, skip, tc):
    bsz, seq, _ = z.shape
    nct = D_HY // tc
    sem = ("arbitrary", "arbitrary")
    zcol = lambda g: pl.BlockSpec((None, seq, tc), lambda b, j: (b, 0, g * nct + j))
    wcol = lambda g: pl.BlockSpec((3, tc), lambda b, j: (0, g * nct + j))
    bcol = lambda g: pl.BlockSpec((1, tc), lambda b, j: (0, g * nct + j))
    act = pl.BlockSpec((None, seq, tc), lambda b, j: (b, 0, j))
    spec2 = pl.BlockSpec((None, 2 * seq, tc), lambda b, j: (b, 0, j))
    kspec = lambda o: pl.BlockSpec((None, seq, tc), lambda b, j: (o, 0, j))
    sspec = lambda o: pl.BlockSpec((None, 1, tc), lambda b, j: (o, 0, j))
    mat = pl.BlockSpec(ffwd.shape, lambda b, j: (0, 0))
    mati = pl.BlockSpec(finv.shape, lambda b, j: (0, 0))
    est = ffwd.size * 2 + 14 * seq * tc * 4
    cb = conv_b.reshape(1, -1)
    skip3 = skip.reshape(HY_ORDER, 1, D_HY)
    spec_shape = jax.ShapeDtypeStruct((bsz, 2 * seq, D_HY), BF16)
    act_shape = jax.ShapeDtypeStruct((bsz, seq, D_HY), F32)

    y1 = pl.pallas_call(
        _fwd_conv_kernel, grid=(bsz, nct),
        in_specs=[zcol(0), wcol(0), bcol(0), mat, kspec(0), kspec(0)],
        out_specs=spec2, out_shape=spec_shape, compiler_params=_params(sem, est), name="hyena_fwd1",
    )(z, conv_w, cb, ffwd, kp, kq)
    g = pl.pallas_call(
        _inv_first_kernel, grid=(bsz, nct),
        in_specs=[spec2, mati, zcol(0), zcol(1), wcol(0), bcol(0), wcol(1), bcol(1), sspec(0)],
        out_specs=act, out_shape=act_shape, compiler_params=_params(sem, est), name="hyena_inv1",
    )(y1, finv, z, z, conv_w, cb, conv_w, cb, skip3)
    y2 = pl.pallas_call(
        _fwd_plain_kernel, grid=(bsz, nct),
        in_specs=[act, mat, kspec(1), kspec(1)],
        out_specs=spec2, out_shape=spec_shape, compiler_params=_params(sem, est), name="hyena_fwd2",
    )(g, ffwd, kp, kq)
    return pl.pallas_call(
        _inv_second_kernel, grid=(bsz, nct),
        in_specs=[spec2, mati, act, zcol(2), wcol(2), bcol(2), sspec(1)],
        out_specs=act, out_shape=act_shape, compiler_params=_params(sem, est), name="hyena_inv2",
    )(y2, finv, g, z, conv_w, cb, skip3)


def _bias_kernel(rpb_ref, o_ref, *, kh):
    h = pl.program_id(0)
    n_r, n_c = 2 * NA_KH - 1, 2 * NA_KW - 1
    cq = lax.broadcasted_iota(jnp.int32, (GRID_W, GRID_W), 0)
    ck = lax.broadcasted_iota(jnp.int32, (GRID_W, GRID_W), 1)
    rel = ck - jnp.clip(cq - NA_KW // 2, 0, GRID_W - NA_KW)
    ok = (rel >= 0) & (rel < NA_KW)
    dcol = jnp.clip(ck - cq + NA_KW - 1, 0, n_c - 1)
    tables = []
    for r in range(n_r):
        acc = jnp.zeros((GRID_W, GRID_W), F32)
        for c in range(n_c):
            acc = jnp.where(dcol == c, rpb_ref[(h * n_r + r) * n_c + c], acc)
        tables.append(jnp.where(ok, acc, NEG_INF))
    for off in range(o_ref.shape[0]):
        for i in range(kh):
            o_ref[off, :, i * GRID_W:(i + 1) * GRID_W] = tables[off + i]


def _bias_bands(rpb, kh):
    heads = rpb.shape[0]
    n_off = 2 * NA_KH - kh
    return pl.pallas_call(
        functools.partial(_bias_kernel, kh=kh),
        grid=(heads,),
        in_specs=[pl.BlockSpec(memory_space=pltpu.SMEM)],
        out_specs=pl.BlockSpec((None, n_off, GRID_W, kh * GRID_W), lambda h: (h, 0, 0, 0)),
        out_shape=jax.ShapeDtypeStruct((heads, n_off, GRID_W, kh * GRID_W), F32),
        compiler_params=_params(("arbitrary",), 4 << 20),
        name="na_bias",
    )(rpb.reshape(-1))


def _attn_kernel(q_ref, k_ref, v_ref, kc_ref, vc_ref, bias0_ref, bias1_ref, *rest, rows, kh, n_riders):
    o_ref = rest[n_riders]
    _convert_riders(rest[:n_riders], rest[n_riders + 1:])
    r2 = pl.program_id(1)
    nk = kh * GRID_W
    gw = NA_GROUP * HEAD_DIM
    lane = lax.broadcasted_iota(jnp.int32, (1, gw), 1)
    head_of_lane = [(lane >= e * HEAD_DIM) & (lane < (e + 1) * HEAD_DIM) for e in range(NA_GROUP)]
    nt = (((1,), (1,)), ((), ()))
    starts = [pl.multiple_of(jnp.clip(2 * r2 + i - kh // 2, 0, rows - kh) * GRID_W, GRID_W) for i in range(2)]
    bias_refs = (bias0_ref, bias1_ref)
    per_row = NA_GROUP * GRID_W
    zero = jnp.zeros((GRID_W, gw), BF16)
    for hg in range(q_ref.shape[1] // gw):
        cols = slice(hg * gw, (hg + 1) * gw)
        qp = q_ref[:, cols] * jnp.asarray(HEAD_DIM ** -0.5, BF16)
        qm = jnp.concatenate([jnp.where(head_of_lane[e], qp[i * GRID_W:(i + 1) * GRID_W], zero)
                              for i in range(2) for e in range(NA_GROUP)], axis=0)
        s_ctx = lax.dot_general(qm, kc_ref[:, cols], nt, preferred_element_type=F32)
        m_ctx = jnp.max(s_ctx, axis=-1, keepdims=True)
        s_lat, m_all = [], []
        for i in range(2):
            blk = slice(i * per_row, (i + 1) * per_row)
            bias = jnp.concatenate([bias_refs[i][hg * NA_GROUP + e] for e in range(NA_GROUP)], axis=0)
            s = lax.dot_general(qm[blk], k_ref[pl.ds(starts[i], nk), cols], nt, preferred_element_type=F32) + bias
            s_lat.append(s)
            m_all.append(jnp.maximum(jnp.max(s, axis=-1, keepdims=True), m_ctx[blk]))
        p_ctx = jnp.exp(s_ctx - jnp.concatenate(m_all, axis=0))
        o_ctx = jnp.dot(p_ctx.astype(BF16), vc_ref[:, cols], preferred_element_type=F32)
        l_ctx = jnp.sum(p_ctx, axis=-1, keepdims=True)
        for i in range(2):
            blk = slice(i * per_row, (i + 1) * per_row)
            p = jnp.exp(s_lat[i] - m_all[i])
            o = jnp.dot(p.astype(BF16), v_ref[pl.ds(starts[i], nk), cols], preferred_element_type=F32) + o_ctx[blk]
            o = o / (jnp.sum(p, axis=-1, keepdims=True) + l_ctx[blk])
            out = o[0:GRID_W]
            for e in range(1, NA_GROUP):
                out = jnp.where(head_of_lane[e], o[e * GRID_W:(e + 1) * GRID_W], out)
            o_ref[i * GRID_W:(i + 1) * GRID_W, cols] = out


def _attention(qkv, kv_ctx, bands, band_of_row, rows, kh, riders):
    bsz, seq, d3 = qkv.shape
    dn = d3 // 3
    ctx_len = kv_ctx.shape[1]
    heads, _, _, nk = bands.shape
    assert rows % 2 == 0
    n_steps = bsz * (rows // 2)
    slab_specs = _rider_specs(riders, n_steps, lambda b, r: b * (rows // 2) + r)
    est = (2 * (2 * seq * dn * 2 + 2 * ctx_len * dn * 2 + 2 * heads * GRID_W * nk * 4 + 2 * GRID_W * dn * 6)
           + _rider_bytes(riders, n_steps))
    band = lambda i: pl.BlockSpec((heads, None, GRID_W, nk), lambda b, r: (0, band_of_row(2 * r + i), 0, 0))
    return pl.pallas_call(
        functools.partial(_attn_kernel, rows=rows, kh=kh, n_riders=len(riders)),
        grid=(bsz, rows // 2),
        in_specs=[pl.BlockSpec((None, 2 * GRID_W, dn), lambda b, r: (b, r, 0)),
                  pl.BlockSpec((None, seq, dn), lambda b, r: (b, 0, 1)),
                  pl.BlockSpec((None, seq, dn), lambda b, r: (b, 0, 2)),
                  pl.BlockSpec((None, ctx_len, dn), lambda b, r: (b, 0, 0)),
                  pl.BlockSpec((None, ctx_len, dn), lambda b, r: (b, 0, 1)),
                  band(0), band(1)] + slab_specs,
        out_specs=[pl.BlockSpec((None, 2 * GRID_W, dn), lambda b, r: (b, r, 0))] + slab_specs,
        out_shape=[jax.ShapeDtypeStruct((bsz, seq, dn), F32)]
        + [jax.ShapeDtypeStruct(w.shape, BF16) for w in riders],
        compiler_params=_params(("arbitrary", "arbitrary"), est),
        name="na_attention",
    )(qkv, qkv, qkv, kv_ctx, kv_ctx, bands, bands, *riders)


def _rms(x, g):
    return x * lax.rsqrt(jnp.mean(x * x, axis=-1, keepdims=True) + LN_EPS) * g


def _layer_norm(x, g, b):
    xc = x - jnp.mean(x, axis=-1, keepdims=True)
    var = jnp.mean(xc * xc, axis=-1, keepdims=True)
    return xc * lax.rsqrt(var + LN_EPS) * g + b


def _merge_kernel(hy_ref, na_ref, x_ref, mod_ref, w_ref, ghy_ref, gna_ref, lg_ref, lb_ref, xo_ref, uo_ref, *, alpha):
    dh = hy_ref.shape[1]
    hy = _rms(hy_ref[...], ghy_ref[...]).astype(BF16)
    na = _rms(na_ref[...], gna_ref[...]).astype(BF16)
    y = (jnp.dot(hy, w_ref[0:dh, :], preferred_element_type=F32)
         + jnp.dot(na, w_ref[dh:, :], preferred_element_type=F32))
    xn = _layer_norm(alpha * x_ref[...] + mod_ref[2:3, :] * y, lg_ref[...], lb_ref[...])
    xo_ref[...] = xn
    uo_ref[...] = (xn * (1.0 + mod_ref[4:5, :]) + mod_ref[3:4, :]).astype(uo_ref.dtype)


def _merge(hy, na, x, mod, w_out, g_hy, g_na, ln_g, ln_b, alpha, tm):
    bsz, seq, d = x.shape
    dh, dn = hy.shape[2], na.shape[2]
    row = lambda a: a.reshape(1, -1)
    vec = lambda n: pl.BlockSpec((1, n), lambda b, i: (0, 0))
    tile = lambda n: pl.BlockSpec((None, tm, n), lambda b, i: (b, i, 0))
    est = d * d * 2 + 2 * tm * (dh + dn + 2 * d) * 4 + 2 * tm * d * 6 + 4 * tm * d * 4
    return pl.pallas_call(
        functools.partial(_merge_kernel, alpha=alpha),
        grid=(bsz, seq // tm),
        in_specs=[tile(dh), tile(dn), tile(d),
                  pl.BlockSpec((None, 6, d), lambda b, i: (b, 0, 0)),
                  pl.BlockSpec((d, d), lambda b, i: (0, 0)),
                  vec(dh), vec(dn), vec(d), vec(d)],
        out_specs=[tile(d), tile(d)],
        out_shape=[jax.ShapeDtypeStruct((bsz, seq, d), F32), jax.ShapeDtypeStruct((bsz, seq, d), BF16)],
        compiler_params=_params(("arbitrary", "arbitrary"), est),
        name="merge",
    )(hy, na, x, mod, w_out, row(g_hy), row(g_na), row(ln_g), row(ln_b))


def _first(a, b):
    return (a[0] > b[0]) | ((a[0] == b[0]) & (a[1] < b[1]))


def _order_pair(xs, i, j):
    a, b = xs[i], xs[j]
    f = _first(a, b)
    xs[i] = (jnp.maximum(a[0], b[0]),) + tuple(jnp.where(f, p, r) for p, r in zip(a[1:], b[1:]))
    xs[j] = (jnp.minimum(a[0], b[0]),) + tuple(jnp.where(f, r, p) for p, r in zip(a[1:], b[1:]))


def _bitonic_merge(xs):
    n = len(xs)
    j = n // 2
    while j >= 1:
        for i in range(n):
            if i & j == 0:
                _order_pair(xs, i, i | j)
        j //= 2
    return xs


def _sort(xs):
    n = len(xs)
    p = 1
    while p < n:
        k = p
        while k >= 1:
            for j in range(k % p, n - k, 2 * k):
                for i in range(min(k, n - j - k)):
                    if (i + j) // (2 * p) == (i + j + k) // (2 * p):
                        _order_pair(xs, i + j, i + j + k)
            k //= 2
        p *= 2
    return xs


def _leading_half(a, b):
    out = []
    for x, y in zip(a, reversed(b)):
        f = _first(x, y)
        out.append((jnp.maximum(x[0], y[0]),) + tuple(jnp.where(f, p, r) for p, r in zip(x[1:], y[1:])))
    return out


def _sorted_top(items, k):
    runs = [_sort(items[i:i + k]) for i in range(0, len(items), k)]
    while len(runs) > 1:
        runs = [_bitonic_merge(_leading_half(runs[i], runs[i + 1])) for i in range(0, len(runs), 2)]
    return runs[0]


def _pitch(rows):
    return rows + 8


def _route_kernel(u_ref, wq_ref, keys_ref, eu_ref, ev_ref, idx_ref, gate_ref, eub_ref, evb_ref, s_ref):
    eub_ref[...] = eu_ref[...].astype(eub_ref.dtype)
    evb_ref[...] = ev_ref[...].astype(evb_ref.dtype)
    nkeys, half = keys_ref.shape[1], keys_ref.shape[2]
    k = PEER_TOPK
    groups = u_ref.shape[0] // V7X_LANES
    pitch = _pitch(nkeys)
    tile = (groups, V7X_LANES)
    q = jnp.dot(u_ref[...], wq_ref[...], preferred_element_type=F32)
    nt = (((1,), (1,)), ((), ()))

    def split(v):
        hi = v.astype(BF16)
        return hi, (v - hi.astype(F32)).astype(BF16)

    for p in range(2):
        k_hi, k_lo = split(keys_ref[p])
        q_hi, q_lo = split(q[:, p * half:(p + 1) * half])
        dot = lambda a, b: lax.dot_general(a, b, nt, preferred_element_type=F32)
        s = dot(k_hi, q_hi) + (dot(k_hi, q_lo) + dot(k_lo, q_hi))
        for g in range(groups):
            s_ref[p, g * pitch:g * pitch + nkeys, :] = s[:, g * V7X_LANES:(g + 1) * V7X_LANES]

    def scores(p):
        return [(s_ref[p, pl.ds(n, groups, stride=pitch), :], jnp.full(tile, float(n), F32)) for n in range(nkeys)]

    top_a = _sorted_top(scores(0), k)
    top_b = _sorted_top(scores(1), k)

    def cand(i, j):
        return (top_a[i][0] + top_b[j][0], jnp.full(tile, float(i * k + j), F32),
                top_a[i][1] * float(nkeys) + top_b[j][1])

    pad = (jnp.full(tile, -jnp.inf, F32), jnp.full(tile, float(k * k), F32), jnp.zeros(tile, F32))
    run = lambda i: [cand(i, j) for j in range(k // (i + 1))]
    tail = [cand(i, 0) for i in range(k // 2, k)]
    assert k == 16
    r0 = run(0)
    r1 = _bitonic_merge(run(1) + tail[::-1])
    r2 = _sort(run(2) + run(3) + run(4) + run(5) + run(6))
    r3 = run(7) + [pad] * (k - 2)
    best = _leading_half(_bitonic_merge(_leading_half(r0, r1)), _bitonic_merge(_leading_half(r2, r3)))

    m = functools.reduce(jnp.maximum, [c[0] for c in best])
    e = [jnp.exp(c[0] - m) for c in best]
    inv = 1.0 / functools.reduce(jnp.add, e)
    for n in range(k):
        idx_ref[n] = best[n][2].astype(jnp.int32)
        gate_ref[n] = e[n] * inv


def _route(u, wq, keys, eu, ev, tm):
    t, d = u.shape
    heads, _, nkeys, half = keys.shape
    groups = tm // V7X_LANES
    n_steps = (t // tm) * heads
    n_exp, de = eu.shape
    slab = n_exp // n_steps
    assert slab * n_steps == n_exp and slab % 16 == 0
    est = (2 * (tm * d * 2 + d * 2 * half * 2 + 2 * nkeys * half * 4) + 8 * nkeys * tm * 4
           + 2 * 2 * slab * de * 6)
    out_spec = pl.BlockSpec((PEER_TOPK, groups, V7X_LANES), lambda i, h: (h, i, 0))
    slab_spec = pl.BlockSpec((slab, de), lambda i, h: (i * heads + h, 0))
    return pl.pallas_call(
        _route_kernel,
        grid=(t // tm, heads),
        in_specs=[pl.BlockSpec((tm, d), lambda i, h: (i, 0)),
                  pl.BlockSpec((d, 2 * half), lambda i, h: (0, h)),
                  pl.BlockSpec((None, 2, nkeys, half), lambda i, h: (h, 0, 0, 0)),
                  slab_spec, slab_spec],
        out_specs=[out_spec, out_spec, slab_spec, slab_spec],
        out_shape=[jax.ShapeDtypeStruct((heads * PEER_TOPK, t // V7X_LANES, V7X_LANES), jnp.int32),
                   jax.ShapeDtypeStruct((heads * PEER_TOPK, t // V7X_LANES, V7X_LANES), F32),
                   jax.ShapeDtypeStruct(eu.shape, BF16), jax.ShapeDtypeStruct(ev.shape, BF16)],
        scratch_shapes=[pltpu.VMEM((2, groups * _pitch(nkeys), V7X_LANES), F32)],
        compiler_params=_params(("arbitrary", "arbitrary"), est),
        name="peer_route",
    )(u, wq, keys, eu, ev)


def _gate_matrix_kernel(idx_ref, gate_ref, o_ref, w0_ref, w1_ref):
    step = pl.program_id(0)

    @pl.when(step == 0)
    def _():
        w1_ref[...] = jnp.zeros_like(w1_ref)

    @pl.when(step % 2 == 0)
    def _():
        _gate_matrix_step(idx_ref, gate_ref, o_ref, w0_ref, w1_ref)

    @pl.when(step % 2 == 1)
    def _():
        _gate_matrix_step(idx_ref, gate_ref, o_ref, w1_ref, w0_ref)


def _gate_matrix_step(idx_ref, gate_ref, o_ref, cur_ref, prev_ref):
    tw, nsel = idx_ref.shape
    pitch = _pitch(PEER_NKEYS)
    key = lax.broadcasted_iota(jnp.int32, (PEER_NKEYS, nsel), 0)
    nt = (((1,), (1,)), ((), ()))
    zero = jnp.zeros((PEER_NKEYS, nsel), BF16)
    n_pairs = tw // 2
    planes_per_pair = PEER_NKEYS // n_pairs
    for p in range(n_pairs):
        a_hot, b_hot = [], []
        for t in (2 * p, 2 * p + 1):
            idx = idx_ref[t:t + 1, :]
            gate = gate_ref[t:t + 1, :]
            a_hot.append(jnp.where((idx >> NKEYS_SHIFT) == key, gate, 0.0).astype(BF16))
            b_hot.append(jnp.where((idx & (PEER_NKEYS - 1)) == key, 1.0, 0.0).astype(BF16))
        lhs = jnp.concatenate(a_hot, axis=1)
        rhs = jnp.concatenate([jnp.concatenate([b_hot[0], zero], axis=1),
                               jnp.concatenate([zero, b_hot[1]], axis=1)], axis=0)
        planes = lax.dot_general(lhs, rhs, nt, preferred_element_type=F32)
        for k in range(2):
            row = (2 * p + k) * pitch
            cur_ref[row:row + PEER_NKEYS, :] = planes[:, k * PEER_NKEYS:(k + 1) * PEER_NKEYS]
        for a in range(p * planes_per_pair, (p + 1) * planes_per_pair):
            o_ref[:, a * PEER_NKEYS:(a + 1) * PEER_NKEYS] = (
                prev_ref[pl.ds(a, tw, stride=pitch), :].astype(o_ref.dtype))


def _gate_matrix(idx, gate, tw):
    t, nsel = idx.shape
    n_exp = PEER_NKEYS * PEER_NKEYS
    n_blocks = t // tw
    assert PEER_NKEYS % (tw // 2) == 0
    est = 2 * (2 * tw * nsel * 4 + tw * n_exp * 2) + 2 * tw * _pitch(PEER_NKEYS) * PEER_NKEYS * 4
    block_in = pl.BlockSpec((tw, nsel), lambda i: (jnp.minimum(i, n_blocks - 1), 0))
    return pl.pallas_call(
        _gate_matrix_kernel,
        grid=(n_blocks + 1,),
        in_specs=[block_in, block_in],
        out_specs=pl.BlockSpec((tw, n_exp), lambda i: (jnp.maximum(i - 1, 0), 0)),
        out_shape=jax.ShapeDtypeStruct((t, n_exp), BF16),
        scratch_shapes=[pltpu.VMEM((tw * _pitch(PEER_NKEYS), PEER_NKEYS), F32)] * 2,
        compiler_params=_params(("arbitrary",), est),
        name="peer_gate_matrix",
    )(idx, gate)


def _peer_kernel(u_ref, eu_ref, ev_ref, w_ref, o_ref, *, n_chunk):
    @pl.when(pl.program_id(1) == 0)
    def _():
        o_ref[...] = jnp.zeros_like(o_ref)

    z = lax.dot_general(u_ref[...], eu_ref[...], (((1,), (1,)), ((), ())), preferred_element_type=F32)
    act = 0.5 * z * (1.0 + lax.erf(z * (2.0 ** -0.5)))
    h = (w_ref[...].astype(F32) * act).astype(BF16)
    for c in range(0, o_ref.shape[1], n_chunk):
        o_ref[:, c:c + n_chunk] += jnp.dot(h, ev_ref[:, c:c + n_chunk], preferred_element_type=F32)


def _final_ln_kernel(x_ref, f_ref, mod_ref, lg_ref, lb_ref, o_ref, *, alpha):
    r = alpha * x_ref[...] + mod_ref[5:6, :] * f_ref[...]
    o_ref[...] = _layer_norm(r, lg_ref[...], lb_ref[...])


def _final_ln(x, f, mod, ln_g, ln_b, alpha, seq, tm):
    t, d = x.shape
    tile = pl.BlockSpec((tm, d), lambda i: (i, 0))
    vec = pl.BlockSpec((1, d), lambda i: (0, 0))
    return pl.pallas_call(
        functools.partial(_final_ln_kernel, alpha=alpha),
        grid=(t // tm,),
        in_specs=[tile, tile, pl.BlockSpec((None, 6, d), lambda i: ((i * tm) // seq, 0, 0)), vec, vec],
        out_specs=tile,
        out_shape=jax.ShapeDtypeStruct((t, d), F32),
        compiler_params=_params(("arbitrary",), 8 * tm * d * 4),
        name="final_ln",
    )(x, f, mod, ln_g.reshape(1, -1), ln_b.reshape(1, -1))


def _peer(u, eu, ev, w, tm, te, n_chunk):
    t, d = u.shape
    n_exp = ev.shape[0]
    once = pl.Buffered(1)
    est = tm * d * 2 + tm * d * 4 + 2 * (2 * te * d * 2 + tm * te * 2) + 4 * tm * te * 4 + tm * n_chunk * 4
    return pl.pallas_call(
        functools.partial(_peer_kernel, n_chunk=n_chunk),
        grid=(t // tm, n_exp // te),
        in_specs=[pl.BlockSpec((tm, d), lambda i, j: (i, 0), pipeline_mode=once),
                  pl.BlockSpec((te, d), lambda i, j: (j, 0)),
                  pl.BlockSpec((te, d), lambda i, j: (j, 0)),
                  pl.BlockSpec((tm, te), lambda i, j: (i, j))],
        out_specs=pl.BlockSpec((tm, d), lambda i, j: (i, 0), pipeline_mode=once),
        out_shape=jax.ShapeDtypeStruct((t, d), F32),
        compiler_params=_params(("arbitrary", "arbitrary"), est),
        name="peer_experts",
    )(u, eu, ev, w)


def _filter_features(seq):
    t = jnp.linspace(0.0, 1.0, seq, dtype=F32)[:, None]
    bands = (FILTER_EMB - 1) // 2
    w = 2.0 * math.pi * jnp.arange(seq, dtype=F32)[:, None] / seq
    f = jnp.linspace(1e-4, bands - 1, bands, dtype=F32)[None, :]
    z = jnp.concatenate([t, jnp.cos(f * w), -jnp.sin(f * w)], axis=-1)
    return jnp.pad(z, ((0, 0), (0, FILTER_PAD - FILTER_EMB)))


@functools.lru_cache(maxsize=None)
def _dft_matrices(seq):
    n2 = 2 * seq
    f = np.arange(seq, dtype=np.int64)[:, None]
    n = np.arange(seq, dtype=np.int64)[None, :]
    ang = ((f * n) % n2).astype(np.float64) * (2.0 * math.pi / n2)
    sin = np.where(f == 0, (1 - 2 * (n % 2)).astype(np.float64), np.sin(ang))
    return np.concatenate([np.cos(ang), sin], axis=0).astype(np.float32)


def kernel(x, c, ctx, c_ctx, w_ada, b_ada, w_in, conv_w, conv_b, filt_w1, filt_b1, filt_w2, filt_b2,
           filt_w3, filt_b3, filt_w4, filt_freq, hy_skip, na_rpb, hy_norm_g, na_norm_g, w_out,
           ln1_g, ln1_b, peer_wq, peer_keys, peer_u, peer_v, ln2_g, ln2_b):
    depth = w_ada.shape[0]
    assert depth == 1, "single-layer block: the context stream is never updated"
    bsz, seq, d = x.shape
    rows = seq // GRID_W
    kh = min(NA_KH, rows)
    assert kh == NA_KH and seq % GRID_W == 0
    alpha = (2.0 * depth) ** 0.25
    d_na = d - D_HY
    off_q = 3 * D_HY
    off_k = off_q + d_na

    n_cond = 8
    cond = jnp.concatenate([c, c_ctx[None], jnp.zeros((n_cond - bsz - 1, d), F32)], axis=0)
    mod = _ada(cond, w_ada[0], b_ada[0]).reshape(n_cond, 6, d)

    w_in_b = w_in[0].astype(BF16)
    assert off_q == 3 * d_na and off_k == 2 * (2 * d_na)
    (z_hy,) = _proj(x, mod, lambda b: b, w_in_b, off_q, 0, F32, PROJ_TM)
    (qkv,) = _proj(x, mod, lambda b: b, w_in_b, 3 * d_na, 1, BF16, PROJ_TM)
    (kv_ctx,) = _proj(ctx, mod, lambda b: bsz, w_in_b, 2 * d_na, 2, BF16, CTX_TM)

    deltas = jnp.abs(jnp.linspace(math.log(DECAY_TARGET) / SLOW_DECAY_PCT,
                                  math.log(DECAY_TARGET) / FAST_DECAY_PCT, D_HY, dtype=F32))[None, :]
    padw = lambda a: jnp.pad(a, ((0, FILTER_PAD - a.shape[0]), (0, FILTER_PAD - a.shape[1])))
    padv = lambda a: jnp.pad(a, ((0, FILTER_PAD - a.shape[0]),))
    w4 = jnp.pad(filt_w4[0], ((0, FILTER_PAD - filt_w4.shape[1]), (0, 0)))
    freq = jnp.pad(filt_freq[0], ((0, 0), (0, FILTER_PAD - filt_freq.shape[2])))
    ksum, kdiff = _hyena_filters(_filter_features(seq), padw(filt_w1[0]), padv(filt_b1[0]), padw(filt_w2[0]),
                                 padv(filt_b2[0]), padw(filt_w3[0]), padv(filt_b3[0]), w4, freq, deltas, HYENA_TC)
    ffwd = jnp.asarray(_dft_matrices(seq)).astype(BF16)
    finv = ffwd.T
    kp, kq = _spectrum(ffwd, ksum, kdiff, HYENA_TC)
    hy_out = _hyena(z_hy, conv_w[0], conv_b[0], ffwd, finv, kp, kq, hy_skip[0], HYENA_TC)

    bands = _bias_bands(na_rpb[0], kh)
    band_of_row = lambda r: jnp.clip(r - kh // 2, 0, rows - kh) - r + (NA_KH - 1)
    na_out, w_out_b, wq_b = _attention(qkv, kv_ctx, bands, band_of_row, rows, kh, [w_out[0], peer_wq[0]])

    x_mid, u_peer = _merge(hy_out, na_out, x, mod, w_out_b, hy_norm_g[0], na_norm_g[0],
                           ln1_g[0], ln1_b[0], alpha, MERGE_TM)

    t = bsz * seq
    u2 = u_peer.reshape(t, d)
    idx_t, gate_t, eu_b, ev_b = _route(u2, wq_b, peer_keys[0], peer_u[0], peer_v[0], ROUTE_TM)
    w_gate = _gate_matrix(idx_t.reshape(-1, t).T, gate_t.reshape(-1, t).T, GATE_TW)
    f = _peer(u2, eu_b, ev_b, w_gate, PEER_TM, PEER_TE, PEER_CHUNK)
    out = _final_ln(x_mid.reshape(t, d), f, mod, ln2_g[0], ln2_b[0], alpha, seq, FINAL_TM)
    return out.reshape(bsz, seq, d)
```

```python
import functools
import math

import numpy as np
import jax
import jax.numpy as jnp
from jax import lax
from jax.experimental import pallas as pl
from jax.experimental.pallas import tpu as pltpu

F32 = jnp.float32
BF16 = jnp.bfloat16
HIGHEST = lax.Precision.HIGHEST

GRID_W = 64
HEAD_DIM = 64
D_HY = 1024
HY_ORDER = 2
FILTER_EMB = 33
FILTER_PAD = 128
DECAY_TARGET = 1e-2
FAST_DECAY_PCT = 0.3
SLOW_DECAY_PCT = 1.5
NA_KH = 8
NA_KW = 16
PEER_HEADS = 8
PEER_NKEYS = 128
NKEYS_SHIFT = PEER_NKEYS.bit_length() - 1
assert 1 << NKEYS_SHIFT == PEER_NKEYS
PEER_TOPK = 16
LN_EPS = 1e-5
NEG_INF = -1e30

V7X_VMEM_BYTES = 64 * 1024 * 1024
V7X_LANES = 128
V7X_MXU_DEPTH = 256
NA_GROUP = V7X_MXU_DEPTH // HEAD_DIM


MIB = 1024 * 1024
VMEM_TEMPORARIES = 12 * MIB
VMEM_RESERVED = 6 * MIB

ADA_TN = 2048
PROJ_TM = 512
CTX_TM = 256
HYENA_TC = 256
MERGE_TM = 256
ROUTE_TM = 1024
GATE_TW = 64
PEER_TM, PEER_TE, PEER_CHUNK = 2048, 512, 512
PEER_LN_ROWS = 256


def _vmem_limit(estimate_bytes):
    return int(min(estimate_bytes + VMEM_TEMPORARIES, V7X_VMEM_BYTES - VMEM_RESERVED))


def _params(semantics, estimate_bytes):
    return pltpu.CompilerParams(dimension_semantics=semantics, vmem_limit_bytes=_vmem_limit(estimate_bytes))


def _ada_kernel(cond_ref, w_ref, b_ref, o_ref):
    s = jax.nn.silu(cond_ref[...])
    o_ref[...] = jnp.dot(s, w_ref[...], precision=HIGHEST, preferred_element_type=F32) + b_ref[...]


def _ada(cond, w, b):
    n, d = cond.shape
    tn = ADA_TN
    return pl.pallas_call(
        _ada_kernel,
        grid=(w.shape[1] // tn,),
        in_specs=[pl.BlockSpec((n, d), lambda j: (0, 0)),
                  pl.BlockSpec((d, tn), lambda j: (0, j)),
                  pl.BlockSpec((1, tn), lambda j: (0, j))],
        out_specs=pl.BlockSpec((n, tn), lambda j: (0, j)),
        out_shape=jax.ShapeDtypeStruct((n, w.shape[1]), F32),
        compiler_params=_params(("arbitrary",), 2 * d * tn * 4),
        name="ada",
    )(cond, w, b.reshape(1, -1))


def _rider_specs(riders, n_steps, step_of):
    specs = []
    for w in riders:
        slab = w.shape[0] // n_steps
        assert slab * n_steps == w.shape[0] and slab % 16 == 0
        specs.append(pl.BlockSpec((slab, w.shape[1]), lambda *g: (step_of(*g), 0)))
    return specs


def _rider_bytes(riders, n_steps):
    return sum(2 * (w.size // n_steps) * 6 for w in riders)


def _convert_riders(src_refs, dst_refs):
    for src_ref, dst_ref in zip(src_refs, dst_refs):
        dst_ref[...] = src_ref[...].astype(dst_ref.dtype)


def _proj_kernel(x_ref, mod_ref, w_ref, *rest, n_riders):
    o_ref = rest[n_riders]
    _convert_riders(rest[:n_riders], rest[n_riders + 1:])
    u = x_ref[...] * (1.0 + mod_ref[1:2, :]) + mod_ref[0:1, :]
    o_ref[...] = jnp.dot(u.astype(BF16), w_ref[...], preferred_element_type=F32).astype(o_ref.dtype)


def _proj(x, mod, mod_row, w, n, col_block, out_dtype, tm, riders=()):
    bsz, seq, d = x.shape
    n_i = seq // tm
    slabs = _rider_specs(riders, bsz * n_i, lambda b, i: b * n_i + i)
    est = d * n * 2 + 2 * tm * d * 4 + 2 * tm * n * 4 + tm * n * 4 + _rider_bytes(riders, bsz * n_i)
    return pl.pallas_call(
        functools.partial(_proj_kernel, n_riders=len(riders)),
        grid=(bsz, n_i),
        in_specs=[pl.BlockSpec((None, tm, d), lambda b, i: (b, i, 0)),
                  pl.BlockSpec((None, 6, d), lambda b, i: (mod_row(b), 0, 0)),
                  pl.BlockSpec((d, n), lambda b, i: (0, col_block), pipeline_mode=pl.Buffered(1))] + slabs,
        out_specs=[pl.BlockSpec((None, tm, n), lambda b, i: (b, i, 0))] + slabs,
        out_shape=[jax.ShapeDtypeStruct((bsz, seq, n), out_dtype)]
        + [jax.ShapeDtypeStruct(r.shape, BF16) for r in riders],
        compiler_params=_params(("arbitrary", "arbitrary"), est),
        name="proj",
    )(x, mod, w, *riders)


def _filter_kernel(z_ref, w1_ref, b1_ref, w2_ref, b2_ref, w3_ref, b3_ref, fr_ref, w4f_ref, w4b_ref, dl_ref,
                   ks_ref, kd_ref, h_ref):
    dot = functools.partial(jnp.dot, precision=HIGHEST, preferred_element_type=F32)

    @pl.when((pl.program_id(0) == 0) & (pl.program_id(1) == 0))
    def _():
        h = jnp.sin(fr_ref[0:1, :] * (dot(z_ref[...], w1_ref[...]) + b1_ref[...]))
        h = jnp.sin(fr_ref[1:2, :] * (dot(h, w2_ref[...]) + b2_ref[...]))
        h_ref[...] = jnp.sin(fr_ref[2:3, :] * (dot(h, w3_ref[...]) + b3_ref[...]))

    h = h_ref[...]
    decay = jnp.exp(-z_ref[:, 0:1] * dl_ref[...])
    hf = dot(h, w4f_ref[...]) * decay
    hb = dot(h, w4b_ref[...]) * decay
    norm = jnp.sum(jnp.abs(hf) + jnp.abs(hb), axis=0, keepdims=True)
    hf = hf / norm
    hb = hb / norm
    row = lax.broadcasted_iota(jnp.int32, hb.shape, 0)
    hb = jnp.where(row == 0, 0.0, hb)
    ks_ref[...] = hf + hb
    kd_ref[...] = hb - hf


def _hyena_filters(zfeat, w1, b1, w2, b2, w3, b3, w4, freq, deltas, tc):
    seq = zfeat.shape[0]
    fo = w2.shape[0]
    nct = D_HY // tc
    full = lambda a: pl.BlockSpec(a.shape, lambda o, j: (0,) * a.ndim)
    args = (zfeat, w1, b1.reshape(1, -1), w2, b2.reshape(1, -1), w3, b3.reshape(1, -1), freq)
    out_spec = pl.BlockSpec((None, seq, tc), lambda o, j: (o, 0, j))
    return pl.pallas_call(
        _filter_kernel,
        grid=(HY_ORDER, nct),
        in_specs=[full(a) for a in args] + [
            pl.BlockSpec((fo, tc), lambda o, j: (0, (2 * o) * nct + j)),
            pl.BlockSpec((fo, tc), lambda o, j: (0, (2 * o + 1) * nct + j)),
            pl.BlockSpec((1, tc), lambda o, j: (0, j))],
        out_specs=[out_spec, out_spec],
        out_shape=[jax.ShapeDtypeStruct((HY_ORDER, seq, D_HY), F32)] * 2,
        scratch_shapes=[pltpu.VMEM((seq, fo), F32)],
        compiler_params=_params(("arbitrary", "arbitrary"), 12 * seq * tc * 4),
        name="hyena_filter",
    )(*args, w4, w4, deltas)


def _spectrum_kernel(ff_ref, ks_ref, kd_ref, kp_ref, kq_ref):
    seq = ks_ref.shape[0]
    ks = ks_ref[...].astype(BF16)
    kd = kd_ref[...].astype(BF16)
    p = jnp.dot(ff_ref[0:seq, :], ks, preferred_element_type=F32)
    q = jnp.dot(ff_ref[seq:2 * seq, :], kd, preferred_element_type=F32)
    nyq = jnp.dot(ff_ref[seq:seq + 16, :], ks, preferred_element_type=F32)[0:1, :]
    row = lax.broadcasted_iota(jnp.int32, p.shape, 0)
    wf = jnp.where(row == 0, 0.5 / seq, 1.0 / seq)
    kp_ref[...] = p * wf
    kq_ref[...] = jnp.where(row == 0, nyq, q) * wf


def _spectrum(ffwd, ksum, kdiff, tc):
    seq = ksum.shape[1]
    spec = pl.BlockSpec((None, seq, tc), lambda o, j: (o, 0, j))
    return pl.pallas_call(
        _spectrum_kernel,
        grid=(HY_ORDER, D_HY // tc),
        in_specs=[pl.BlockSpec(ffwd.shape, lambda o, j: (0, 0)), spec, spec],
        out_specs=[spec, spec],
        out_shape=[jax.ShapeDtypeStruct(ksum.shape, F32)] * 2,
        compiler_params=_params(("arbitrary", "arbitrary"), ffwd.size * 2 + 12 * seq * tc * 4),
        name="hyena_spectrum",
    )(ffwd, ksum, kdiff)


def _short_conv(z, w, b):
    seq = z.shape[0]
    row = lax.broadcasted_iota(jnp.int32, z.shape, 0)
    prev = jnp.where(row == 0, 0.0, pltpu.roll(z, 1, 0))
    nxt = jnp.where(row == seq - 1, 0.0, pltpu.roll(z, seq - 1, 0))
    return prev * w[0:1, :] + z * w[1:2, :] + nxt * w[2:3, :] + b


def _dft_mul(u, ff_ref, kp_ref, kq_ref, o_ref):
    seq = u.shape[0]
    a = jnp.dot(ff_ref[...], u.astype(BF16), preferred_element_type=F32)
    p, q = a[0:seq], a[seq:2 * seq]
    kp, kq = kp_ref[...], kq_ref[...]
    row0 = lax.broadcasted_iota(jnp.int32, p.shape, 0) == 0
    o_ref[0:seq, :] = jnp.where(row0, kp * p, kp * p + kq * q).astype(o_ref.dtype)
    o_ref[seq:2 * seq, :] = jnp.where(row0, kq * q, kp * q - kq * p).astype(o_ref.dtype)


def _fwd_conv_kernel(z_ref, cw_ref, cb_ref, ff_ref, kp_ref, kq_ref, o_ref):
    _dft_mul(_short_conv(z_ref[...], cw_ref[...], cb_ref[...]), ff_ref, kp_ref, kq_ref, o_ref)


def _fwd_plain_kernel(u_ref, ff_ref, kp_ref, kq_ref, o_ref):
    _dft_mul(u_ref[...], ff_ref, kp_ref, kq_ref, o_ref)


def _inv_first_kernel(y_ref, fi_ref, zv_ref, zx_ref, cwv_ref, cbv_ref, cwx_ref, cbx_ref, skip_ref, o_ref):
    v = _short_conv(zv_ref[...], cwv_ref[...], cbv_ref[...])
    x1 = _short_conv(zx_ref[...], cwx_ref[...], cbx_ref[...])
    y = jnp.dot(fi_ref[...], y_ref[...], preferred_element_type=F32)
    o_ref[...] = x1 * (y + v * skip_ref[...])


def _inv_second_kernel(y_ref, fi_ref, g_ref, zx_ref, cwx_ref, cbx_ref, skip_ref, o_ref):
    x2 = _short_conv(zx_ref[...], cwx_ref[...], cbx_ref[...])
    y = jnp.dot(fi_ref[...], y_ref[...], preferred_element_type=F32)
    o_ref[...] = x2 * (y + g_ref[...] * skip_ref[...])


def _hyena(z, conv_w, conv_b, ffwd, finv, kp, kq, skip, tc):
    bsz, seq, _ = z.shape
    nct = D_HY // tc
    sem = ("arbitrary", "arbitrary")
    zcol = lambda g: pl.BlockSpec((None, seq, tc), lambda b, j: (b, 0, g * nct + j))
    wcol = lambda g: pl.BlockSpec((3, tc), lambda b, j: (0, g * nct + j))
    bcol = lambda g: pl.BlockSpec((1, tc), lambda b, j: (0, g * nct + j))
    act = pl.BlockSpec((None, seq, tc), lambda b, j: (b, 0, j))
    spec2 = pl.BlockSpec((None, 2 * seq, tc), lambda b, j: (b, 0, j))
    kspec = lambda o: pl.BlockSpec((None, seq, tc), lambda b, j: (o, 0, j))
    sspec = lambda o: pl.BlockSpec((None, 1, tc), lambda b, j: (o, 0, j))
    mat = pl.BlockSpec(ffwd.shape, lambda b, j: (0, 0))
    mati = pl.BlockSpec(finv.shape, lambda b, j: (0, 0))
    est = ffwd.size * 2 + 14 * seq * tc * 4
    cb = conv_b.reshape(1, -1)
    skip3 = skip.reshape(HY_ORDER, 1, D_HY)
    spec_shape = jax.ShapeDtypeStruct((bsz, 2 * seq, D_HY), BF16)
    act_shape = jax.ShapeDtypeStruct((bsz, seq, D_HY), F32)

    y1 = pl.pallas_call(
        _fwd_conv_kernel, grid=(bsz, nct),
        in_specs=[zcol(0), wcol(0), bcol(0), mat, kspec(0), kspec(0)],
        out_specs=spec2, out_shape=spec_shape, compiler_params=_params(sem, est), name="hyena_fwd1",
    )(z, conv_w, cb, ffwd, kp, kq)
    g = pl.pallas_call(
        _inv_first_kernel, grid=(bsz, nct),
        in_specs=[spec2, mati, zcol(0), zcol(1), wcol(0), bcol(0), wcol(1), bcol(1), sspec(0)],
        out_specs=act, out_shape=act_shape, compiler_params=_params(sem, est), name="hyena_inv1",
    )(y1, finv, z, z, conv_w, cb, conv_w, cb, skip3)
    y2 = pl.pallas_call(
        _fwd_plain_kernel, grid=(bsz, nct),
        in_specs=[act, mat, kspec(1), kspec(1)],
        out_specs=spec2, out_shape=spec_shape, compiler_params=_params(sem, est), name="hyena_fwd2",
    )(g, ffwd, kp, kq)
    return pl.pallas_call(
        _inv_second_kernel, grid=(bsz, nct),
        in_specs=[spec2, mati, act, zcol(2), wcol(2), bcol(2), sspec(1)],
        out_specs=act, out_shape=act_shape, compiler_params=_params(sem, est), name="hyena_inv2",
    )(y2, finv, g, z, conv_w, cb, skip3)


def _bias_kernel(rpb_ref, o_ref, *, kh):
    h = pl.program_id(0)
    n_r, n_c = 2 * NA_KH - 1, 2 * NA_KW - 1
    cq = lax.broadcasted_iota(jnp.int32, (GRID_W, GRID_W), 0)
    ck = lax.broadcasted_iota(jnp.int32, (GRID_W, GRID_W), 1)
    rel = ck - jnp.clip(cq - NA_KW // 2, 0, GRID_W - NA_KW)
    ok = (rel >= 0) & (rel < NA_KW)
    dcol = jnp.clip(ck - cq + NA_KW - 1, 0, n_c - 1)
    tables = []
    for r in range(n_r):
        acc = jnp.zeros((GRID_W, GRID_W), F32)
        for c in range(n_c):
            acc = jnp.where(dcol == c, rpb_ref[(h * n_r + r) * n_c + c], acc)
        tables.append(jnp.where(ok, acc, NEG_INF))
    for off in range(o_ref.shape[0]):
        for i in range(kh):
            o_ref[off, :, i * GRID_W:(i + 1) * GRID_W] = tables[off + i]


def _bias_bands(rpb, kh):
    heads = rpb.shape[0]
    n_off = 2 * NA_KH - kh
    return pl.pallas_call(
        functools.partial(_bias_kernel, kh=kh),
        grid=(heads,),
        in_specs=[pl.BlockSpec(memory_space=pltpu.SMEM)],
        out_specs=pl.BlockSpec((None, n_off, GRID_W, kh * GRID_W), lambda h: (h, 0, 0, 0)),
        out_shape=jax.ShapeDtypeStruct((heads, n_off, GRID_W, kh * GRID_W), F32),
        compiler_params=_params(("arbitrary",), 4 << 20),
        name="na_bias",
    )(rpb.reshape(-1))


def _attn_kernel(q_ref, k_ref, v_ref, kc_ref, vc_ref, bias0_ref, bias1_ref, *rest, rows, kh, n_riders):
    o_ref = rest[n_riders]
    _convert_riders(rest[:n_riders], rest[n_riders + 1:])
    r2 = pl.program_id(1)
    nk = kh * GRID_W
    gw = NA_GROUP * HEAD_DIM
    lane = lax.broadcasted_iota(jnp.int32, (1, gw), 1)
    head_of_lane = [(lane >= e * HEAD_DIM) & (lane < (e + 1) * HEAD_DIM) for e in range(NA_GROUP)]
    nt = (((1,), (1,)), ((), ()))
    starts = [pl.multiple_of(jnp.clip(2 * r2 + i - kh // 2, 0, rows - kh) * GRID_W, GRID_W) for i in range(2)]
    bias_refs = (bias0_ref, bias1_ref)
    per_row = NA_GROUP * GRID_W
    zero = jnp.zeros((GRID_W, gw), BF16)
    for hg in range(q_ref.shape[1] // gw):
        cols = slice(hg * gw, (hg + 1) * gw)
        qp = q_ref[:, cols] * jnp.asarray(HEAD_DIM ** -0.5, BF16)
        qm = jnp.concatenate([jnp.where(head_of_lane[e], qp[i * GRID_W:(i + 1) * GRID_W], zero)
                              for i in range(2) for e in range(NA_GROUP)], axis=0)
        s_ctx = lax.dot_general(qm, kc_ref[:, cols], nt, preferred_element_type=F32)
        m_ctx = jnp.max(s_ctx, axis=-1, keepdims=True)
        s_lat, m_all = [], []
        for i in range(2):
            blk = slice(i * per_row, (i + 1) * per_row)
            bias = jnp.concatenate([bias_refs[i][hg * NA_GROUP + e] for e in range(NA_GROUP)], axis=0)
            s = lax.dot_general(qm[blk], k_ref[pl.ds(starts[i], nk), cols], nt, preferred_element_type=F32) + bias
            s_lat.append(s)
            m_all.append(jnp.maximum(jnp.max(s, axis=-1, keepdims=True), m_ctx[blk]))
        p_ctx = jnp.exp(s_ctx - jnp.concatenate(m_all, axis=0))
        o_ctx = jnp.dot(p_ctx.astype(BF16), vc_ref[:, cols], preferred_element_type=F32)
        l_ctx = jnp.sum(p_ctx, axis=-1, keepdims=True)
        for i in range(2):
            blk = slice(i * per_row, (i + 1) * per_row)
            p = jnp.exp(s_lat[i] - m_all[i])
            o = jnp.dot(p.astype(BF16), v_ref[pl.ds(starts[i], nk), cols], preferred_element_type=F32) + o_ctx[blk]
            o = o / (jnp.sum(p, axis=-1, keepdims=True) + l_ctx[blk])
            out = o[0:GRID_W]
            for e in range(1, NA_GROUP):
                out = jnp.where(head_of_lane[e], o[e * GRID_W:(e + 1) * GRID_W], out)
            o_ref[i * GRID_W:(i + 1) * GRID_W, cols] = out


def _attention(qkv, kv_ctx, bands, band_of_row, rows, kh, riders):
    bsz, seq, d3 = qkv.shape
    dn = d3 // 3
    ctx_len = kv_ctx.shape[1]
    heads, _, _, nk = bands.shape
    assert rows % 2 == 0
    n_steps = bsz * (rows // 2)
    slab_specs = _rider_specs(riders, n_steps, lambda b, r: b * (rows // 2) + r)
    est = (2 * (2 * seq * dn * 2 + 2 * ctx_len * dn * 2 + 2 * heads * GRID_W * nk * 4 + 2 * GRID_W * dn * 6)
           + _rider_bytes(riders, n_steps))
    band = lambda i: pl.BlockSpec((heads, None, GRID_W, nk), lambda b, r: (0, band_of_row(2 * r + i), 0, 0))
    return pl.pallas_call(
        functools.partial(_attn_kernel, rows=rows, kh=kh, n_riders=len(riders)),
        grid=(bsz, rows // 2),
        in_specs=[pl.BlockSpec((None, 2 * GRID_W, dn), lambda b, r: (b, r, 0)),
                  pl.BlockSpec((None, seq, dn), lambda b, r: (b, 0, 1)),
                  pl.BlockSpec((None, seq, dn), lambda b, r: (b, 0, 2)),
                  pl.BlockSpec((None, ctx_len, dn), lambda b, r: (b, 0, 0)),
                  pl.BlockSpec((None, ctx_len, dn), lambda b, r: (b, 0, 1)),
                  band(0), band(1)] + slab_specs,
        out_specs=[pl.BlockSpec((None, 2 * GRID_W, dn), lambda b, r: (b, r, 0))] + slab_specs,
        out_shape=[jax.ShapeDtypeStruct((bsz, seq, dn), F32)]
        + [jax.ShapeDtypeStruct(w.shape, BF16) for w in riders],
        compiler_params=_params(("arbitrary", "arbitrary"), est),
        name="na_attention",
    )(qkv, qkv, qkv, kv_ctx, kv_ctx, bands, bands, *riders)


def _rms(x, g):
    return x * lax.rsqrt(jnp.mean(x * x, axis=-1, keepdims=True) + LN_EPS) * g


def _layer_norm(x, g, b):
    xc = x - jnp.mean(x, axis=-1, keepdims=True)
    var = jnp.mean(xc * xc, axis=-1, keepdims=True)
    return xc * lax.rsqrt(var + LN_EPS) * g + b


def _merge_kernel(hy_ref, na_ref, x_ref, mod_ref, w_ref, ghy_ref, gna_ref, lg_ref, lb_ref, xo_ref, uo_ref, *, alpha):
    dh = hy_ref.shape[1]
    hy = _rms(hy_ref[...], ghy_ref[...]).astype(BF16)
    na = _rms(na_ref[...], gna_ref[...]).astype(BF16)
    y = (jnp.dot(hy, w_ref[0:dh, :], preferred_element_type=F32)
         + jnp.dot(na, w_ref[dh:, :], preferred_element_type=F32))
    xn = _layer_norm(alpha * x_ref[...] + mod_ref[2:3, :] * y, lg_ref[...], lb_ref[...])
    xo_ref[...] = xn
    uo_ref[...] = (xn * (1.0 + mod_ref[4:5, :]) + mod_ref[3:4, :]).astype(uo_ref.dtype)


def _merge(hy, na, x, mod, w_out, g_hy, g_na, ln_g, ln_b, alpha, tm):
    bsz, seq, d = x.shape
    dh, dn = hy.shape[2], na.shape[2]
    row = lambda a: a.reshape(1, -1)
    vec = lambda n: pl.BlockSpec((1, n), lambda b, i: (0, 0))
    tile = lambda n: pl.BlockSpec((None, tm, n), lambda b, i: (b, i, 0))
    est = d * d * 2 + 2 * tm * (dh + dn + 2 * d) * 4 + 2 * tm * d * 6 + 4 * tm * d * 4
    return pl.pallas_call(
        functools.partial(_merge_kernel, alpha=alpha),
        grid=(bsz, seq // tm),
        in_specs=[tile(dh), tile(dn), tile(d),
                  pl.BlockSpec((None, 6, d), lambda b, i: (b, 0, 0)),
                  pl.BlockSpec((d, d), lambda b, i: (0, 0)),
                  vec(dh), vec(dn), vec(d), vec(d)],
        out_specs=[tile(d), tile(d)],
        out_shape=[jax.ShapeDtypeStruct((bsz, seq, d), F32), jax.ShapeDtypeStruct((bsz, seq, d), BF16)],
        compiler_params=_params(("arbitrary", "arbitrary"), est),
        name="merge",
    )(hy, na, x, mod, w_out, row(g_hy), row(g_na), row(ln_g), row(ln_b))


def _first(a, b):
    return (a[0] > b[0]) | ((a[0] == b[0]) & (a[1] < b[1]))


def _order_pair(xs, i, j):
    a, b = xs[i], xs[j]
    f = _first(a, b)
    xs[i] = (jnp.maximum(a[0], b[0]),) + tuple(jnp.where(f, p, r) for p, r in zip(a[1:], b[1:]))
    xs[j] = (jnp.minimum(a[0], b[0]),) + tuple(jnp.where(f, r, p) for p, r in zip(a[1:], b[1:]))


def _bitonic_merge(xs):
    n = len(xs)
    j = n // 2
    while j >= 1:
        for i in range(n):
            if i & j == 0:
                _order_pair(xs, i, i | j)
        j //= 2
    return xs


def _sort(xs):
    n = len(xs)
    p = 1
    while p < n:
        k = p
        while k >= 1:
            for j in range(k % p, n - k, 2 * k):
                for i in range(min(k, n - j - k)):
                    if (i + j) // (2 * p) == (i + j + k) // (2 * p):
                        _order_pair(xs, i + j, i + j + k)
            k //= 2
        p *= 2
    return xs


def _leading_half(a, b):
    out = []
    for x, y in zip(a, reversed(b)):
        f = _first(x, y)
        out.append((jnp.maximum(x[0], y[0]),) + tuple(jnp.where(f, p, r) for p, r in zip(x[1:], y[1:])))
    return out


def _sorted_top(items, k):
    runs = [_sort(items[i:i + k]) for i in range(0, len(items), k)]
    while len(runs) > 1:
        runs = [_bitonic_merge(_leading_half(runs[i], runs[i + 1])) for i in range(0, len(runs), 2)]
    return runs[0]


def _pitch(rows):
    return rows + 8


def _route_kernel(u_ref, wq_ref, keys_ref, eu_ref, ev_ref, idx_ref, gate_ref, eub_ref, evb_ref, s_ref):
    eub_ref[...] = eu_ref[...].astype(eub_ref.dtype)
    evb_ref[...] = ev_ref[...].astype(evb_ref.dtype)
    nkeys, half = keys_ref.shape[1], keys_ref.shape[2]
    k = PEER_TOPK
    groups = u_ref.shape[0] // V7X_LANES
    pitch = _pitch(nkeys)
    tile = (groups, V7X_LANES)
    q = jnp.dot(u_ref[...], wq_ref[...], preferred_element_type=F32)
    nt = (((1,), (1,)), ((), ()))

    def split(v):
        hi = v.astype(BF16)
        return hi, (v - hi.astype(F32)).astype(BF16)

    for p in range(2):
        k_hi, k_lo = split(keys_ref[p])
        q_hi, q_lo = split(q[:, p * half:(p + 1) * half])
        dot = lambda a, b: lax.dot_general(a, b, nt, preferred_element_type=F32)
        s = dot(k_hi, q_hi) + (dot(k_hi, q_lo) + dot(k_lo, q_hi))
        for g in range(groups):
            s_ref[p, g * pitch:g * pitch + nkeys, :] = s[:, g * V7X_LANES:(g + 1) * V7X_LANES]

    def scores(p):
        return [(s_ref[p, pl.ds(n, groups, stride=pitch), :], jnp.full(tile, float(n), F32)) for n in range(nkeys)]

    top_a = _sorted_top(scores(0), k)
    top_b = _sorted_top(scores(1), k)

    def cand(i, j):
        return (top_a[i][0] + top_b[j][0], jnp.full(tile, float(i * k + j), F32),
                top_a[i][1] * float(nkeys) + top_b[j][1])

    pad = (jnp.full(tile, -jnp.inf, F32), jnp.full(tile, float(k * k), F32), jnp.zeros(tile, F32))
    run = lambda i: [cand(i, j) for j in range(k // (i + 1))]
    tail = [cand(i, 0) for i in range(k // 2, k)]
    assert k == 16
    r0 = run(0)
    r1 = _bitonic_merge(run(1) + tail[::-1])
    r2 = _sort(run(2) + run(3) + run(4) + run(5) + run(6))
    r3 = run(7) + [pad] * (k - 2)
    best = _leading_half(_bitonic_merge(_leading_half(r0, r1)), _bitonic_merge(_leading_half(r2, r3)))

    m = functools.reduce(jnp.maximum, [c[0] for c in best])
    e = [jnp.exp(c[0] - m) for c in best]
    inv = 1.0 / functools.reduce(jnp.add, e)
    for n in range(k):
        idx_ref[n] = best[n][2].astype(jnp.int32)
        gate_ref[n] = e[n] * inv


def _route(u, wq, keys, eu, ev, tm):
    t, d = u.shape
    heads, _, nkeys, half = keys.shape
    groups = tm // V7X_LANES
    n_steps = (t // tm) * heads
    n_exp, de = eu.shape
    slab = n_exp // n_steps
    assert slab * n_steps == n_exp and slab % 16 == 0
    est = (2 * (tm * d * 2 + d * 2 * half * 2 + 2 * nkeys * half * 4) + 8 * nkeys * tm * 4
           + 2 * 2 * slab * de * 6)
    out_spec = pl.BlockSpec((PEER_TOPK, groups, V7X_LANES), lambda i, h: (h, i, 0))
    slab_spec = pl.BlockSpec((slab, de), lambda i, h: (i * heads + h, 0))
    return pl.pallas_call(
        _route_kernel,
        grid=(t // tm, heads),
        in_specs=[pl.BlockSpec((tm, d), lambda i, h: (i, 0)),
                  pl.BlockSpec((d, 2 * half), lambda i, h: (0, h)),
                  pl.BlockSpec((None, 2, nkeys, half), lambda i, h: (h, 0, 0, 0)),
                  slab_spec, slab_spec],
        out_specs=[out_spec, out_spec, slab_spec, slab_spec],
        out_shape=[jax.ShapeDtypeStruct((heads * PEER_TOPK, t // V7X_LANES, V7X_LANES), jnp.int32),
                   jax.ShapeDtypeStruct((heads * PEER_TOPK, t // V7X_LANES, V7X_LANES), F32),
                   jax.ShapeDtypeStruct(eu.shape, BF16), jax.ShapeDtypeStruct(ev.shape, BF16)],
        scratch_shapes=[pltpu.VMEM((2, groups * _pitch(nkeys), V7X_LANES), F32)],
        compiler_params=_params(("arbitrary", "arbitrary"), est),
        name="peer_route",
    )(u, wq, keys, eu, ev)


def _gate_matrix_kernel(idx_ref, gate_ref, o_ref, w0_ref, w1_ref):
    step = pl.program_id(0)

    @pl.when(step == 0)
    def _():
        w1_ref[...] = jnp.zeros_like(w1_ref)

    @pl.when(step % 2 == 0)
    def _():
        _gate_matrix_step(idx_ref, gate_ref, o_ref, w0_ref, w1_ref)

    @pl.when(step % 2 == 1)
    def _():
        _gate_matrix_step(idx_ref, gate_ref, o_ref, w1_ref, w0_ref)


def _gate_matrix_step(idx_ref, gate_ref, o_ref, cur_ref, prev_ref):
    tw, nsel = idx_ref.shape
    pitch = _pitch(PEER_NKEYS)
    key = lax.broadcasted_iota(jnp.int32, (PEER_NKEYS, nsel), 0)
    nt = (((1,), (1,)), ((), ()))
    zero = jnp.zeros((PEER_NKEYS, nsel), BF16)
    n_pairs = tw // 2
    planes_per_pair = PEER_NKEYS // n_pairs
    for p in range(n_pairs):
        a_hot, b_hot = [], []
        for t in (2 * p, 2 * p + 1):
            idx = idx_ref[t:t + 1, :]
            gate = gate_ref[t:t + 1, :]
            a_hot.append(jnp.where((idx >> NKEYS_SHIFT) == key, gate, 0.0).astype(BF16))
            b_hot.append(jnp.where((idx & (PEER_NKEYS - 1)) == key, 1.0, 0.0).astype(BF16))
        lhs = jnp.concatenate(a_hot, axis=1)
        rhs = jnp.concatenate([jnp.concatenate([b_hot[0], zero], axis=1),
                               jnp.concatenate([zero, b_hot[1]], axis=1)], axis=0)
        planes = lax.dot_general(lhs, rhs, nt, preferred_element_type=F32)
        for k in range(2):
            row = (2 * p + k) * pitch
            cur_ref[row:row + PEER_NKEYS, :] = planes[:, k * PEER_NKEYS:(k + 1) * PEER_NKEYS]
        for a in range(p * planes_per_pair, (p + 1) * planes_per_pair):
            o_ref[:, a * PEER_NKEYS:(a + 1) * PEER_NKEYS] = (
                prev_ref[pl.ds(a, tw, stride=pitch), :].astype(o_ref.dtype))


def _gate_matrix(idx, gate, tw):
    t, nsel = idx.shape
    n_exp = PEER_NKEYS * PEER_NKEYS
    n_blocks = t // tw
    assert PEER_NKEYS % (tw // 2) == 0
    est = 2 * (2 * tw * nsel * 4 + tw * n_exp * 2) + 2 * tw * _pitch(PEER_NKEYS) * PEER_NKEYS * 4
    block_in = pl.BlockSpec((tw, nsel), lambda i: (jnp.minimum(i, n_blocks - 1), 0))
    return pl.pallas_call(
        _gate_matrix_kernel,
        grid=(n_blocks + 1,),
        in_specs=[block_in, block_in],
        out_specs=pl.BlockSpec((tw, n_exp), lambda i: (jnp.maximum(i - 1, 0), 0)),
        out_shape=jax.ShapeDtypeStruct((t, n_exp), BF16),
        scratch_shapes=[pltpu.VMEM((tw * _pitch(PEER_NKEYS), PEER_NKEYS), F32)] * 2,
        compiler_params=_params(("arbitrary",), est),
        name="peer_gate_matrix",
    )(idx, gate)


def _rows_in_place(ref, fn, rows):
    def body(c, carry):
        sl = pl.ds(pl.multiple_of(c * rows, rows), rows)
        ref[sl, :] = fn(ref[sl, :])
        return carry

    lax.fori_loop(0, ref.shape[0] // rows, body, 0)


def _peer_kernel(u_ref, eu_ref, ev_ref, w_ref, x_hbm, mod_ref, lg_ref, lb_ref, o_ref, sem, *, alpha, n_chunk, rows):
    i, j = pl.program_id(0), pl.program_id(1)
    tm = o_ref.shape[0]
    x_copy = pltpu.make_async_copy(x_hbm.at[pl.ds(pl.multiple_of(i * tm, tm), tm), :], o_ref, sem)

    @pl.when(j == 0)
    def _():
        x_copy.start()

    z = lax.dot_general(u_ref[...], eu_ref[...], (((1,), (1,)), ((), ())), preferred_element_type=F32)
    act = 0.5 * z * (1.0 + lax.erf(z * (2.0 ** -0.5)))
    h = (w_ref[...].astype(F32) * act).astype(BF16)

    @pl.when(j == 0)
    def _():
        x_copy.wait()
        _rows_in_place(o_ref, lambda x: alpha * x, rows)

    for c in range(0, o_ref.shape[1], n_chunk):
        o_ref[:, c:c + n_chunk] += (jnp.dot(h, ev_ref[:, c:c + n_chunk], preferred_element_type=F32)
                                    * mod_ref[5:6, c:c + n_chunk])

    @pl.when(j == pl.num_programs(1) - 1)
    def _():
        _rows_in_place(o_ref, lambda r: _layer_norm(r, lg_ref[...], lb_ref[...]), rows)


def _peer(u, eu, ev, w, x, mod, ln_g, ln_b, alpha, seq, tm, te, n_chunk, rows):
    t, d = u.shape
    n_exp = ev.shape[0]
    assert seq % tm == 0
    once = pl.Buffered(1)
    vec = pl.BlockSpec((1, d), lambda i, j: (0, 0))
    est = tm * d * 2 + tm * d * 4 + 2 * (2 * te * d * 2 + tm * te * 2) + 4 * tm * te * 4 + tm * n_chunk * 4
    return pl.pallas_call(
        functools.partial(_peer_kernel, alpha=alpha, n_chunk=n_chunk, rows=rows),
        grid=(t // tm, n_exp // te),
        in_specs=[pl.BlockSpec((tm, d), lambda i, j: (i, 0), pipeline_mode=once),
                  pl.BlockSpec((te, d), lambda i, j: (j, 0)),
                  pl.BlockSpec((te, d), lambda i, j: (j, 0)),
                  pl.BlockSpec((tm, te), lambda i, j: (i, j)),
                  pl.BlockSpec(memory_space=pl.ANY),
                  pl.BlockSpec((None, 6, d), lambda i, j: ((i * tm) // seq, 0, 0)),
                  vec, vec],
        out_specs=pl.BlockSpec((tm, d), lambda i, j: (i, 0), pipeline_mode=once),
        out_shape=jax.ShapeDtypeStruct((t, d), F32),
        scratch_shapes=[pltpu.SemaphoreType.DMA(())],
        compiler_params=_params(("arbitrary", "arbitrary"), est),
        name="peer_experts",
    )(u, eu, ev, w, x, mod, ln_g.reshape(1, -1), ln_b.reshape(1, -1))


def _filter_features(seq):
    t = jnp.linspace(0.0, 1.0, seq, dtype=F32)[:, None]
    bands = (FILTER_EMB - 1) // 2
    w = 2.0 * math.pi * jnp.arange(seq, dtype=F32)[:, None] / seq
    f = jnp.linspace(1e-4, bands - 1, bands, dtype=F32)[None, :]
    z = jnp.concatenate([t, jnp.cos(f * w), -jnp.sin(f * w)], axis=-1)
    return jnp.pad(z, ((0, 0), (0, FILTER_PAD - FILTER_EMB)))


@functools.lru_cache(maxsize=None)
def _dft_matrices(seq):
    n2 = 2 * seq
    f = np.arange(seq, dtype=np.int64)[:, None]
    n = np.arange(seq, dtype=np.int64)[None, :]
    ang = ((f * n) % n2).astype(np.float64) * (2.0 * math.pi / n2)
    sin = np.where(f == 0, (1 - 2 * (n % 2)).astype(np.float64), np.sin(ang))
    return np.concatenate([np.cos(ang), sin], axis=0).astype(np.float32)


def kernel(x, c, ctx, c_ctx, w_ada, b_ada, w_in, conv_w, conv_b, filt_w1, filt_b1, filt_w2, filt_b2,
           filt_w3, filt_b3, filt_w4, filt_freq, hy_skip, na_rpb, hy_norm_g, na_norm_g, w_out,
           ln1_g, ln1_b, peer_wq, peer_keys, peer_u, peer_v, ln2_g, ln2_b):
    depth = w_ada.shape[0]
    assert depth == 1, "single-layer block: the context stream is never updated"
    bsz, seq, d = x.shape
    rows = seq // GRID_W
    kh = min(NA_KH, rows)
    assert kh == NA_KH and seq % GRID_W == 0
    alpha = (2.0 * depth) ** 0.25
    d_na = d - D_HY
    off_q = 3 * D_HY
    off_k = off_q + d_na

    n_cond = 8
    cond = jnp.concatenate([c, c_ctx[None], jnp.zeros((n_cond - bsz - 1, d), F32)], axis=0)
    mod = _ada(cond, w_ada[0], b_ada[0]).reshape(n_cond, 6, d)

    w_in_b = w_in[0].astype(BF16)
    assert off_q == 3 * d_na and off_k == 2 * (2 * d_na)
    (z_hy,) = _proj(x, mod, lambda b: b, w_in_b, off_q, 0, F32, PROJ_TM)
    (qkv,) = _proj(x, mod, lambda b: b, w_in_b, 3 * d_na, 1, BF16, PROJ_TM)
    (kv_ctx,) = _proj(ctx, mod, lambda b: bsz, w_in_b, 2 * d_na, 2, BF16, CTX_TM)

    deltas = jnp.abs(jnp.linspace(math.log(DECAY_TARGET) / SLOW_DECAY_PCT,
                                  math.log(DECAY_TARGET) / FAST_DECAY_PCT, D_HY, dtype=F32))[None, :]
    padw = lambda a: jnp.pad(a, ((0, FILTER_PAD - a.shape[0]), (0, FILTER_PAD - a.shape[1])))
    padv = lambda a: jnp.pad(a, ((0, FILTER_PAD - a.shape[0]),))
    w4 = jnp.pad(filt_w4[0], ((0, FILTER_PAD - filt_w4.shape[1]), (0, 0)))
    freq = jnp.pad(filt_freq[0], ((0, 0), (0, FILTER_PAD - filt_freq.shape[2])))
    ksum, kdiff = _hyena_filters(_filter_features(seq), padw(filt_w1[0]), padv(filt_b1[0]), padw(filt_w2[0]),
                                 padv(filt_b2[0]), padw(filt_w3[0]), padv(filt_b3[0]), w4, freq, deltas, HYENA_TC)
    ffwd = jnp.asarray(_dft_matrices(seq)).astype(BF16)
    finv = ffwd.T
    kp, kq = _spectrum(ffwd, ksum, kdiff, HYENA_TC)
    hy_out = _hyena(z_hy, conv_w[0], conv_b[0], ffwd, finv, kp, kq, hy_skip[0], HYENA_TC)

    bands = _bias_bands(na_rpb[0], kh)
    band_of_row = lambda r: jnp.clip(r - kh // 2, 0, rows - kh) - r + (NA_KH - 1)
    na_out, w_out_b, wq_b = _attention(qkv, kv_ctx, bands, band_of_row, rows, kh, [w_out[0], peer_wq[0]])

    x_mid, u_peer = _merge(hy_out, na_out, x, mod, w_out_b, hy_norm_g[0], na_norm_g[0],
                           ln1_g[0], ln1_b[0], alpha, MERGE_TM)

    t = bsz * seq
    u2 = u_peer.reshape(t, d)
    idx_t, gate_t, eu_b, ev_b = _route(u2, wq_b, peer_keys[0], peer_u[0], peer_v[0], ROUTE_TM)
    w_gate = _gate_matrix(idx_t.reshape(-1, t).T, gate_t.reshape(-1, t).T, GATE_TW)
    out = _peer(u2, eu_b, ev_b, w_gate, x_mid.reshape(t, d), mod, ln2_g[0], ln2_b[0], alpha, seq,
                PEER_TM, PEER_TE, PEER_CHUNK, PEER_LN_ROWS)
    return out.reshape(bsz, seq, d)
```

```python
import functools
import math

import numpy as np
import jax
import jax.numpy as jnp
from jax import lax
from jax.experimental import pallas as pl
from jax.experimental.pallas import tpu as pltpu

F32 = jnp.float32
BF16 = jnp.bfloat16
HIGHEST = lax.Precision.HIGHEST

GRID_W = 64
HEAD_DIM = 64
D_HY = 1024
HY_ORDER = 2
FILTER_EMB = 33
FILTER_PAD = 128
DECAY_TARGET = 1e-2
FAST_DECAY_PCT = 0.3
SLOW_DECAY_PCT = 1.5
NA_KH = 8
NA_KW = 16
PEER_HEADS = 8
PEER_NKEYS = 128
NKEYS_SHIFT = PEER_NKEYS.bit_length() - 1
assert 1 << NKEYS_SHIFT == PEER_NKEYS
PEER_TOPK = 16
LN_EPS = 1e-5
NEG_INF = -1e30

V7X_VMEM_BYTES = 64 * 1024 * 1024
V7X_LANES = 128
V7X_MXU_DEPTH = 256
NA_GROUP = V7X_MXU_DEPTH // HEAD_DIM


MIB = 1024 * 1024
VMEM_TEMPORARIES = 12 * MIB
VMEM_RESERVED = 6 * MIB

ADA_TN = 2048
PROJ_TM = 512
CTX_TM = 256
HYENA_TC = 256
NA_ROWS = 2
MERGE_TM = 256
ROUTE_TM = 1024
GATE_TW = 128
PEER_TM, PEER_TE, PEER_CHUNK = 2048, 512, 512
FINAL_TM = 512


def _vmem_limit(estimate_bytes):
    return int(min(estimate_bytes + VMEM_TEMPORARIES, V7X_VMEM_BYTES - VMEM_RESERVED))


def _params(semantics, estimate_bytes):
    return pltpu.CompilerParams(dimension_semantics=semantics, vmem_limit_bytes=_vmem_limit(estimate_bytes))


def _ada_kernel(cond_ref, w_ref, b_ref, o_ref):
    s = jax.nn.silu(cond_ref[...])
    o_ref[...] = jnp.dot(s, w_ref[...], precision=HIGHEST, preferred_element_type=F32) + b_ref[...]


def _ada(cond, w, b):
    n, d = cond.shape
    tn = ADA_TN
    return pl.pallas_call(
        _ada_kernel,
        grid=(w.shape[1] // tn,),
        in_specs=[pl.BlockSpec((n, d), lambda j: (0, 0)),
                  pl.BlockSpec((d, tn), lambda j: (0, j)),
                  pl.BlockSpec((1, tn), lambda j: (0, j))],
        out_specs=pl.BlockSpec((n, tn), lambda j: (0, j)),
        out_shape=jax.ShapeDtypeStruct((n, w.shape[1]), F32),
        compiler_params=_params(("arbitrary",), 2 * d * tn * 4),
        name="ada",
    )(cond, w, b.reshape(1, -1))


def _rider_specs(riders, n_steps, step_of):
    specs = []
    for w in riders:
        slab = w.shape[0] // n_steps
        assert slab * n_steps == w.shape[0] and slab % 16 == 0
        specs.append(pl.BlockSpec((slab, w.shape[1]), lambda *g: (step_of(*g), 0)))
    return specs


def _rider_bytes(riders, n_steps):
    return sum(2 * (w.size // n_steps) * 6 for w in riders)


def _convert_riders(src_refs, dst_refs):
    for src_ref, dst_ref in zip(src_refs, dst_refs):
        dst_ref[...] = src_ref[...].astype(dst_ref.dtype)


def _proj_kernel(x_ref, mod_ref, w_ref, *rest, n_riders):
    o_ref = rest[n_riders]
    _convert_riders(rest[:n_riders], rest[n_riders + 1:])
    u = x_ref[...] * (1.0 + mod_ref[1:2, :]) + mod_ref[0:1, :]
    o_ref[...] = jnp.dot(u.astype(BF16), w_ref[...], preferred_element_type=F32).astype(o_ref.dtype)


def _proj(x, mod, mod_row, w, n, col_block, out_dtype, tm, riders=()):
    bsz, seq, d = x.shape
    n_i = seq // tm
    slabs = _rider_specs(riders, bsz * n_i, lambda b, i: b * n_i + i)
    est = d * n * 2 + 2 * tm * d * 4 + 2 * tm * n * 4 + tm * n * 4 + _rider_bytes(riders, bsz * n_i)
    return pl.pallas_call(
        functools.partial(_proj_kernel, n_riders=len(riders)),
        grid=(bsz, n_i),
        in_specs=[pl.BlockSpec((None, tm, d), lambda b, i: (b, i, 0)),
                  pl.BlockSpec((None, 6, d), lambda b, i: (mod_row(b), 0, 0)),
                  pl.BlockSpec((d, n), lambda b, i: (0, col_block), pipeline_mode=pl.Buffered(1))] + slabs,
        out_specs=[pl.BlockSpec((None, tm, n), lambda b, i: (b, i, 0))] + slabs,
        out_shape=[jax.ShapeDtypeStruct((bsz, seq, n), out_dtype)]
        + [jax.ShapeDtypeStruct(r.shape, BF16) for r in riders],
        compiler_params=_params(("arbitrary", "arbitrary"), est),
        name="proj",
    )(x, mod, w, *riders)


def _filter_kernel(z_ref, w1_ref, b1_ref, w2_ref, b2_ref, w3_ref, b3_ref, fr_ref, w4f_ref, w4b_ref, dl_ref,
                   ks_ref, kd_ref, h_ref):
    dot = functools.partial(jnp.dot, precision=HIGHEST, preferred_element_type=F32)

    @pl.when((pl.program_id(0) == 0) & (pl.program_id(1) == 0))
    def _():
        h = jnp.sin(fr_ref[0:1, :] * (dot(z_ref[...], w1_ref[...]) + b1_ref[...]))
        h = jnp.sin(fr_ref[1:2, :] * (dot(h, w2_ref[...]) + b2_ref[...]))
        h_ref[...] = jnp.sin(fr_ref[2:3, :] * (dot(h, w3_ref[...]) + b3_ref[...]))

    h = h_ref[...]
    decay = jnp.exp(-z_ref[:, 0:1] * dl_ref[...])
    hf = dot(h, w4f_ref[...]) * decay
    hb = dot(h, w4b_ref[...]) * decay
    norm = jnp.sum(jnp.abs(hf) + jnp.abs(hb), axis=0, keepdims=True)
    hf = hf / norm
    hb = hb / norm
    row = lax.broadcasted_iota(jnp.int32, hb.shape, 0)
    hb = jnp.where(row == 0, 0.0, hb)
    ks_ref[...] = hf + hb
    kd_ref[...] = hb - hf


def _hyena_filters(zfeat, w1, b1, w2, b2, w3, b3, w4, freq, deltas, tc):
    seq = zfeat.shape[0]
    fo = w2.shape[0]
    nct = D_HY // tc
    full = lambda a: pl.BlockSpec(a.shape, lambda o, j: (0,) * a.ndim)
    args = (zfeat, w1, b1.reshape(1, -1), w2, b2.reshape(1, -1), w3, b3.reshape(1, -1), freq)
    out_spec = pl.BlockSpec((None, seq, tc), lambda o, j: (o, 0, j))
    return pl.pallas_call(
        _filter_kernel,
        grid=(HY_ORDER, nct),
        in_specs=[full(a) for a in args] + [
            pl.BlockSpec((fo, tc), lambda o, j: (0, (2 * o) * nct + j)),
            pl.BlockSpec((fo, tc), lambda o, j: (0, (2 * o + 1) * nct + j)),
            pl.BlockSpec((1, tc), lambda o, j: (0, j))],
        out_specs=[out_spec, out_spec],
        out_shape=[jax.ShapeDtypeStruct((HY_ORDER, seq, D_HY), F32)] * 2,
        scratch_shapes=[pltpu.VMEM((seq, fo), F32)],
        compiler_params=_params(("arbitrary", "arbitrary"), 12 * seq * tc * 4),
        name="hyena_filter",
    )(*args, w4, w4, deltas)


def _spectrum_kernel(ff_ref, ks_ref, kd_ref, kp_ref, kq_ref):
    seq = ks_ref.shape[0]
    ks = ks_ref[...].astype(BF16)
    kd = kd_ref[...].astype(BF16)
    p = jnp.dot(ff_ref[0:seq, :], ks, preferred_element_type=F32)
    q = jnp.dot(ff_ref[seq:2 * seq, :], kd, preferred_element_type=F32)
    nyq = jnp.dot(ff_ref[seq:seq + 16, :], ks, preferred_element_type=F32)[0:1, :]
    row = lax.broadcasted_iota(jnp.int32, p.shape, 0)
    wf = jnp.where(row == 0, 0.5 / seq, 1.0 / seq)
    kp_ref[...] = p * wf
    kq_ref[...] = jnp.where(row == 0, nyq, q) * wf


def _spectrum(ffwd, ksum, kdiff, tc):
    seq = ksum.shape[1]
    spec = pl.BlockSpec((None, seq, tc), lambda o, j: (o, 0, j))
    return pl.pallas_call(
        _spectrum_kernel,
        grid=(HY_ORDER, D_HY // tc),
        in_specs=[pl.BlockSpec(ffwd.shape, lambda o, j: (0, 0)), spec, spec],
        out_specs=[spec, spec],
        out_shape=[jax.ShapeDtypeStruct(ksum.shape, F32)] * 2,
        compiler_params=_params(("arbitrary", "arbitrary"), ffwd.size * 2 + 12 * seq * tc * 4),
        name="hyena_spectrum",
    )(ffwd, ksum, kdiff)


def _short_conv(z, w, b):
    seq = z.shape[0]
    row = lax.broadcasted_iota(jnp.int32, z.shape, 0)
    prev = jnp.where(row == 0, 0.0, pltpu.roll(z, 1, 0))
    nxt = jnp.where(row == seq - 1, 0.0, pltpu.roll(z, seq - 1, 0))
    return prev * w[0:1, :] + z * w[1:2, :] + nxt * w[2:3, :] + b


def _dft_mul(u, ff_ref, kp_ref, kq_ref, o_ref):
    seq = u.shape[0]
    a = jnp.dot(ff_ref[...], u.astype(BF16), preferred_element_type=F32)
    p, q = a[0:seq], a[seq:2 * seq]
    kp, kq = kp_ref[...], kq_ref[...]
    row0 = lax.broadcasted_iota(jnp.int32, p.shape, 0) == 0
    o_ref[0:seq, :] = jnp.where(row0, kp * p, kp * p + kq * q).astype(o_ref.dtype)
    o_ref[seq:2 * seq, :] = jnp.where(row0, kq * q, kp * q - kq * p).astype(o_ref.dtype)


def _fwd_conv_kernel(z_ref, cw_ref, cb_ref, ff_ref, kp_ref, kq_ref, o_ref):
    _dft_mul(_short_conv(z_ref[...], cw_ref[...], cb_ref[...]), ff_ref, kp_ref, kq_ref, o_ref)


def _fwd_plain_kernel(u_ref, ff_ref, kp_ref, kq_ref, o_ref):
    _dft_mul(u_ref[...], ff_ref, kp_ref, kq_ref, o_ref)


def _inv_first_kernel(y_ref, fi_ref, zv_ref, zx_ref, cwv_ref, cbv_ref, cwx_ref, cbx_ref, skip_ref, o_ref):
    v = _short_conv(zv_ref[...], cwv_ref[...], cbv_ref[...])
    x1 = _short_conv(zx_ref[...], cwx_ref[...], cbx_ref[...])
    y = jnp.dot(fi_ref[...], y_ref[...], preferred_element_type=F32)
    o_ref[...] = x1 * (y + v * skip_ref[...])


def _inv_second_kernel(y_ref, fi_ref, g_ref, zx_ref, cwx_ref, cbx_ref, skip_ref, o_ref):
    x2 = _short_conv(zx_ref[...], cwx_ref[...], cbx_ref[...])
    y = jnp.dot(fi_ref[...], y_ref[...], preferred_element_type=F32)
    o_ref[...] = x2 * (y + g_ref[...] * skip_ref[...])


def _hyena(z, conv_w, conv_b, ffwd, finv, kp, kq, skip, tc):
    bsz, seq, _ = z.shape
    nct = D_HY // tc
    sem = ("arbitrary", "arbitrary")
    zcol = lambda g: pl.BlockSpec((None, seq, tc), lambda b, j: (b, 0, g * nct + j))
    wcol = lambda g: pl.BlockSpec((3, tc), lambda b, j: (0, g * nct + j))
    bcol = lambda g: pl.BlockSpec((1, tc), lambda b, j: (0, g * nct + j))
    act = pl.BlockSpec((None, seq, tc), lambda b, j: (b, 0, j))
    spec2 = pl.BlockSpec((None, 2 * seq, tc), lambda b, j: (b, 0, j))
    kspec = lambda o: pl.BlockSpec((None, seq, tc), lambda b, j: (o, 0, j))
    sspec = lambda o: pl.BlockSpec((None, 1, tc), lambda b, j: (o, 0, j))
    mat = pl.BlockSpec(ffwd.shape, lambda b, j: (0, 0))
    mati = pl.BlockSpec(finv.shape, lambda b, j: (0, 0))
    est = ffwd.size * 2 + 14 * seq * tc * 4
    cb = conv_b.reshape(1, -1)
    skip3 = skip.reshape(HY_ORDER, 1, D_HY)
    spec_shape = jax.ShapeDtypeStruct((bsz, 2 * seq, D_HY), BF16)
    act_shape = jax.ShapeDtypeStruct((bsz, seq, D_HY), F32)

    y1 = pl.pallas_call(
        _fwd_conv_kernel, grid=(bsz, nct),
        in_specs=[zcol(0), wcol(0), bcol(0), mat, kspec(0), kspec(0)],
        out_specs=spec2, out_shape=spec_shape, compiler_params=_params(sem, est), name="hyena_fwd1",
    )(z, conv_w, cb, ffwd, kp, kq)
    g = pl.pallas_call(
        _inv_first_kernel, grid=(bsz, nct),
        in_specs=[spec2, mati, zcol(0), zcol(1), wcol(0), bcol(0), wcol(1), bcol(1), sspec(0)],
        out_specs=act, out_shape=act_shape, compiler_params=_params(sem, est), name="hyena_inv1",
    )(y1, finv, z, z, conv_w, cb, conv_w, cb, skip3)
    y2 = pl.pallas_call(
        _fwd_plain_kernel, grid=(bsz, nct),
        in_specs=[act, mat, kspec(1), kspec(1)],
        out_specs=spec2, out_shape=spec_shape, compiler_params=_params(sem, est), name="hyena_fwd2",
    )(g, ffwd, kp, kq)
    return pl.pallas_call(
        _inv_second_kernel, grid=(bsz, nct),
        in_specs=[spec2, mati, act, zcol(2), wcol(2), bcol(2), sspec(1)],
        out_specs=act, out_shape=act_shape, compiler_params=_params(sem, est), name="hyena_inv2",
    )(y2, finv, g, z, conv_w, cb, skip3)


def _bias_kernel(rpb_ref, o_ref, *, kh):
    h = pl.program_id(0)
    n_r, n_c = 2 * NA_KH - 1, 2 * NA_KW - 1
    cq = lax.broadcasted_iota(jnp.int32, (GRID_W, GRID_W), 0)
    ck = lax.broadcasted_iota(jnp.int32, (GRID_W, GRID_W), 1)
    rel = ck - jnp.clip(cq - NA_KW // 2, 0, GRID_W - NA_KW)
    ok = (rel >= 0) & (rel < NA_KW)
    dcol = jnp.clip(ck - cq + NA_KW - 1, 0, n_c - 1)
    tables = []
    for r in range(n_r):
        acc = jnp.zeros((GRID_W, GRID_W), F32)
        for c in range(n_c):
            acc = jnp.where(dcol == c, rpb_ref[(h * n_r + r) * n_c + c], acc)
        tables.append(jnp.where(ok, acc, NEG_INF))
    for off in range(o_ref.shape[0]):
        for i in range(kh):
            o_ref[off, :, i * GRID_W:(i + 1) * GRID_W] = tables[off + i]


def _bias_bands(rpb, kh):
    heads = rpb.shape[0]
    n_off = 2 * NA_KH - kh
    return pl.pallas_call(
        functools.partial(_bias_kernel, kh=kh),
        grid=(heads,),
        in_specs=[pl.BlockSpec(memory_space=pltpu.SMEM)],
        out_specs=pl.BlockSpec((None, n_off, GRID_W, kh * GRID_W), lambda h: (h, 0, 0, 0)),
        out_shape=jax.ShapeDtypeStruct((heads, n_off, GRID_W, kh * GRID_W), F32),
        compiler_params=_params(("arbitrary",), 4 << 20),
        name="na_bias",
    )(rpb.reshape(-1))


def _attn_kernel(q_ref, k_ref, v_ref, kc_ref, vc_ref, *rest, rows, kh, n_rows, n_riders):
    bias_refs, rest = rest[:n_rows], rest[n_rows:]
    o_ref = rest[n_riders]
    _convert_riders(rest[:n_riders], rest[n_riders + 1:])
    first_row = n_rows * pl.program_id(1)
    nk = kh * GRID_W
    gw = NA_GROUP * HEAD_DIM
    lane = lax.broadcasted_iota(jnp.int32, (1, gw), 1)
    head_of_lane = [(lane >= e * HEAD_DIM) & (lane < (e + 1) * HEAD_DIM) for e in range(NA_GROUP)]
    nt = (((1,), (1,)), ((), ()))
    starts = [pl.multiple_of(jnp.clip(first_row + i - kh // 2, 0, rows - kh) * GRID_W, GRID_W)
              for i in range(n_rows)]
    per_row = NA_GROUP * GRID_W
    zero = jnp.zeros((GRID_W, gw), BF16)
    for hg in range(q_ref.shape[1] // gw):
        cols = slice(hg * gw, (hg + 1) * gw)
        qp = q_ref[:, cols] * jnp.asarray(HEAD_DIM ** -0.5, BF16)
        qm = jnp.concatenate([jnp.where(head_of_lane[e], qp[i * GRID_W:(i + 1) * GRID_W], zero)
                              for i in range(n_rows) for e in range(NA_GROUP)], axis=0)
        s_ctx = lax.dot_general(qm, kc_ref[:, cols], nt, preferred_element_type=F32)
        m_ctx = jnp.max(s_ctx, axis=-1, keepdims=True)
        s_lat, m_all = [], []
        for i in range(n_rows):
            blk = slice(i * per_row, (i + 1) * per_row)
            bias = jnp.concatenate([bias_refs[i][hg * NA_GROUP + e] for e in range(NA_GROUP)], axis=0)
            s = lax.dot_general(qm[blk], k_ref[pl.ds(starts[i], nk), cols], nt, preferred_element_type=F32) + bias
            s_lat.append(s)
            m_all.append(jnp.maximum(jnp.max(s, axis=-1, keepdims=True), m_ctx[blk]))
        p_ctx = jnp.exp(s_ctx - jnp.concatenate(m_all, axis=0))
        o_ctx = jnp.dot(p_ctx.astype(BF16), vc_ref[:, cols], preferred_element_type=F32)
        l_ctx = jnp.sum(p_ctx, axis=-1, keepdims=True)
        for i in range(n_rows):
            blk = slice(i * per_row, (i + 1) * per_row)
            p = jnp.exp(s_lat[i] - m_all[i])
            o = jnp.dot(p.astype(BF16), v_ref[pl.ds(starts[i], nk), cols], preferred_element_type=F32) + o_ctx[blk]
            o = o / (jnp.sum(p, axis=-1, keepdims=True) + l_ctx[blk])
            out = o[0:GRID_W]
            for e in range(1, NA_GROUP):
                out = jnp.where(head_of_lane[e], o[e * GRID_W:(e + 1) * GRID_W], out)
            o_ref[i * GRID_W:(i + 1) * GRID_W, cols] = out


def _attention(qkv, kv_ctx, bands, band_of_row, rows, kh, n_rows, riders):
    bsz, seq, d3 = qkv.shape
    dn = d3 // 3
    ctx_len = kv_ctx.shape[1]
    heads, _, _, nk = bands.shape
    assert rows % n_rows == 0
    n_steps = bsz * (rows // n_rows)
    slab_specs = _rider_specs(riders, n_steps, lambda b, r: b * (rows // n_rows) + r)
    est = (2 * (2 * seq * dn * 2 + 2 * ctx_len * dn * 2 + n_rows * heads * GRID_W * nk * 4 + n_rows * GRID_W * dn * 6)
           + _rider_bytes(riders, n_steps))
    band = lambda i: pl.BlockSpec((heads, None, GRID_W, nk), lambda b, r: (0, band_of_row(n_rows * r + i), 0, 0))
    return pl.pallas_call(
        functools.partial(_attn_kernel, rows=rows, kh=kh, n_rows=n_rows, n_riders=len(riders)),
        grid=(bsz, rows // n_rows),
        in_specs=[pl.BlockSpec((None, n_rows * GRID_W, dn), lambda b, r: (b, r, 0)),
                  pl.BlockSpec((None, seq, dn), lambda b, r: (b, 0, 1)),
                  pl.BlockSpec((None, seq, dn), lambda b, r: (b, 0, 2)),
                  pl.BlockSpec((None, ctx_len, dn), lambda b, r: (b, 0, 0)),
                  pl.BlockSpec((None, ctx_len, dn), lambda b, r: (b, 0, 1)),
                  *[band(i) for i in range(n_rows)]] + slab_specs,
        out_specs=[pl.BlockSpec((None, n_rows * GRID_W, dn), lambda b, r: (b, r, 0))] + slab_specs,
        out_shape=[jax.ShapeDtypeStruct((bsz, seq, dn), F32)]
        + [jax.ShapeDtypeStruct(w.shape, BF16) for w in riders],
        compiler_params=_params(("arbitrary", "arbitrary"), est),
        name="na_attention",
    )(qkv, qkv, qkv, kv_ctx, kv_ctx, *([bands] * n_rows), *riders)


def _rms(x, g):
    return x * lax.rsqrt(jnp.mean(x * x, axis=-1, keepdims=True) + LN_EPS) * g


def _layer_norm(x, g, b):
    xc = x - jnp.mean(x, axis=-1, keepdims=True)
    var = jnp.mean(xc * xc, axis=-1, keepdims=True)
    return xc * lax.rsqrt(var + LN_EPS) * g + b


def _merge_kernel(hy_ref, na_ref, x_ref, mod_ref, w_ref, ghy_ref, gna_ref, lg_ref, lb_ref, xo_ref, uo_ref, *, alpha):
    dh = hy_ref.shape[1]
    hy = _rms(hy_ref[...], ghy_ref[...]).astype(BF16)
    na = _rms(na_ref[...], gna_ref[...]).astype(BF16)
    y = (jnp.dot(hy, w_ref[0:dh, :], preferred_element_type=F32)
         + jnp.dot(na, w_ref[dh:, :], preferred_element_type=F32))
    xn = _layer_norm(alpha * x_ref[...] + mod_ref[2:3, :] * y, lg_ref[...], lb_ref[...])
    xo_ref[...] = xn
    uo_ref[...] = (xn * (1.0 + mod_ref[4:5, :]) + mod_ref[3:4, :]).astype(uo_ref.dtype)


def _merge(hy, na, x, mod, w_out, g_hy, g_na, ln_g, ln_b, alpha, tm):
    bsz, seq, d = x.shape
    dh, dn = hy.shape[2], na.shape[2]
    row = lambda a: a.reshape(1, -1)
    vec = lambda n: pl.BlockSpec((1, n), lambda b, i: (0, 0))
    tile = lambda n: pl.BlockSpec((None, tm, n), lambda b, i: (b, i, 0))
    est = d * d * 2 + 2 * tm * (dh + dn + 2 * d) * 4 + 2 * tm * d * 6 + 4 * tm * d * 4
    return pl.pallas_call(
        functools.partial(_merge_kernel, alpha=alpha),
        grid=(bsz, seq // tm),
        in_specs=[tile(dh), tile(dn), tile(d),
                  pl.BlockSpec((None, 6, d), lambda b, i: (b, 0, 0)),
                  pl.BlockSpec((d, d), lambda b, i: (0, 0)),
                  vec(dh), vec(dn), vec(d), vec(d)],
        out_specs=[tile(d), tile(d)],
        out_shape=[jax.ShapeDtypeStruct((bsz, seq, d), F32), jax.ShapeDtypeStruct((bsz, seq, d), BF16)],
        compiler_params=_params(("arbitrary", "arbitrary"), est),
        name="merge",
    )(hy, na, x, mod, w_out, row(g_hy), row(g_na), row(ln_g), row(ln_b))


def _first(a, b):
    return (a[0] > b[0]) | ((a[0] == b[0]) & (a[1] < b[1]))


def _order_pair(xs, i, j):
    a, b = xs[i], xs[j]
    f = _first(a, b)
    xs[i] = (jnp.maximum(a[0], b[0]),) + tuple(jnp.where(f, p, r) for p, r in zip(a[1:], b[1:]))
    xs[j] = (jnp.minimum(a[0], b[0]),) + tuple(jnp.where(f, r, p) for p, r in zip(a[1:], b[1:]))


def _bitonic_merge(xs):
    n = len(xs)
    j = n // 2
    while j >= 1:
        for i in range(n):
            if i & j == 0:
                _order_pair(xs, i, i | j)
        j //= 2
    return xs


def _sort(xs):
    n = len(xs)
    p = 1
    while p < n:
        k = p
        while k >= 1:
            for j in range(k % p, n - k, 2 * k):
                for i in range(min(k, n - j - k)):
                    if (i + j) // (2 * p) == (i + j + k) // (2 * p):
                        _order_pair(xs, i + j, i + j + k)
            k //= 2
        p *= 2
    return xs


def _leading_half(a, b):
    out = []
    for x, y in zip(a, reversed(b)):
        f = _first(x, y)
        out.append((jnp.maximum(x[0], y[0]),) + tuple(jnp.where(f, p, r) for p, r in zip(x[1:], y[1:])))
    return out


def _sorted_top(items, k):
    runs = [_sort(items[i:i + k]) for i in range(0, len(items), k)]
    while len(runs) > 1:
        runs = [_bitonic_merge(_leading_half(runs[i], runs[i + 1])) for i in range(0, len(runs), 2)]
    return runs[0]


def _pitch(rows):
    return rows + 8


def _route_kernel(u_ref, wq_ref, keys_ref, eu_ref, ev_ref, idx_ref, gate_ref, eub_ref, evb_ref, s_ref):
    eub_ref[...] = eu_ref[...].astype(eub_ref.dtype)
    evb_ref[...] = ev_ref[...].astype(evb_ref.dtype)
    nkeys, half = keys_ref.shape[1], keys_ref.shape[2]
    k = PEER_TOPK
    groups = u_ref.shape[0] // V7X_LANES
    pitch = _pitch(nkeys)
    tile = (groups, V7X_LANES)
    q = jnp.dot(u_ref[...], wq_ref[...], preferred_element_type=F32)
    nt = (((1,), (1,)), ((), ()))

    def split(v):
        hi = v.astype(BF16)
        return hi, (v - hi.astype(F32)).astype(BF16)

    for p in range(2):
        k_hi, k_lo = split(keys_ref[p])
        q_hi, q_lo = split(q[:, p * half:(p + 1) * half])
        dot = lambda a, b: lax.dot_general(a, b, nt, preferred_element_type=F32)
        s = dot(k_hi, q_hi) + (dot(k_hi, q_lo) + dot(k_lo, q_hi))
        for g in range(groups):
            s_ref[p, g * pitch:g * pitch + nkeys, :] = s[:, g * V7X_LANES:(g + 1) * V7X_LANES]

    def scores(p):
        return [(s_ref[p, pl.ds(n, groups, stride=pitch), :], jnp.full(tile, float(n), F32)) for n in range(nkeys)]

    top_a = _sorted_top(scores(0), k)
    top_b = _sorted_top(scores(1), k)

    def cand(i, j):
        return (top_a[i][0] + top_b[j][0], jnp.full(tile, float(i * k + j), F32),
                top_a[i][1] * float(nkeys) + top_b[j][1])

    pad = (jnp.full(tile, -jnp.inf, F32), jnp.full(tile, float(k * k), F32), jnp.zeros(tile, F32))
    run = lambda i: [cand(i, j) for j in range(k // (i + 1))]
    tail = [cand(i, 0) for i in range(k // 2, k)]
    assert k == 16
    r0 = run(0)
    r1 = _bitonic_merge(run(1) + tail[::-1])
    r2 = _sort(run(2) + run(3) + run(4) + run(5) + run(6))
    r3 = run(7) + [pad] * (k - 2)
    best = _leading_half(_bitonic_merge(_leading_half(r0, r1)), _bitonic_merge(_leading_half(r2, r3)))

    m = functools.reduce(jnp.maximum, [c[0] for c in best])
    e = [jnp.exp(c[0] - m) for c in best]
    inv = 1.0 / functools.reduce(jnp.add, e)
    for n in range(k):
        idx_ref[n] = best[n][2].astype(jnp.int32)
        gate_ref[n] = e[n] * inv


def _route(u, wq, keys, eu, ev, tm):
    t, d = u.shape
    heads, _, nkeys, half = keys.shape
    groups = tm // V7X_LANES
    n_steps = (t // tm) * heads
    n_exp, de = eu.shape
    slab = n_exp // n_steps
    assert slab * n_steps == n_exp and slab % 16 == 0
    est = (2 * (tm * d * 2 + d * 2 * half * 2 + 2 * nkeys * half * 4) + 8 * nkeys * tm * 4
           + 2 * 2 * slab * de * 6)
    out_spec = pl.BlockSpec((PEER_TOPK, groups, V7X_LANES), lambda i, h: (h, i, 0))
    slab_spec = pl.BlockSpec((slab, de), lambda i, h: (i * heads + h, 0))
    return pl.pallas_call(
        _route_kernel,
        grid=(t // tm, heads),
        in_specs=[pl.BlockSpec((tm, d), lambda i, h: (i, 0)),
                  pl.BlockSpec((d, 2 * half), lambda i, h: (0, h)),
                  pl.BlockSpec((None, 2, nkeys, half), lambda i, h: (h, 0, 0, 0)),
                  slab_spec, slab_spec],
        out_specs=[out_spec, out_spec, slab_spec, slab_spec],
        out_shape=[jax.ShapeDtypeStruct((heads * PEER_TOPK, t // V7X_LANES, V7X_LANES), jnp.int32),
                   jax.ShapeDtypeStruct((heads * PEER_TOPK, t // V7X_LANES, V7X_LANES), F32),
                   jax.ShapeDtypeStruct(eu.shape, BF16), jax.ShapeDtypeStruct(ev.shape, BF16)],
        scratch_shapes=[pltpu.VMEM((2, groups * _pitch(nkeys), V7X_LANES), F32)],
        compiler_params=_params(("arbitrary", "arbitrary"), est),
        name="peer_route",
    )(u, wq, keys, eu, ev)


def _gate_matrix_kernel(idx_ref, gate_ref, o_ref, w0_ref, w1_ref):
    step = pl.program_id(0)

    @pl.when(step == 0)
    def _():
        w1_ref[...] = jnp.zeros_like(w1_ref)

    @pl.when(step % 2 == 0)
    def _():
        _gate_matrix_step(idx_ref, gate_ref, o_ref, w0_ref, w1_ref)

    @pl.when(step % 2 == 1)
    def _():
        _gate_matrix_step(idx_ref, gate_ref, o_ref, w1_ref, w0_ref)


def _gate_matrix_step(idx_ref, gate_ref, o_ref, cur_ref, prev_ref):
    tw, nsel = idx_ref.shape
    pitch = _pitch(PEER_NKEYS)
    key = lax.broadcasted_iota(jnp.int32, (PEER_NKEYS, nsel), 0)
    nt = (((1,), (1,)), ((), ()))
    zero = jnp.zeros((PEER_NKEYS, nsel), BF16)
    n_pairs = tw // 2
    planes_per_pair = PEER_NKEYS // n_pairs
    for p in range(n_pairs):
        a_hot, b_hot = [], []
        for t in (2 * p, 2 * p + 1):
            idx = idx_ref[t:t + 1, :]
            gate = gate_ref[t:t + 1, :]
            a_hot.append(jnp.where((idx >> NKEYS_SHIFT) == key, gate, 0.0).astype(BF16))
            b_hot.append(jnp.where((idx & (PEER_NKEYS - 1)) == key, 1.0, 0.0).astype(BF16))
        lhs = jnp.concatenate(a_hot, axis=1)
        rhs = jnp.concatenate([jnp.concatenate([b_hot[0], zero], axis=1),
                               jnp.concatenate([zero, b_hot[1]], axis=1)], axis=0)
        planes = lax.dot_general(lhs, rhs, nt, preferred_element_type=F32)
        for k in range(2):
            row = (2 * p + k) * pitch
            cur_ref[row:row + PEER_NKEYS, :] = planes[:, k * PEER_NKEYS:(k + 1) * PEER_NKEYS]
        for a in range(p * planes_per_pair, (p + 1) * planes_per_pair):
            o_ref[:, a * PEER_NKEYS:(a + 1) * PEER_NKEYS] = (
                prev_ref[pl.ds(a, tw, stride=pitch), :].astype(o_ref.dtype))


def _gate_matrix(idx, gate, tw):
    t, nsel = idx.shape
    n_exp = PEER_NKEYS * PEER_NKEYS
    n_blocks = t // tw
    assert PEER_NKEYS % (tw // 2) == 0
    est = 2 * (2 * tw * nsel * 4 + tw * n_exp * 2) + 2 * tw * _pitch(PEER_NKEYS) * PEER_NKEYS * 4
    block_in = pl.BlockSpec((tw, nsel), lambda i: (jnp.minimum(i, n_blocks - 1), 0))
    return pl.pallas_call(
        _gate_matrix_kernel,
        grid=(n_blocks + 1,),
        in_specs=[block_in, block_in],
        out_specs=pl.BlockSpec((tw, n_exp), lambda i: (jnp.maximum(i - 1, 0), 0)),
        out_shape=jax.ShapeDtypeStruct((t, n_exp), BF16),
        scratch_shapes=[pltpu.VMEM((tw * _pitch(PEER_NKEYS), PEER_NKEYS), F32)] * 2,
        compiler_params=_params(("arbitrary",), est),
        name="peer_gate_matrix",
    )(idx, gate)


def _peer_kernel(u_ref, eu_ref, ev_ref, w_ref, o_ref, *, n_chunk):
    @pl.when(pl.program_id(1) == 0)
    def _():
        o_ref[...] = jnp.zeros_like(o_ref)

    z = lax.dot_general(u_ref[...], eu_ref[...], (((1,), (1,)), ((), ())), preferred_element_type=F32)
    act = 0.5 * z * (1.0 + lax.erf(z * (2.0 ** -0.5)))
    h = (w_ref[...].astype(F32) * act).astype(BF16)
    for c in range(0, o_ref.shape[1], n_chunk):
        o_ref[:, c:c + n_chunk] += jnp.dot(h, ev_ref[:, c:c + n_chunk], preferred_element_type=F32)


def _final_ln_kernel(x_ref, f_ref, mod_ref, lg_ref, lb_ref, o_ref, *, alpha):
    r = alpha * x_ref[...] + mod_ref[5:6, :] * f_ref[...]
    o_ref[...] = _layer_norm(r, lg_ref[...], lb_ref[...])


def _final_ln(x, f, mod, ln_g, ln_b, alpha, seq, tm):
    t, d = x.shape
    tile = pl.BlockSpec((tm, d), lambda i: (i, 0))
    vec = pl.BlockSpec((1, d), lambda i: (0, 0))
    return pl.pallas_call(
        functools.partial(_final_ln_kernel, alpha=alpha),
        grid=(t // tm,),
        in_specs=[tile, tile, pl.BlockSpec((None, 6, d), lambda i: ((i * tm) // seq, 0, 0)), vec, vec],
        out_specs=tile,
        out_shape=jax.ShapeDtypeStruct((t, d), F32),
        compiler_params=_params(("arbitrary",), 8 * tm * d * 4),
        name="final_ln",
    )(x, f, mod, ln_g.reshape(1, -1), ln_b.reshape(1, -1))


def _peer(u, eu, ev, w, tm, te, n_chunk):
    t, d = u.shape
    n_exp = ev.shape[0]
    once = pl.Buffered(1)
    est = tm * d * 2 + tm * d * 4 + 2 * (2 * te * d * 2 + tm * te * 2) + 4 * tm * te * 4 + tm * n_chunk * 4
    return pl.pallas_call(
        functools.partial(_peer_kernel, n_chunk=n_chunk),
        grid=(t // tm, n_exp // te),
        in_specs=[pl.BlockSpec((tm, d), lambda i, j: (i, 0), pipeline_mode=once),
                  pl.BlockSpec((te, d), lambda i, j: (j, 0)),
                  pl.BlockSpec((te, d), lambda i, j: (j, 0)),
                  pl.BlockSpec((tm, te), lambda i, j: (i, j))],
        out_specs=pl.BlockSpec((tm, d), lambda i, j: (i, 0), pipeline_mode=once),
        out_shape=jax.ShapeDtypeStruct((t, d), F32),
        compiler_params=_params(("arbitrary", "arbitrary"), est),
        name="peer_experts",
    )(u, eu, ev, w)


def _filter_features(seq):
    t = jnp.linspace(0.0, 1.0, seq, dtype=F32)[:, None]
    bands = (FILTER_EMB - 1) // 2
    w = 2.0 * math.pi * jnp.arange(seq, dtype=F32)[:, None] / seq
    f = jnp.linspace(1e-4, bands - 1, bands, dtype=F32)[None, :]
    z = jnp.concatenate([t, jnp.cos(f * w), -jnp.sin(f * w)], axis=-1)
    return jnp.pad(z, ((0, 0), (0, FILTER_PAD - FILTER_EMB)))


@functools.lru_cache(maxsize=None)
def _dft_matrices(seq):
    n2 = 2 * seq
    f = np.arange(seq, dtype=np.int64)[:, None]
    n = np.arange(seq, dtype=np.int64)[None, :]
    ang = ((f * n) % n2).astype(np.float64) * (2.0 * math.pi / n2)
    sin = np.where(f == 0, (1 - 2 * (n % 2)).astype(np.float64), np.sin(ang))
    return np.concatenate([np.cos(ang), sin], axis=0).astype(np.float32)


def kernel(x, c, ctx, c_ctx, w_ada, b_ada, w_in, conv_w, conv_b, filt_w1, filt_b1, filt_w2, filt_b2,
           filt_w3, filt_b3, filt_w4, filt_freq, hy_skip, na_rpb, hy_norm_g, na_norm_g, w_out,
           ln1_g, ln1_b, peer_wq, peer_keys, peer_u, peer_v, ln2_g, ln2_b):
    depth = w_ada.shape[0]
    assert depth == 1, "single-layer block: the context stream is never updated"
    bsz, seq, d = x.shape
    rows = seq // GRID_W
    kh = min(NA_KH, rows)
    assert kh == NA_KH and seq % GRID_W == 0
    alpha = (2.0 * depth) ** 0.25
    d_na = d - D_HY
    off_q = 3 * D_HY
    off_k = off_q + d_na

    n_cond = 8
    cond = jnp.concatenate([c, c_ctx[None], jnp.zeros((n_cond - bsz - 1, d), F32)], axis=0)
    mod = _ada(cond, w_ada[0], b_ada[0]).reshape(n_cond, 6, d)

    w_in_b = w_in[0].astype(BF16)
    assert off_q == 3 * d_na and off_k == 2 * (2 * d_na)
    (z_hy,) = _proj(x, mod, lambda b: b, w_in_b, off_q, 0, F32, PROJ_TM)
    (qkv,) = _proj(x, mod, lambda b: b, w_in_b, 3 * d_na, 1, BF16, PROJ_TM)
    (kv_ctx,) = _proj(ctx, mod, lambda b: bsz, w_in_b, 2 * d_na, 2, BF16, CTX_TM)

    deltas = jnp.abs(jnp.linspace(math.log(DECAY_TARGET) / SLOW_DECAY_PCT,
                                  math.log(DECAY_TARGET) / FAST_DECAY_PCT, D_HY, dtype=F32))[None, :]
    padw = lambda a: jnp.pad(a, ((0, FILTER_PAD - a.shape[0]), (0, FILTER_PAD - a.shape[1])))
    padv = lambda a: jnp.pad(a, ((0, FILTER_PAD - a.shape[0]),))
    w4 = jnp.pad(filt_w4[0], ((0, FILTER_PAD - filt_w4.shape[1]), (0, 0)))
    freq = jnp.pad(filt_freq[0], ((0, 0), (0, FILTER_PAD - filt_freq.shape[2])))
    ksum, kdiff = _hyena_filters(_filter_features(seq), padw(filt_w1[0]), padv(filt_b1[0]), padw(filt_w2[0]),
                                 padv(filt_b2[0]), padw(filt_w3[0]), padv(filt_b3[0]), w4, freq, deltas, HYENA_TC)
    ffwd = jnp.asarray(_dft_matrices(seq)).astype(BF16)
    finv = ffwd.T
    kp, kq = _spectrum(ffwd, ksum, kdiff, HYENA_TC)
    hy_out = _hyena(z_hy, conv_w[0], conv_b[0], ffwd, finv, kp, kq, hy_skip[0], HYENA_TC)

    bands = _bias_bands(na_rpb[0], kh)
    band_of_row = lambda r: jnp.clip(r - kh // 2, 0, rows - kh) - r + (NA_KH - 1)
    na_out, w_out_b, wq_b = _attention(qkv, kv_ctx, bands, band_of_row, rows, kh, NA_ROWS,
                                       [w_out[0], peer_wq[0]])

    x_mid, u_peer = _merge(hy_out, na_out, x, mod, w_out_b, hy_norm_g[0], na_norm_g[0],
                           ln1_g[0], ln1_b[0], alpha, MERGE_TM)

    t = bsz * seq
    u2 = u_peer.reshape(t, d)
    idx_t, gate_t, eu_b, ev_b = _route(u2, wq_b, peer_keys[0], peer_u[0], peer_v[0], ROUTE_TM)
    w_gate = _gate_matrix(idx_t.reshape(-1, t).T, gate_t.reshape(-1, t).T, GATE_TW)
    f = _peer(u2, eu_b, ev_b, w_gate, PEER_TM, PEER_TE, PEER_CHUNK)
    out = _final_ln(x_mid.reshape(t, d), f, mod, ln2_g[0], ln2_b[0], alpha, seq, FINAL_TM)
    return out.reshape(bsz, seq, d)
```

```python
import functools
import math

import numpy as np
import jax
import jax.numpy as jnp
from jax import lax
from jax.experimental import pallas as pl
from jax.experimental.pallas import tpu as pltpu

F32 = jnp.float32
BF16 = jnp.bfloat16
HIGHEST = lax.Precision.HIGHEST

GRID_W = 64
HEAD_DIM = 64
D_HY = 1024
HY_ORDER = 2
FILTER_EMB = 33
FILTER_PAD = 128
DECAY_TARGET = 1e-2
FAST_DECAY_PCT = 0.3
SLOW_DECAY_PCT = 1.5
NA_KH = 8
NA_KW = 16
PEER_HEADS = 8
PEER_NKEYS = 128
NKEYS_SHIFT = PEER_NKEYS.bit_length() - 1
assert 1 << NKEYS_SHIFT == PEER_NKEYS
PEER_TOPK = 16
LN_EPS = 1e-5
NEG_INF = -1e30

V7X_VMEM_BYTES = 64 * 1024 * 1024
V7X_LANES = 128
V7X_MXU_DEPTH = 256
NA_GROUP = V7X_MXU_DEPTH // HEAD_DIM


MIB = 1024 * 1024
VMEM_TEMPORARIES = 12 * MIB
VMEM_RESERVED = 6 * MIB

ADA_TN = 2048
PROJ_TM = 512
CTX_TM = 256
HYENA_TC = 256
NA_ROWS = 2
MERGE_TM = 256
ROUTE_TM = 1024
GATE_TW = 128
PEER_TM, PEER_TE, PEER_CHUNK = 2048, 512, 512
FINAL_TM = 512


def _vmem_limit(estimate_bytes):
    return int(min(estimate_bytes + VMEM_TEMPORARIES, V7X_VMEM_BYTES - VMEM_RESERVED))


def _params(semantics, estimate_bytes):
    return pltpu.CompilerParams(dimension_semantics=semantics, vmem_limit_bytes=_vmem_limit(estimate_bytes))


def _ada_kernel(cond_ref, w_ref, b_ref, o_ref):
    s = jax.nn.silu(cond_ref[...])
    o_ref[...] = jnp.dot(s, w_ref[...], precision=HIGHEST, preferred_element_type=F32) + b_ref[...]


def _ada(cond, w, b):
    n, d = cond.shape
    tn = ADA_TN
    return pl.pallas_call(
        _ada_kernel,
        grid=(w.shape[1] // tn,),
        in_specs=[pl.BlockSpec((n, d), lambda j: (0, 0)),
                  pl.BlockSpec((d, tn), lambda j: (0, j)),
                  pl.BlockSpec((1, tn), lambda j: (0, j))],
        out_specs=pl.BlockSpec((n, tn), lambda j: (0, j)),
        out_shape=jax.ShapeDtypeStruct((n, w.shape[1]), F32),
        compiler_params=_params(("arbitrary",), 2 * d * tn * 4),
        name="ada",
    )(cond, w, b.reshape(1, -1))


def _rider_specs(riders, n_steps, step_of):
    specs = []
    for w in riders:
        slab = w.shape[0] // n_steps
        assert slab * n_steps == w.shape[0] and slab % 16 == 0
        specs.append(pl.BlockSpec((slab, w.shape[1]), lambda *g: (step_of(*g), 0)))
    return specs


def _rider_bytes(riders, n_steps):
    return sum(2 * (w.size // n_steps) * 6 for w in riders)


def _convert_riders(src_refs, dst_refs):
    for src_ref, dst_ref in zip(src_refs, dst_refs):
        dst_ref[...] = src_ref[...].astype(dst_ref.dtype)


def _proj_kernel(x_ref, mod_ref, w_ref, *rest, n_riders):
    o_ref = rest[n_riders]
    _convert_riders(rest[:n_riders], rest[n_riders + 1:])
    u = x_ref[...] * (1.0 + mod_ref[1:2, :]) + mod_ref[0:1, :]
    o_ref[...] = jnp.dot(u.astype(BF16), w_ref[...], preferred_element_type=F32).astype(o_ref.dtype)


def _proj(x, mod, mod_row, w, n, col_block, out_dtype, tm, riders=()):
    bsz, seq, d = x.shape
    n_i = seq // tm
    slabs = _rider_specs(riders, bsz * n_i, lambda b, i: b * n_i + i)
    est = d * n * 2 + 2 * tm * d * 4 + 2 * tm * n * 4 + tm * n * 4 + _rider_bytes(riders, bsz * n_i)
    return pl.pallas_call(
        functools.partial(_proj_kernel, n_riders=len(riders)),
        grid=(bsz, n_i),
        in_specs=[pl.BlockSpec((None, tm, d), lambda b, i: (b, i, 0)),
                  pl.BlockSpec((None, 6, d), lambda b, i: (mod_row(b), 0, 0)),
                  pl.BlockSpec((d, n), lambda b, i: (0, col_block), pipeline_mode=pl.Buffered(1))] + slabs,
        out_specs=[pl.BlockSpec((None, tm, n), lambda b, i: (b, i, 0))] + slabs,
        out_shape=[jax.ShapeDtypeStruct((bsz, seq, n), out_dtype)]
        + [jax.ShapeDtypeStruct(r.shape, BF16) for r in riders],
        compiler_params=_params(("arbitrary", "arbitrary"), est),
        name="proj",
    )(x, mod, w, *riders)


def _filter_kernel(z_ref, w1_ref, b1_ref, w2_ref, b2_ref, w3_ref, b3_ref, fr_ref, w4f_ref, w4b_ref, dl_ref,
                   *rest, n_riders):
    ks_ref, kd_ref = rest[n_riders:n_riders + 2]
    h_ref = rest[-1]
    _convert_riders(rest[:n_riders], rest[n_riders + 2:-1])
    dot = functools.partial(jnp.dot, precision=HIGHEST, preferred_element_type=F32)

    @pl.when((pl.program_id(0) == 0) & (pl.program_id(1) == 0))
    def _():
        h = jnp.sin(fr_ref[0:1, :] * (dot(z_ref[...], w1_ref[...]) + b1_ref[...]))
        h = jnp.sin(fr_ref[1:2, :] * (dot(h, w2_ref[...]) + b2_ref[...]))
        h_ref[...] = jnp.sin(fr_ref[2:3, :] * (dot(h, w3_ref[...]) + b3_ref[...]))

    h = h_ref[...]
    decay = jnp.exp(-z_ref[:, 0:1] * dl_ref[...])
    hf = dot(h, w4f_ref[...]) * decay
    hb = dot(h, w4b_ref[...]) * decay
    norm = jnp.sum(jnp.abs(hf) + jnp.abs(hb), axis=0, keepdims=True)
    hf = hf / norm
    hb = hb / norm
    row = lax.broadcasted_iota(jnp.int32, hb.shape, 0)
    hb = jnp.where(row == 0, 0.0, hb)
    ks_ref[...] = hf + hb
    kd_ref[...] = hb - hf


def _hyena_filters(zfeat, w1, b1, w2, b2, w3, b3, w4, freq, deltas, tc, riders=()):
    seq = zfeat.shape[0]
    fo = w2.shape[0]
    nct = D_HY // tc
    full = lambda a: pl.BlockSpec(a.shape, lambda o, j: (0,) * a.ndim)
    args = (zfeat, w1, b1.reshape(1, -1), w2, b2.reshape(1, -1), w3, b3.reshape(1, -1), freq)
    out_spec = pl.BlockSpec((None, seq, tc), lambda o, j: (o, 0, j))
    slabs = _rider_specs(riders, HY_ORDER * nct, lambda o, j: o * nct + j)
    return pl.pallas_call(
        functools.partial(_filter_kernel, n_riders=len(riders)),
        grid=(HY_ORDER, nct),
        in_specs=[full(a) for a in args] + [
            pl.BlockSpec((fo, tc), lambda o, j: (0, (2 * o) * nct + j)),
            pl.BlockSpec((fo, tc), lambda o, j: (0, (2 * o + 1) * nct + j)),
            pl.BlockSpec((1, tc), lambda o, j: (0, j))] + slabs,
        out_specs=[out_spec, out_spec] + slabs,
        out_shape=[jax.ShapeDtypeStruct((HY_ORDER, seq, D_HY), F32)] * 2
        + [jax.ShapeDtypeStruct(r.shape, BF16) for r in riders],
        scratch_shapes=[pltpu.VMEM((seq, fo), F32)],
        compiler_params=_params(("arbitrary", "arbitrary"),
                                12 * seq * tc * 4 + _rider_bytes(riders, HY_ORDER * nct)),
        name="hyena_filter",
    )(*args, w4, w4, deltas, *riders)


def _spectrum_kernel(ff_ref, ks_ref, kd_ref, kp_ref, kq_ref):
    seq = ks_ref.shape[0]
    ks = ks_ref[...].astype(BF16)
    kd = kd_ref[...].astype(BF16)
    p = jnp.dot(ff_ref[0:seq, :], ks, preferred_element_type=F32)
    q = jnp.dot(ff_ref[seq:2 * seq, :], kd, preferred_element_type=F32)
    nyq = jnp.dot(ff_ref[seq:seq + 16, :], ks, preferred_element_type=F32)[0:1, :]
    row = lax.broadcasted_iota(jnp.int32, p.shape, 0)
    wf = jnp.where(row == 0, 0.5 / seq, 1.0 / seq)
    kp_ref[...] = p * wf
    kq_ref[...] = jnp.where(row == 0, nyq, q) * wf


def _spectrum(ffwd, ksum, kdiff, tc):
    seq = ksum.shape[1]
    spec = pl.BlockSpec((None, seq, tc), lambda o, j: (o, 0, j))
    return pl.pallas_call(
        _spectrum_kernel,
        grid=(HY_ORDER, D_HY // tc),
        in_specs=[pl.BlockSpec(ffwd.shape, lambda o, j: (0, 0)), spec, spec],
        out_specs=[spec, spec],
        out_shape=[jax.ShapeDtypeStruct(ksum.shape, F32)] * 2,
        compiler_params=_params(("arbitrary", "arbitrary"), ffwd.size * 2 + 12 * seq * tc * 4),
        name="hyena_spectrum",
    )(ffwd, ksum, kdiff)


def _short_conv(z, w, b):
    seq = z.shape[0]
    row = lax.broadcasted_iota(jnp.int32, z.shape, 0)
    prev = jnp.where(row == 0, 0.0, pltpu.roll(z, 1, 0))
    nxt = jnp.where(row == seq - 1, 0.0, pltpu.roll(z, seq - 1, 0))
    return prev * w[0:1, :] + z * w[1:2, :] + nxt * w[2:3, :] + b


def _dft_mul(u, ff_ref, kp_ref, kq_ref, o_ref):
    seq = u.shape[0]
    a = jnp.dot(ff_ref[...], u.astype(BF16), preferred_element_type=F32)
    p, q = a[0:seq], a[seq:2 * seq]
    kp, kq = kp_ref[...], kq_ref[...]
    row0 = lax.broadcasted_iota(jnp.int32, p.shape, 0) == 0
    o_ref[0:seq, :] = jnp.where(row0, kp * p, kp * p + kq * q).astype(o_ref.dtype)
    o_ref[seq:2 * seq, :] = jnp.where(row0, kq * q, kp * q - kq * p).astype(o_ref.dtype)


def _fwd_conv_kernel(z_ref, cw_ref, cb_ref, ff_ref, kp_ref, kq_ref, o_ref):
    _dft_mul(_short_conv(z_ref[...], cw_ref[...], cb_ref[...]), ff_ref, kp_ref, kq_ref, o_ref)


def _fwd_plain_kernel(u_ref, ff_ref, kp_ref, kq_ref, o_ref):
    _dft_mul(u_ref[...], ff_ref, kp_ref, kq_ref, o_ref)


def _inv_first_kernel(y_ref, fi_ref, zv_ref, zx_ref, cwv_ref, cbv_ref, cwx_ref, cbx_ref, skip_ref, o_ref):
    v = _short_conv(zv_ref[...], cwv_ref[...], cbv_ref[...])
    x1 = _short_conv(zx_ref[...], cwx_ref[...], cbx_ref[...])
    y = jnp.dot(fi_ref[...], y_ref[...], preferred_element_type=F32)
    o_ref[...] = x1 * (y + v * skip_ref[...])


def _inv_second_kernel(y_ref, fi_ref, g_ref, zx_ref, cwx_ref, cbx_ref, skip_ref, o_ref):
    x2 = _short_conv(zx_ref[...], cwx_ref[...], cbx_ref[...])
    y = jnp.dot(fi_ref[...], y_ref[...], preferred_element_type=F32)
    o_ref[...] = x2 * (y + g_ref[...] * skip_ref[...])


def _hyena(z, conv_w, conv_b, ffwd, finv, kp, kq, skip, tc):
    bsz, seq, _ = z.shape
    nct = D_HY // tc
    sem = ("arbitrary", "arbitrary")
    zcol = lambda g: pl.BlockSpec((None, seq, tc), lambda b, j: (b, 0, g * nct + j))
    wcol = lambda g: pl.BlockSpec((3, tc), lambda b, j: (0, g * nct + j))
    bcol = lambda g: pl.BlockSpec((1, tc), lambda b, j: (0, g * nct + j))
    act = pl.BlockSpec((None, seq, tc), lambda b, j: (b, 0, j))
    spec2 = pl.BlockSpec((None, 2 * seq, tc), lambda b, j: (b, 0, j))
    kspec = lambda o: pl.BlockSpec((None, seq, tc), lambda b, j: (o, 0, j))
    sspec = lambda o: pl.BlockSpec((None, 1, tc), lambda b, j: (o, 0, j))
    mat = pl.BlockSpec(ffwd.shape, lambda b, j: (0, 0))
    mati = pl.BlockSpec(finv.shape, lambda b, j: (0, 0))
    est = ffwd.size * 2 + 14 * seq * tc * 4
    cb = conv_b.reshape(1, -1)
    skip3 = skip.reshape(HY_ORDER, 1, D_HY)
    spec_shape = jax.ShapeDtypeStruct((bsz, 2 * seq, D_HY), BF16)
    act_shape = jax.ShapeDtypeStruct((bsz, seq, D_HY), F32)

    y1 = pl.pallas_call(
        _fwd_conv_kernel, grid=(bsz, nct),
        in_specs=[zcol(0), wcol(0), bcol(0), mat, kspec(0), kspec(0)],
        out_specs=spec2, out_shape=spec_shape, compiler_params=_params(sem, est), name="hyena_fwd1",
    )(z, conv_w, cb, ffwd, kp, kq)
    g = pl.pallas_call(
        _inv_first_kernel, grid=(bsz, nct),
        in_specs=[spec2, mati, zcol(0), zcol(1), wcol(0), bcol(0), wcol(1), bcol(1), sspec(0)],
        out_specs=act, out_shape=act_shape, compiler_params=_params(sem, est), name="hyena_inv1",
    )(y1, finv, z, z, conv_w, cb, conv_w, cb, skip3)
    y2 = pl.pallas_call(
        _fwd_plain_kernel, grid=(bsz, nct),
        in_specs=[act, mat, kspec(1), kspec(1)],
        out_specs=spec2, out_shape=spec_shape, compiler_params=_params(sem, est), name="hyena_fwd2",
    )(g, ffwd, kp, kq)
    return pl.pallas_call(
        _inv_second_kernel, grid=(bsz, nct),
        in_specs=[spec2, mati, act, zcol(2), wcol(2), bcol(2), sspec(1)],
        out_specs=act, out_shape=act_shape, compiler_params=_params(sem, est), name="hyena_inv2",
    )(y2, finv, g, z, conv_w, cb, skip3)


def _bias_kernel(rpb_ref, o_ref, *, kh):
    h = pl.program_id(0)
    n_r, n_c = 2 * NA_KH - 1, 2 * NA_KW - 1
    cq = lax.broadcasted_iota(jnp.int32, (GRID_W, GRID_W), 0)
    ck = lax.broadcasted_iota(jnp.int32, (GRID_W, GRID_W), 1)
    rel = ck - jnp.clip(cq - NA_KW // 2, 0, GRID_W - NA_KW)
    ok = (rel >= 0) & (rel < NA_KW)
    dcol = jnp.clip(ck - cq + NA_KW - 1, 0, n_c - 1)
    tables = []
    for r in range(n_r):
        acc = jnp.zeros((GRID_W, GRID_W), F32)
        for c in range(n_c):
            acc = jnp.where(dcol == c, rpb_ref[(h * n_r + r) * n_c + c], acc)
        tables.append(jnp.where(ok, acc, NEG_INF))
    for off in range(o_ref.shape[0]):
        for i in range(kh):
            o_ref[off, :, i * GRID_W:(i + 1) * GRID_W] = tables[off + i]


def _bias_bands(rpb, kh):
    heads = rpb.shape[0]
    n_off = 2 * NA_KH - kh
    return pl.pallas_call(
        functools.partial(_bias_kernel, kh=kh),
        grid=(heads,),
        in_specs=[pl.BlockSpec(memory_space=pltpu.SMEM)],
        out_specs=pl.BlockSpec((None, n_off, GRID_W, kh * GRID_W), lambda h: (h, 0, 0, 0)),
        out_shape=jax.ShapeDtypeStruct((heads, n_off, GRID_W, kh * GRID_W), F32),
        compiler_params=_params(("arbitrary",), 4 << 20),
        name="na_bias",
    )(rpb.reshape(-1))


def _attn_kernel(q_ref, k_ref, v_ref, kc_ref, vc_ref, *rest, rows, kh, n_rows, n_riders):
    bias_refs, rest = rest[:n_rows], rest[n_rows:]
    o_ref = rest[n_riders]
    _convert_riders(rest[:n_riders], rest[n_riders + 1:])
    first_row = n_rows * pl.program_id(1)
    nk = kh * GRID_W
    gw = NA_GROUP * HEAD_DIM
    lane = lax.broadcasted_iota(jnp.int32, (1, gw), 1)
    head_of_lane = [(lane >= e * HEAD_DIM) & (lane < (e + 1) * HEAD_DIM) for e in range(NA_GROUP)]
    nt = (((1,), (1,)), ((), ()))
    starts = [pl.multiple_of(jnp.clip(first_row + i - kh // 2, 0, rows - kh) * GRID_W, GRID_W)
              for i in range(n_rows)]
    per_row = NA_GROUP * GRID_W
    zero = jnp.zeros((GRID_W, gw), BF16)
    for hg in range(q_ref.shape[1] // gw):
        cols = slice(hg * gw, (hg + 1) * gw)
        qp = q_ref[:, cols] * jnp.asarray(HEAD_DIM ** -0.5, BF16)
        qm = jnp.concatenate([jnp.where(head_of_lane[e], qp[i * GRID_W:(i + 1) * GRID_W], zero)
                              for i in range(n_rows) for e in range(NA_GROUP)], axis=0)
        s_ctx = lax.dot_general(qm, kc_ref[:, cols], nt, preferred_element_type=F32)
        m_ctx = jnp.max(s_ctx, axis=-1, keepdims=True)
        s_lat, m_all = [], []
        for i in range(n_rows):
            blk = slice(i * per_row, (i + 1) * per_row)
            bias = jnp.concatenate([bias_refs[i][hg * NA_GROUP + e] for e in range(NA_GROUP)], axis=0)
            s = lax.dot_general(qm[blk], k_ref[pl.ds(starts[i], nk), cols], nt, preferred_element_type=F32) + bias
            s_lat.append(s)
            m_all.append(jnp.maximum(jnp.max(s, axis=-1, keepdims=True), m_ctx[blk]))
        p_ctx = jnp.exp(s_ctx - jnp.concatenate(m_all, axis=0))
        o_ctx = jnp.dot(p_ctx.astype(BF16), vc_ref[:, cols], preferred_element_type=F32)
        l_ctx = jnp.sum(p_ctx, axis=-1, keepdims=True)
        for i in range(n_rows):
            blk = slice(i * per_row, (i + 1) * per_row)
            p = jnp.exp(s_lat[i] - m_all[i])
            o = jnp.dot(p.astype(BF16), v_ref[pl.ds(starts[i], nk), cols], preferred_element_type=F32) + o_ctx[blk]
            o = o / (jnp.sum(p, axis=-1, keepdims=True) + l_ctx[blk])
            out = o[0:GRID_W]
            for e in range(1, NA_GROUP):
                out = jnp.where(head_of_lane[e], o[e * GRID_W:(e + 1) * GRID_W], out)
            o_ref[i * GRID_W:(i + 1) * GRID_W, cols] = out


def _attention(qkv, kv_ctx, bands, band_of_row, rows, kh, n_rows, riders):
    bsz, seq, d3 = qkv.shape
    dn = d3 // 3
    ctx_len = kv_ctx.shape[1]
    heads, _, _, nk = bands.shape
    assert rows % n_rows == 0
    n_steps = bsz * (rows // n_rows)
    slab_specs = _rider_specs(riders, n_steps, lambda b, r: b * (rows // n_rows) + r)
    est = (2 * (2 * seq * dn * 2 + 2 * ctx_len * dn * 2 + n_rows * heads * GRID_W * nk * 4 + n_rows * GRID_W * dn * 6)
           + _rider_bytes(riders, n_steps))
    band = lambda i: pl.BlockSpec((heads, None, GRID_W, nk), lambda b, r: (0, band_of_row(n_rows * r + i), 0, 0))
    return pl.pallas_call(
        functools.partial(_attn_kernel, rows=rows, kh=kh, n_rows=n_rows, n_riders=len(riders)),
        grid=(bsz, rows // n_rows),
        in_specs=[pl.BlockSpec((None, n_rows * GRID_W, dn), lambda b, r: (b, r, 0)),
                  pl.BlockSpec((None, seq, dn), lambda b, r: (b, 0, 1)),
                  pl.BlockSpec((None, seq, dn), lambda b, r: (b, 0, 2)),
                  pl.BlockSpec((None, ctx_len, dn), lambda b, r: (b, 0, 0)),
                  pl.BlockSpec((None, ctx_len, dn), lambda b, r: (b, 0, 1)),
                  *[band(i) for i in range(n_rows)]] + slab_specs,
        out_specs=[pl.BlockSpec((None, n_rows * GRID_W, dn), lambda b, r: (b, r, 0))] + slab_specs,
        out_shape=[jax.ShapeDtypeStruct((bsz, seq, dn), F32)]
        + [jax.ShapeDtypeStruct(w.shape, BF16) for w in riders],
        compiler_params=_params(("arbitrary", "arbitrary"), est),
        name="na_attention",
    )(qkv, qkv, qkv, kv_ctx, kv_ctx, *([bands] * n_rows), *riders)


def _rms(x, g):
    return x * lax.rsqrt(jnp.mean(x * x, axis=-1, keepdims=True) + LN_EPS) * g


def _layer_norm(x, g, b):
    xc = x - jnp.mean(x, axis=-1, keepdims=True)
    var = jnp.mean(xc * xc, axis=-1, keepdims=True)
    return xc * lax.rsqrt(var + LN_EPS) * g + b


def _merge_kernel(hy_ref, na_ref, x_ref, mod_ref, w_ref, ghy_ref, gna_ref, lg_ref, lb_ref, xo_ref, uo_ref, *, alpha):
    dh = hy_ref.shape[1]
    hy = _rms(hy_ref[...], ghy_ref[...]).astype(BF16)
    na = _rms(na_ref[...], gna_ref[...]).astype(BF16)
    y = (jnp.dot(hy, w_ref[0:dh, :], preferred_element_type=F32)
         + jnp.dot(na, w_ref[dh:, :], preferred_element_type=F32))
    xn = _layer_norm(alpha * x_ref[...] + mod_ref[2:3, :] * y, lg_ref[...], lb_ref[...])
    xo_ref[...] = xn
    uo_ref[...] = (xn * (1.0 + mod_ref[4:5, :]) + mod_ref[3:4, :]).astype(uo_ref.dtype)


def _merge(hy, na, x, mod, w_out, g_hy, g_na, ln_g, ln_b, alpha, tm):
    bsz, seq, d = x.shape
    dh, dn = hy.shape[2], na.shape[2]
    row = lambda a: a.reshape(1, -1)
    vec = lambda n: pl.BlockSpec((1, n), lambda b, i: (0, 0))
    tile = lambda n: pl.BlockSpec((None, tm, n), lambda b, i: (b, i, 0))
    est = d * d * 2 + 2 * tm * (dh + dn + 2 * d) * 4 + 2 * tm * d * 6 + 4 * tm * d * 4
    return pl.pallas_call(
        functools.partial(_merge_kernel, alpha=alpha),
        grid=(bsz, seq // tm),
        in_specs=[tile(dh), tile(dn), tile(d),
                  pl.BlockSpec((None, 6, d), lambda b, i: (b, 0, 0)),
                  pl.BlockSpec((d, d), lambda b, i: (0, 0)),
                  vec(dh), vec(dn), vec(d), vec(d)],
        out_specs=[tile(d), tile(d)],
        out_shape=[jax.ShapeDtypeStruct((bsz, seq, d), F32), jax.ShapeDtypeStruct((bsz, seq, d), BF16)],
        compiler_params=_params(("arbitrary", "arbitrary"), est),
        name="merge",
    )(hy, na, x, mod, w_out, row(g_hy), row(g_na), row(ln_g), row(ln_b))


def _first(a, b):
    return (a[0] > b[0]) | ((a[0] == b[0]) & (a[1] < b[1]))


def _order_pair(xs, i, j):
    a, b = xs[i], xs[j]
    f = _first(a, b)
    xs[i] = (jnp.maximum(a[0], b[0]),) + tuple(jnp.where(f, p, r) for p, r in zip(a[1:], b[1:]))
    xs[j] = (jnp.minimum(a[0], b[0]),) + tuple(jnp.where(f, r, p) for p, r in zip(a[1:], b[1:]))


def _bitonic_merge(xs):
    n = len(xs)
    j = n // 2
    while j >= 1:
        for i in range(n):
            if i & j == 0:
                _order_pair(xs, i, i | j)
        j //= 2
    return xs


def _sort(xs):
    n = len(xs)
    p = 1
    while p < n:
        k = p
        while k >= 1:
            for j in range(k % p, n - k, 2 * k):
                for i in range(min(k, n - j - k)):
                    if (i + j) // (2 * p) == (i + j + k) // (2 * p):
                        _order_pair(xs, i + j, i + j + k)
            k //= 2
        p *= 2
    return xs


def _leading_half(a, b):
    out = []
    for x, y in zip(a, reversed(b)):
        f = _first(x, y)
        out.append((jnp.maximum(x[0], y[0]),) + tuple(jnp.where(f, p, r) for p, r in zip(x[1:], y[1:])))
    return out


def _sorted_top(items, k):
    runs = [_sort(items[i:i + k]) for i in range(0, len(items), k)]
    while len(runs) > 1:
        runs = [_bitonic_merge(_leading_half(runs[i], runs[i + 1])) for i in range(0, len(runs), 2)]
    return runs[0]


def _pitch(rows):
    return rows + 8


def _route_kernel(u_ref, wq_ref, keys_ref, eu_ref, ev_ref, idx_ref, gate_ref, eub_ref, evb_ref, s_ref):
    eub_ref[...] = eu_ref[...].astype(eub_ref.dtype)
    evb_ref[...] = ev_ref[...].astype(evb_ref.dtype)
    nkeys, half = keys_ref.shape[1], keys_ref.shape[2]
    k = PEER_TOPK
    groups = u_ref.shape[0] // V7X_LANES
    pitch = _pitch(nkeys)
    tile = (groups, V7X_LANES)
    q = jnp.dot(u_ref[...], wq_ref[...], preferred_element_type=F32)
    nt = (((1,), (1,)), ((), ()))

    def split(v):
        hi = v.astype(BF16)
        return hi, (v - hi.astype(F32)).astype(BF16)

    for p in range(2):
        k_hi, k_lo = split(keys_ref[p])
        q_hi, q_lo = split(q[:, p * half:(p + 1) * half])
        dot = lambda a, b: lax.dot_general(a, b, nt, preferred_element_type=F32)
        s = dot(k_hi, q_hi) + (dot(k_hi, q_lo) + dot(k_lo, q_hi))
        for g in range(groups):
            s_ref[p, g * pitch:g * pitch + nkeys, :] = s[:, g * V7X_LANES:(g + 1) * V7X_LANES]

    def scores(p):
        return [(s_ref[p, pl.ds(n, groups, stride=pitch), :], jnp.full(tile, float(n), F32)) for n in range(nkeys)]

    top_a = _sorted_top(scores(0), k)
    top_b = _sorted_top(scores(1), k)

    def cand(i, j):
        return (top_a[i][0] + top_b[j][0], jnp.full(tile, float(i * k + j), F32),
                top_a[i][1] * float(nkeys) + top_b[j][1])

    pad = (jnp.full(tile, -jnp.inf, F32), jnp.full(tile, float(k * k), F32), jnp.zeros(tile, F32))
    run = lambda i: [cand(i, j) for j in range(k // (i + 1))]
    tail = [cand(i, 0) for i in range(k // 2, k)]
    assert k == 16
    r0 = run(0)
    r1 = _bitonic_merge(run(1) + tail[::-1])
    r2 = _sort(run(2) + run(3) + run(4) + run(5) + run(6))
    r3 = run(7) + [pad] * (k - 2)
    best = _leading_half(_bitonic_merge(_leading_half(r0, r1)), _bitonic_merge(_leading_half(r2, r3)))

    m = functools.reduce(jnp.maximum, [c[0] for c in best])
    e = [jnp.exp(c[0] - m) for c in best]
    inv = 1.0 / functools.reduce(jnp.add, e)
    for n in range(k):
        idx_ref[n] = best[n][2].astype(jnp.int32)
        gate_ref[n] = e[n] * inv


def _route(u, wq, keys, eu, ev, tm):
    t, d = u.shape
    heads, _, nkeys, half = keys.shape
    groups = tm // V7X_LANES
    n_steps = (t // tm) * heads
    n_exp, de = eu.shape
    slab = n_exp // n_steps
    assert slab * n_steps == n_exp and slab % 16 == 0
    est = (2 * (tm * d * 2 + d * 2 * half * 2 + 2 * nkeys * half * 4) + 8 * nkeys * tm * 4
           + 2 * 2 * slab * de * 6)
    out_spec = pl.BlockSpec((PEER_TOPK, groups, V7X_LANES), lambda i, h: (h, i, 0))
    slab_spec = pl.BlockSpec((slab, de), lambda i, h: (i * heads + h, 0))
    return pl.pallas_call(
        _route_kernel,
        grid=(t // tm, heads),
        in_specs=[pl.BlockSpec((tm, d), lambda i, h: (i, 0)),
                  pl.BlockSpec((d, 2 * half), lambda i, h: (0, h)),
                  pl.BlockSpec((None, 2, nkeys, half), lambda i, h: (h, 0, 0, 0)),
                  slab_spec, slab_spec],
        out_specs=[out_spec, out_spec, slab_spec, slab_spec],
        out_shape=[jax.ShapeDtypeStruct((heads * PEER_TOPK, t // V7X_LANES, V7X_LANES), jnp.int32),
                   jax.ShapeDtypeStruct((heads * PEER_TOPK, t // V7X_LANES, V7X_LANES), F32),
                   jax.ShapeDtypeStruct(eu.shape, BF16), jax.ShapeDtypeStruct(ev.shape, BF16)],
        scratch_shapes=[pltpu.VMEM((2, groups * _pitch(nkeys), V7X_LANES), F32)],
        compiler_params=_params(("arbitrary", "arbitrary"), est),
        name="peer_route",
    )(u, wq, keys, eu, ev)


def _gate_matrix_kernel(idx_ref, gate_ref, o_ref, w0_ref, w1_ref):
    step = pl.program_id(0)

    @pl.when(step == 0)
    def _():
        w1_ref[...] = jnp.zeros_like(w1_ref)

    @pl.when(step % 2 == 0)
    def _():
        _gate_matrix_step(idx_ref, gate_ref, o_ref, w0_ref, w1_ref)

    @pl.when(step % 2 == 1)
    def _():
        _gate_matrix_step(idx_ref, gate_ref, o_ref, w1_ref, w0_ref)


def _gate_matrix_step(idx_ref, gate_ref, o_ref, cur_ref, prev_ref):
    tw, nsel = idx_ref.shape
    pitch = _pitch(PEER_NKEYS)
    key = lax.broadcasted_iota(jnp.int32, (PEER_NKEYS, nsel), 0)
    nt = (((1,), (1,)), ((), ()))
    zero = jnp.zeros((PEER_NKEYS, nsel), BF16)
    n_pairs = tw // 2
    planes_per_pair = PEER_NKEYS // n_pairs
    for p in range(n_pairs):
        a_hot, b_hot = [], []
        for t in (2 * p, 2 * p + 1):
            idx = idx_ref[t:t + 1, :]
            gate = gate_ref[t:t + 1, :]
            a_hot.append(jnp.where((idx >> NKEYS_SHIFT) == key, gate, 0.0).astype(BF16))
            b_hot.append(jnp.where((idx & (PEER_NKEYS - 1)) == key, 1.0, 0.0).astype(BF16))
        lhs = jnp.concatenate(a_hot, axis=1)
        rhs = jnp.concatenate([jnp.concatenate([b_hot[0], zero], axis=1),
                               jnp.concatenate([zero, b_hot[1]], axis=1)], axis=0)
        planes = lax.dot_general(lhs, rhs, nt, preferred_element_type=F32)
        for k in range(2):
            row = (2 * p + k) * pitch
            cur_ref[row:row + PEER_NKEYS, :] = planes[:, k * PEER_NKEYS:(k + 1) * PEER_NKEYS]
        for a in range(p * planes_per_pair, (p + 1) * planes_per_pair):
            o_ref[:, a * PEER_NKEYS:(a + 1) * PEER_NKEYS] = (
                prev_ref[pl.ds(a, tw, stride=pitch), :].astype(o_ref.dtype))


def _gate_matrix(idx, gate, tw):
    t, nsel = idx.shape
    n_exp = PEER_NKEYS * PEER_NKEYS
    n_blocks = t // tw
    assert PEER_NKEYS % (tw // 2) == 0
    est = 2 * (2 * tw * nsel * 4 + tw * n_exp * 2) + 2 * tw * _pitch(PEER_NKEYS) * PEER_NKEYS * 4
    block_in = pl.BlockSpec((tw, nsel), lambda i: (jnp.minimum(i, n_blocks - 1), 0))
    return pl.pallas_call(
        _gate_matrix_kernel,
        grid=(n_blocks + 1,),
        in_specs=[block_in, block_in],
        out_specs=pl.BlockSpec((tw, n_exp), lambda i: (jnp.maximum(i - 1, 0), 0)),
        out_shape=jax.ShapeDtypeStruct((t, n_exp), BF16),
        scratch_shapes=[pltpu.VMEM((tw * _pitch(PEER_NKEYS), PEER_NKEYS), F32)] * 2,
        compiler_params=_params(("arbitrary",), est),
        name="peer_gate_matrix",
    )(idx, gate)


def _peer_kernel(u_ref, eu_ref, ev_ref, w_ref, o_ref, *, n_chunk):
    @pl.when(pl.program_id(1) == 0)
    def _():
        o_ref[...] = jnp.zeros_like(o_ref)

    z = lax.dot_general(u_ref[...], eu_ref[...], (((1,), (1,)), ((), ())), preferred_element_type=F32)
    act = 0.5 * z * (1.0 + lax.erf(z * (2.0 ** -0.5)))
    h = (w_ref[...].astype(F32) * act).astype(BF16)
    for c in range(0, o_ref.shape[1], n_chunk):
        o_ref[:, c:c + n_chunk] += jnp.dot(h, ev_ref[:, c:c + n_chunk], preferred_element_type=F32)


def _final_ln_kernel(x_ref, f_ref, mod_ref, lg_ref, lb_ref, o_ref, *, alpha):
    r = alpha * x_ref[...] + mod_ref[5:6, :] * f_ref[...]
    o_ref[...] = _layer_norm(r, lg_ref[...], lb_ref[...])


def _final_ln(x, f, mod, ln_g, ln_b, alpha, seq, tm):
    t, d = x.shape
    tile = pl.BlockSpec((tm, d), lambda i: (i, 0))
    vec = pl.BlockSpec((1, d), lambda i: (0, 0))
    return pl.pallas_call(
        functools.partial(_final_ln_kernel, alpha=alpha),
        grid=(t // tm,),
        in_specs=[tile, tile, pl.BlockSpec((None, 6, d), lambda i: ((i * tm) // seq, 0, 0)), vec, vec],
        out_specs=tile,
        out_shape=jax.ShapeDtypeStruct((t, d), F32),
        compiler_params=_params(("arbitrary",), 8 * tm * d * 4),
        name="final_ln",
    )(x, f, mod, ln_g.reshape(1, -1), ln_b.reshape(1, -1))


def _peer(u, eu, ev, w, tm, te, n_chunk):
    t, d = u.shape
    n_exp = ev.shape[0]
    once = pl.Buffered(1)
    est = tm * d * 2 + tm * d * 4 + 2 * (2 * te * d * 2 + tm * te * 2) + 4 * tm * te * 4 + tm * n_chunk * 4
    return pl.pallas_call(
        functools.partial(_peer_kernel, n_chunk=n_chunk),
        grid=(t // tm, n_exp // te),
        in_specs=[pl.BlockSpec((tm, d), lambda i, j: (i, 0), pipeline_mode=once),
                  pl.BlockSpec((te, d), lambda i, j: (j, 0)),
                  pl.BlockSpec((te, d), lambda i, j: (j, 0)),
                  pl.BlockSpec((tm, te), lambda i, j: (i, j))],
        out_specs=pl.BlockSpec((tm, d), lambda i, j: (i, 0), pipeline_mode=once),
        out_shape=jax.ShapeDtypeStruct((t, d), F32),
        compiler_params=_params(("arbitrary", "arbitrary"), est),
        name="peer_experts",
    )(u, eu, ev, w)


def _filter_features(seq):
    t = jnp.linspace(0.0, 1.0, seq, dtype=F32)[:, None]
    bands = (FILTER_EMB - 1) // 2
    w = 2.0 * math.pi * jnp.arange(seq, dtype=F32)[:, None] / seq
    f = jnp.linspace(1e-4, bands - 1, bands, dtype=F32)[None, :]
    z = jnp.concatenate([t, jnp.cos(f * w), -jnp.sin(f * w)], axis=-1)
    return jnp.pad(z, ((0, 0), (0, FILTER_PAD - FILTER_EMB)))


@functools.lru_cache(maxsize=None)
def _dft_matrices(seq):
    n2 = 2 * seq
    f = np.arange(seq, dtype=np.int64)[:, None]
    n = np.arange(seq, dtype=np.int64)[None, :]
    ang = ((f * n) % n2).astype(np.float64) * (2.0 * math.pi / n2)
    sin = np.where(f == 0, (1 - 2 * (n % 2)).astype(np.float64), np.sin(ang))
    return np.concatenate([np.cos(ang), sin], axis=0).astype(np.float32)


def kernel(x, c, ctx, c_ctx, w_ada, b_ada, w_in, conv_w, conv_b, filt_w1, filt_b1, filt_w2, filt_b2,
           filt_w3, filt_b3, filt_w4, filt_freq, hy_skip, na_rpb, hy_norm_g, na_norm_g, w_out,
           ln1_g, ln1_b, peer_wq, peer_keys, peer_u, peer_v, ln2_g, ln2_b):
    depth = w_ada.shape[0]
    assert depth == 1, "single-layer block: the context stream is never updated"
    bsz, seq, d = x.shape
    rows = seq // GRID_W
    kh = min(NA_KH, rows)
    assert kh == NA_KH and seq % GRID_W == 0
    alpha = (2.0 * depth) ** 0.25
    d_na = d - D_HY
    off_q = 3 * D_HY
    off_k = off_q + d_na

    n_cond = 8
    cond = jnp.concatenate([c, c_ctx[None], jnp.zeros((n_cond - bsz - 1, d), F32)], axis=0)
    mod = _ada(cond, w_ada[0], b_ada[0]).reshape(n_cond, 6, d)

    deltas = jnp.abs(jnp.linspace(math.log(DECAY_TARGET) / SLOW_DECAY_PCT,
                                  math.log(DECAY_TARGET) / FAST_DECAY_PCT, D_HY, dtype=F32))[None, :]
    padw = lambda a: jnp.pad(a, ((0, FILTER_PAD - a.shape[0]), (0, FILTER_PAD - a.shape[1])))
    padv = lambda a: jnp.pad(a, ((0, FILTER_PAD - a.shape[0]),))
    w4 = jnp.pad(filt_w4[0], ((0, FILTER_PAD - filt_w4.shape[1]), (0, 0)))
    freq = jnp.pad(filt_freq[0], ((0, 0), (0, FILTER_PAD - filt_freq.shape[2])))
    ksum, kdiff, w_in_b = _hyena_filters(
        _filter_features(seq), padw(filt_w1[0]), padv(filt_b1[0]), padw(filt_w2[0]), padv(filt_b2[0]),
        padw(filt_w3[0]), padv(filt_b3[0]), w4, freq, deltas, HYENA_TC, [w_in[0]])

    assert off_q == 3 * d_na and off_k == 2 * (2 * d_na)
    (z_hy,) = _proj(x, mod, lambda b: b, w_in_b, off_q, 0, F32, PROJ_TM)
    (qkv,) = _proj(x, mod, lambda b: b, w_in_b, 3 * d_na, 1, BF16, PROJ_TM)
    (kv_ctx,) = _proj(ctx, mod, lambda b: bsz, w_in_b, 2 * d_na, 2, BF16, CTX_TM)

    ffwd = jnp.asarray(_dft_matrices(seq)).astype(BF16)
    finv = ffwd.T
    kp, kq = _spectrum(ffwd, ksum, kdiff, HYENA_TC)
    hy_out = _hyena(z_hy, conv_w[0], conv_b[0], ffwd, finv, kp, kq, hy_skip[0], HYENA_TC)

    bands = _bias_bands(na_rpb[0], kh)
    band_of_row = lambda r: jnp.clip(r - kh // 2, 0, rows - kh) - r + (NA_KH - 1)
    na_out, w_out_b, wq_b = _attention(qkv, kv_ctx, bands, band_of_row, rows, kh, NA_ROWS,
                                       [w_out[0], peer_wq[0]])

    x_mid, u_peer = _merge(hy_out, na_out, x, mod, w_out_b, hy_norm_g[0], na_norm_g[0],
                           ln1_g[0], ln1_b[0], alpha, MERGE_TM)

    t = bsz * seq
    u2 = u_peer.reshape(t, d)
    idx_t, gate_t, eu_b, ev_b = _route(u2, wq_b, peer_keys[0], peer_u[0], peer_v[0], ROUTE_TM)
    w_gate = _gate_matrix(idx_t.reshape(-1, t).T, gate_t.reshape(-1, t).T, GATE_TW)
    f = _peer(u2, eu_b, ev_b, w_gate, PEER_TM, PEER_TE, PEER_CHUNK)
    out = _final_ln(x_mid.reshape(t, d), f, mod, ln2_g[0], ln2_b[0], alpha, seq, FINAL_TM)
    return out.reshape(bsz, seq, d)
```
